```python
import math
import jax
import jax.numpy as jnp
from jax import lax
import numpy as np

D_MODEL = 1024
BATCH = 2
SEQ = 16384
DEPTH = 4

GRID_W = 64
CTX_LEN = 256
EPS = 1e-6
F32 = jnp.float32
POOL_WINDOWS = (2, 4, 8, 16)
N_POOL_GROUPS = len(POOL_WINDOWS)
D_POOL = D_MODEL // 4
POOL_GROUP = D_POOL // N_POOL_GROUPS
D_DIFF = D_MODEL - D_POOL
DIFF_HEAD = 64
DIFF_V = 2 * DIFF_HEAD
DIFF_HEADS = D_DIFF // DIFF_V
DIFF_SCALE = DIFF_HEAD ** -0.5
Q_BLOCK = 128
ROPE_BASE = 10000.0
ROPE_FREQS = DIFF_HEAD // 4
D_CONV = D_MODEL // 2
CONV_WIDTH = 31
D_HGRN = D_MODEL - D_CONV
HGRN_HEAD = 128
HGRN_HEADS = D_HGRN // HGRN_HEAD
CHUNK = 64
N_EVEN = (DEPTH + 1) // 2
N_ODD = DEPTH // 2
EVEN_SPLITS = (D_POOL, D_POOL, D_DIFF, D_DIFF, D_DIFF, D_DIFF)
ODD_SPLITS = (D_CONV, D_CONV, D_CONV, D_HGRN, D_HGRN, D_HGRN, D_HGRN, D_HGRN)
D_IN_EVEN = sum(EVEN_SPLITS)
D_IN_ODD = sum(ODD_SPLITS)
D_MIX = D_POOL + D_DIFF

kernel_name = 'hybrid_pool_diffattn_conformer_hgrn2_prefix_dit'


def rmsnorm(x, g):
    xf = x.astype(F32)
    y = xf * lax.rsqrt(jnp.mean(xf * xf, axis=-1, keepdims=True) + EPS)
    return (y * g.astype(F32)).astype(x.dtype)


def layernorm(x, g, b):
    xf = x.astype(F32)
    xc = xf - jnp.mean(xf, axis=-1, keepdims=True)
    y = xc * lax.rsqrt(jnp.mean(xc * xc, axis=-1, keepdims=True) + EPS)
    return (y * g.astype(F32) + b.astype(F32)).astype(x.dtype)


def split_cols(p, sizes):
    return jnp.split(p, [int(s) for s in np.cumsum(sizes)[:-1]], axis=-1)


def pool_mix(u, w_pool, scale):
    B, L, _ = u.shape
    uf = u.astype(F32)
    cs = jnp.concatenate([jnp.zeros((B, 1, D_POOL), F32), jnp.cumsum(uf, axis=1)], axis=1)
    t = np.arange(L)
    outs = []
    for g, w in enumerate(POOL_WINDOWS):
        lo = np.maximum(t - w // 2, 0)
        hi = np.minimum(t + w // 2 - 1, L - 1) + 1
        cnt = (hi - lo).astype(np.float32)
        sl = slice(g * POOL_GROUP, (g + 1) * POOL_GROUP)
        csg = cs[..., sl]
        outs.append((csg[:, hi] - csg[:, lo]) / cnt[None, :, None] - uf[..., sl])
    d = jnp.stack(outs, axis=2)
    y = jnp.einsum('blgc,gcd->blgd', d, w_pool.astype(F32)).reshape(B, L, D_POOL)
    return (y * scale.astype(F32)).astype(u.dtype)


def axial_rope(n_lat):
    rows = n_lat // GRID_W
    row = jnp.repeat(jnp.arange(rows), GRID_W)
    col = jnp.arange(rows * GRID_W) % GRID_W
    inv = ROPE_BASE ** (-jnp.arange(ROPE_FREQS, dtype=F32) / ROPE_FREQS)
    ang = jnp.stack([row, col], axis=-1).astype(F32)[:, :, None] * inv
    return jnp.cos(ang), jnp.sin(ang)


def apply_rope(t, cos, sin):
    tf = t.astype(F32).reshape(t.shape[:-1] + (2, 2, ROPE_FREQS))
    t1, t2 = tf[..., 0, :], tf[..., 1, :]
    out = jnp.stack([t1 * cos - t2 * sin, t2 * cos + t1 * sin], axis=-2)
    return out.reshape(t.shape).astype(t.dtype)


def diff_heads(q, k, v):
    B, L, _ = q.shape
    qh = q.reshape(B, L, DIFF_HEADS, 2, DIFF_HEAD).transpose(0, 2, 3, 1, 4)
    kh = k.reshape(B, L, DIFF_HEADS, 2, DIFF_HEAD).transpose(0, 2, 3, 1, 4)
    vh = v.reshape(B, L, DIFF_HEADS, DIFF_V).transpose(0, 2, 1, 3)
    return qh, kh, vh


def diff_softmax_mix(qb, k_all, v_all, lam):
    s = jnp.einsum('bhmqd,bhmkd->bhmqk', qb, k_all).astype(F32) * DIFF_SCALE
    p = jax.nn.softmax(s, axis=-1)
    a = p[:, :, 0] - lam * p[:, :, 1]
    return jnp.einsum('bhqk,bhkv->bhqv', a, v_all.astype(F32))


def diff_attention(q, k, v, qc, kc, vc, lam, with_ctx):
    B, L, _ = q.shape
    qh, kh, vh = diff_heads(q, k, v)
    qch, kch, vch = diff_heads(qc, kc, vc)
    cos, sin = axial_rope(L)
    qh = apply_rope(qh, cos, sin)
    kh = apply_rope(kh, cos, sin)
    k_all = jnp.concatenate([kch, kh], axis=3)
    v_all = jnp.concatenate([vch, vh], axis=2)
    nb = L // Q_BLOCK
    qb = jnp.moveaxis(qh.reshape(B, DIFF_HEADS, 2, nb, Q_BLOCK, DIFF_HEAD), 3, 0)
    ob = lax.map(lambda blk: diff_softmax_mix(blk, k_all, v_all, lam), qb)
    o = jnp.moveaxis(ob, 0, 2).reshape(B, DIFF_HEADS, L, DIFF_V)
    oc = diff_softmax_mix(qch, kch, vch, lam) if with_ctx else None
    return o, oc


def diff_post(o, g, lam_init):
    B, H, L, _ = o.shape
    return (rmsnorm(o, g) * (1.0 - lam_init)).transpose(0, 2, 1, 3).reshape(B, L, H * DIFF_V)


def even_mixer(h, hc, w_in, w_out, pool_w, pool_scale, lam_p, subln_g, lam_init, with_ctx):
    u, ga, q, k, v, gb = split_cols(h @ w_in, EVEN_SPLITS)
    uc, gac, qc, kc, vc, gbc = split_cols(hc @ w_in, EVEN_SPLITS)
    lp = lam_p.astype(F32)
    lam = jnp.exp(jnp.sum(lp[0] * lp[1])) - jnp.exp(jnp.sum(lp[2] * lp[3])) + lam_init
    o, oc = diff_attention(q, k, v, qc, kc, vc, lam, with_ctx)
    y_a = pool_mix(u, pool_w, pool_scale) * jax.nn.silu(ga)
    y_b = diff_post(o, subln_g, lam_init).astype(h.dtype) * jax.nn.silu(gb)
    y = jnp.concatenate([y_a, y_b], axis=-1) @ w_out
    if not with_ctx:
        return y, None
    yc_a = pool_mix(uc, pool_w, pool_scale) * jax.nn.silu(gac)
    yc_b = diff_post(oc, subln_g, lam_init).astype(hc.dtype) * jax.nn.silu(gbc)
    yc = jnp.concatenate([yc_a, yc_b], axis=-1) @ w_out
    return y, yc


def conv_module(a, b, gate, conv_w, conv_b, ln_g, ln_b):
    glu = a * jax.nn.sigmoid(b)
    z = lax.conv_general_dilated(glu, conv_w[:, None, :], window_strides=(1,),
                                 padding=[(CONV_WIDTH // 2, CONV_WIDTH // 2)],
                                 dimension_numbers=('NWC', 'WIO', 'NWC'),
                                 feature_group_count=D_CONV) + conv_b
    z = layernorm(z, ln_g, ln_b)
    return jax.nn.silu(z) * jax.nn.silu(gate)


def to_chunks(t):
    B, L, H, d = t.shape
    return t.reshape(B, L // CHUNK, CHUNK, H, d).transpose(1, 0, 3, 2, 4)


def from_chunks(t):
    nc, B, H, C, d = t.shape
    return t.transpose(1, 0, 3, 2, 4).reshape(B, nc * C, H, d)


def hgrn_scan(q, lf, k, v, s0):
    tri = jnp.tril(jnp.ones((CHUNK, CHUNK), dtype=bool))

    def step(S, inp):
        qt, lft, kt, vt = inp
        b = jnp.cumsum(lft, axis=2)
        inter = jnp.einsum('bhtc,bhcv->bhtv', qt * jnp.exp(b), S)
        rel = jnp.where(tri[:, :, None], b[:, :, :, None, :] - b[:, :, None, :, :], -jnp.inf)
        att = jnp.einsum('bhtsc,bhsc->bhts', qt[:, :, :, None, :] * jnp.exp(rel), kt)
        intra = jnp.einsum('bhts,bhsv->bhtv', att, vt)
        bl = b[:, :, -1:, :]
        S_new = jnp.exp(bl[:, :, 0, :])[..., None] * S + jnp.einsum('bhsc,bhsv->bhcv', kt * jnp.exp(bl - b), vt)
        return S_new, inter + intra

    s_fin, o = lax.scan(step, s0, (to_chunks(q), to_chunks(lf), to_chunks(k), to_chunks(v)))
    return s_fin, from_chunks(o)


def hgrn_gates(logit, lb):
    f = lb + (1.0 - lb) * jax.nn.sigmoid(logit)
    return jnp.log(f), 1.0 - f


def hgrn_bidir(q, ff, fb, v, qc, ffc, fbc, vc, lb, with_ctx):
    B = q.shape[0]
    s0 = jnp.zeros((B, HGRN_HEADS, HGRN_HEAD, HGRN_HEAD), F32)
    flip = lambda t: jnp.flip(t, axis=1)
    lf_f, k_f = hgrn_gates(ff, lb[0])
    lf_b, k_b = hgrn_gates(fb, lb[1])
    lfc_f, kc_f = hgrn_gates(ffc, lb[0])
    lfc_b, kc_b = hgrn_gates(fbc, lb[1])
    sc_f, oc_f = hgrn_scan(qc, lfc_f, kc_f, vc, s0)
    sc_b, oc_b = hgrn_scan(flip(qc), flip(lfc_b), flip(kc_b), flip(vc), s0)
    _, o_f = hgrn_scan(q, lf_f, k_f, v, sc_f)
    _, o_b = hgrn_scan(flip(q), flip(lf_b), flip(k_b), flip(v), sc_b)
    o = o_f + flip(o_b)
    oc = oc_f + flip(oc_b) if with_ctx else None
    return o, oc


def odd_mixer(h, hc, w_in, w_out, conv_w, conv_b, ln_g, ln_b, o_norm, lb, with_ctx):
    B, L, _ = h.shape
    Lc = hc.shape[1]
    ca, cb, cg, q, ff, fb, iv, og = split_cols(h @ w_in, ODD_SPLITS)
    cca, ccb, ccg, qc, ffc, fbc, ivc, ogc = split_cols(hc @ w_in, ODD_SPLITS)
    heads = lambda t: t.reshape(t.shape[0], t.shape[1], HGRN_HEADS, HGRN_HEAD).astype(F32)
    o, oc = hgrn_bidir(heads(jax.nn.silu(q)), heads(ff), heads(fb), heads(iv),
                       heads(jax.nn.silu(qc)), heads(ffc), heads(fbc), heads(ivc), lb, with_ctx)
    g_heads = o_norm.reshape(HGRN_HEADS, HGRN_HEAD)
    y_d = rmsnorm(o, g_heads).reshape(B, L, D_HGRN).astype(h.dtype) * jax.nn.silu(og)
    y_c = conv_module(ca, cb, cg, conv_w, conv_b, ln_g, ln_b)
    y = jnp.concatenate([y_c, y_d], axis=-1) @ w_out
    if not with_ctx:
        return y, None
    yc_d = rmsnorm(oc, g_heads).reshape(B, Lc, D_HGRN).astype(hc.dtype) * jax.nn.silu(ogc)
    yc_c = conv_module(cca, ccb, ccg, conv_w, conv_b, ln_g, ln_b)
    yc = jnp.concatenate([yc_c, yc_d], axis=-1) @ w_out
    return y, yc


def setup_inputs(seed: int = 0) -> dict:
    key = jax.random.key(seed)
    ks = jax.random.split(key, 22)

    def nrm(k, shape, s):
        return jax.random.normal(k, shape, F32) * s

    d = D_MODEL
    return {
        'x': nrm(ks[0], (BATCH, SEQ, d), 1.0),
        'c': nrm(ks[1], (BATCH, d), 1.0),
        'ctx': nrm(ks[2], (BATCH, CTX_LEN, d), 1.0),
        'c_ctx': nrm(ks[3], (d,), 1.0),
        'ada_w': nrm(ks[4], (DEPTH, d, 3 * d), 0.5 * d ** -0.5),
        'ada_b': nrm(ks[5], (DEPTH, 3 * d), 0.02),
        'norm_pre': 1.0 + nrm(ks[6], (DEPTH, d), 0.05),
        'norm_post': 1.0 + nrm(ks[7], (DEPTH, d), 0.05),
        'w_in_even': nrm(ks[8], (N_EVEN, d, D_IN_EVEN), d ** -0.5),
        'w_out_even': nrm(ks[9], (N_EVEN, D_MIX, d), D_MIX ** -0.5),
        'pool_w': nrm(ks[10], (N_EVEN, N_POOL_GROUPS, POOL_GROUP, POOL_GROUP), POOL_GROUP ** -0.5),
        'pool_scale': 1.0 + nrm(ks[11], (N_EVEN, D_POOL), 0.05),
        'diff_lambda': nrm(ks[12], (N_EVEN, 4, DIFF_HEAD), 0.1),
        'diff_subln': 1.0 + nrm(ks[13], (N_EVEN, DIFF_V), 0.05),
        'w_in_odd': nrm(ks[14], (N_ODD, d, D_IN_ODD), d ** -0.5),
        'w_out_odd': nrm(ks[15], (N_ODD, D_MIX, d), D_MIX ** -0.5),
        'conv_w': nrm(ks[16], (N_ODD, CONV_WIDTH, D_CONV), CONV_WIDTH ** -0.5),
        'conv_b': nrm(ks[17], (N_ODD, D_CONV), 0.02),
        'conv_ln_g': 1.0 + nrm(ks[18], (N_ODD, D_CONV), 0.05),
        'conv_ln_b': nrm(ks[19], (N_ODD, D_CONV), 0.02),
        'hgrn_norm': 1.0 + nrm(ks[20], (N_ODD, D_HGRN), 0.05),
        'hgrn_lb': 1.0 + nrm(ks[21], (2, DEPTH, D_HGRN), 0.5),
    }


def reference(x, c, ctx, c_ctx, ada_w, ada_b, norm_pre, norm_post, w_in_even, w_out_even,
              pool_w, pool_scale, diff_lambda, diff_subln, w_in_odd, w_out_odd,
              conv_w, conv_b, conv_ln_g, conv_ln_b, hgrn_norm, hgrn_lb):
    c_act = jax.nn.silu(c)
    cc_act = jax.nn.silu(c_ctx)
    lbs = jnp.cumsum(jax.nn.softmax(hgrn_lb.astype(F32), axis=1), axis=1)
    lbs = lbs - lbs[:, :1]
    for l in range(DEPTH):
        with_ctx = l < DEPTH - 1
        shift, scale, gate = jnp.split(c_act @ ada_w[l] + ada_b[l], 3, axis=-1)
        shift_c, scale_c, gate_c = jnp.split(cc_act @ ada_w[l] + ada_b[l], 3, axis=-1)
        h = rmsnorm(x, norm_pre[l]) * (1.0 + scale[:, None]) + shift[:, None]
        hc = rmsnorm(ctx, norm_pre[l]) * (1.0 + scale_c) + shift_c
        j = l // 2
        if l % 2 == 0:
            lam_init = 0.8 - 0.6 * math.exp(-0.3 * l)
            y, yc = even_mixer(h, hc, w_in_even[j], w_out_even[j], pool_w[j], pool_scale[j],
                               diff_lambda[j], diff_subln[j], lam_init, with_ctx)
        else:
            lb = lbs[:, l].reshape(2, HGRN_HEADS, HGRN_HEAD)
            y, yc = odd_mixer(h, hc, w_in_odd[j], w_out_odd[j], conv_w[j], conv_b[j],
                              conv_ln_g[j], conv_ln_b[j], hgrn_norm[j], lb, with_ctx)
        x = x + gate[:, None] * rmsnorm(y, norm_post[l])
        if with_ctx:
            ctx = ctx + gate_c * rmsnorm(yc, norm_post[l])
    return x
```

```python
import functools
import math

import numpy as np
import jax
import jax.numpy as jnp
from jax import lax
from jax.experimental import pallas as pl
from jax.experimental.pallas import tpu as pltpu

F32 = jnp.float32
BF16 = jnp.bfloat16
EPS = 1e-6

GRID_W = 64
POOL_WINDOWS = (2, 4, 8, 16)
POOL_GROUP = 64
DIFF_HEAD = 64
DIFF_V = 2 * DIFF_HEAD
DIFF_SCALE = DIFF_HEAD ** -0.5
ROPE_BASE = 10000.0
ROPE_FREQS = DIFF_HEAD // 4
CONV_WIDTH = 31
CONV_HALO = 16
POOL_HALO = 8
HGRN_HEAD = 128
LANES = 128
SUBLANES = 8

ROW_TILE = 256
Q_TILE = 256
KV_TILE = 256
HGRN_CHUNK = 64
VMEM_LIMIT = 48 * 1024 * 1024


def _cparams(sem):
    return pltpu.CompilerParams(dimension_semantics=sem, vmem_limit_bytes=VMEM_LIMIT)


def _silu(x):
    return x * jax.nn.sigmoid(x)


def _bdot(a, b):
    return jnp.dot(a.astype(BF16), b.astype(BF16), preferred_element_type=F32)


def _ada_kernel(c_ref, w_ref, b_ref, o_ref):
    c = c_ref[...]
    o_ref[0] = jnp.dot(_silu(c), w_ref[0], preferred_element_type=F32,
                       precision=lax.Precision.HIGHEST) + b_ref[0]


def _ada_call(cvec, ada_w, ada_b):
    depth, d, d3 = ada_w.shape
    nj = d3 // d
    return pl.pallas_call(
        _ada_kernel,
        grid=(depth, nj),
        in_specs=[pl.BlockSpec((SUBLANES, d), lambda l, j: (0, 0)),
                  pl.BlockSpec((1, d, d), lambda l, j: (l, 0, j)),
                  pl.BlockSpec((1, 1, d), lambda l, j: (l, 0, j))],
        out_specs=pl.BlockSpec((1, SUBLANES, d), lambda l, j: (l, 0, j)),
        out_shape=jax.ShapeDtypeStruct((depth, SUBLANES, d3), F32),
        name="ada_params",
        compiler_params=_cparams(("parallel", "parallel")),
    )(cvec, ada_w, ada_b.reshape(depth, 1, d3))


def _prenorm(x_ref, g_ref, mod_ref):
    x = x_ref[0]
    mod = mod_ref[0, 0]
    ms = jnp.mean(x * x, axis=-1, keepdims=True)
    h = x * lax.rsqrt(ms + EPS) * g_ref[...] * (1.0 + mod[1:2]) + mod[0:1]
    return h.astype(BF16)


def _postnorm_residual(x_ref, y, gpost_ref, mod_ref, o_ref):
    mod = mod_ref[0, 0]
    ms = jnp.mean(y * y, axis=-1, keepdims=True)
    yn = y * lax.rsqrt(ms + EPS) * gpost_ref[...]
    o_ref[0] = x_ref[0] + mod[2:3] * yn


def _row_specs(tm, ncb):
    def rows(width):
        return pl.BlockSpec((1, tm, width), lambda b, i: (b, i, 0))

    def full2(a, bdim):
        return pl.BlockSpec((a, bdim), lambda b, i: (0, 0))

    def mod(d):
        return pl.BlockSpec((1, 1, 3, d), lambda b, i: (b, jnp.where(i < ncb, 0, 1), 0, 0))

    return rows, full2, mod


def _inproj_even_kernel(x_ref, g_ref, mod_ref, w_ref, cos_ref, sin_ref,
                        u_ref, sga_ref, q_ref, k_ref, v_ref, sgb_ref, *, d_pool, d_diff):
    hb = _prenorm(x_ref, g_ref, mod_ref)

    def proj(c0, width):
        return jnp.dot(hb, w_ref[:, c0:c0 + width], preferred_element_type=F32)

    u_ref[0] = proj(0, d_pool)
    sga_ref[0] = _silu(proj(d_pool, d_pool))
    cos = cos_ref[...]
    sin = sin_ref[...]
    lane = lax.broadcasted_iota(jnp.int32, cos.shape, 1)
    first_half = (lane & ROPE_FREQS) == 0

    def rope(t):
        partner = jnp.where(first_half, pltpu.roll(t, LANES - ROPE_FREQS, 1), pltpu.roll(t, ROPE_FREQS, 1))
        return t * cos + partner * sin

    q0 = 2 * d_pool
    k0 = q0 + d_diff
    for s in range(d_diff // LANES):
        sl = slice(s * LANES, (s + 1) * LANES)
        q_ref[0, :, sl] = (rope(proj(q0 + s * LANES, LANES)) * DIFF_SCALE).astype(BF16)
        k_ref[0, :, sl] = rope(proj(k0 + s * LANES, LANES)).astype(BF16)
    v_ref[0] = proj(k0 + d_diff, d_diff).astype(BF16)
    sgb_ref[0] = _silu(proj(k0 + 2 * d_diff, d_diff)).astype(BF16)


def _inproj_even(xc, g, mod, w, cos, sin, ncb, d_pool, d_diff):
    bsz, ltot, d = xc.shape
    tm = ROW_TILE
    rows, full2, modspec = _row_specs(tm, ncb)
    tab = pl.BlockSpec((tm, LANES), lambda b, i: (i, 0))
    shp = lambda width, dt: jax.ShapeDtypeStruct((bsz, ltot, width), dt)
    return pl.pallas_call(
        functools.partial(_inproj_even_kernel, d_pool=d_pool, d_diff=d_diff),
        grid=(bsz, ltot // tm),
        in_specs=[rows(d), full2(1, d), modspec(d), full2(*w.shape), tab, tab],
        out_specs=[rows(d_pool), rows(d_pool), rows(d_diff), rows(d_diff), rows(d_diff), rows(d_diff)],
        out_shape=[shp(d_pool, F32), shp(d_pool, F32), shp(d_diff, BF16), shp(d_diff, BF16),
                   shp(d_diff, BF16), shp(d_diff, BF16)],
        name="inproj_even",
        compiler_params=_cparams(("parallel", "parallel")),
    )(xc, g.reshape(1, d), mod, w, cos, sin)


def _attn_kernel(q_ref, k_ref, v_ref, sgb_ref, lam_ref, sub_ref, o_ref, m_sc, l_sc, acc_sc,
                 *, tq, tk, n_ctx_q, n_ctx_kv, n_kv, lam_init):
    i = pl.program_id(2)
    q = q_ref[0]
    lane = lax.broadcasted_iota(jnp.int32, q.shape, 1)
    zero = jnp.zeros_like(q)
    qs = jnp.concatenate([jnp.where(lane < DIFF_HEAD, q, zero), jnp.where(lane >= DIFF_HEAD, q, zero)], axis=0)
    m_sc[...] = jnp.full(m_sc.shape, -jnp.inf, F32)
    l_sc[...] = jnp.zeros(l_sc.shape, F32)
    acc_sc[...] = jnp.zeros(acc_sc.shape, F32)
    steps = jnp.where(i < n_ctx_q, n_ctx_kv, n_kv)

    def body(j, carry):
        off = pl.multiple_of(j * tk, tk)
        kc = k_ref[0, pl.ds(off, tk), :]
        vc = v_ref[0, pl.ds(off, tk), :]
        s = lax.dot_general(qs, kc, (((1,), (1,)), ((), ())), preferred_element_type=F32)
        m_prev = m_sc[...]
        m_new = jnp.maximum(m_prev, jnp.max(s, axis=1, keepdims=True))
        alpha = jnp.exp(m_prev - m_new)
        p = jnp.exp(s - m_new)
        l_sc[...] = alpha * l_sc[...] + jnp.sum(p, axis=1, keepdims=True)
        acc_sc[...] = alpha * acc_sc[...] + jnp.dot(p.astype(BF16), vc, preferred_element_type=F32)
        m_sc[...] = m_new
        return carry

    lax.fori_loop(0, steps, body, 0)

    lp = lam_ref[...]
    lam = (jnp.exp(jnp.sum(lp[0:1] * lp[1:2], axis=1, keepdims=True))
           - jnp.exp(jnp.sum(lp[2:3] * lp[3:4], axis=1, keepdims=True)) + lam_init)
    o_all = acc_sc[...] / l_sc[...]
    o = o_all[:tq] - lam * o_all[tq:]
    ms = jnp.mean(o * o, axis=-1, keepdims=True)
    on = o * lax.rsqrt(ms + EPS) * sub_ref[...] * (1.0 - lam_init)
    o_ref[0] = (on.astype(F32) * sgb_ref[0].astype(F32)).astype(BF16)


def _attention(q, k, v, sgb, lam_p, subln, lc, lam_init):
    bsz, ltot, d_diff = q.shape
    heads = d_diff // DIFF_V
    tq, tk = Q_TILE, KV_TILE
    assert lc % tq == 0 and lc % tk == 0 and ltot % tq == 0 and ltot % tk == 0
    qspec = pl.BlockSpec((1, tq, DIFF_V), lambda b, h, i: (b, i, h))
    kvspec = pl.BlockSpec((1, ltot, DIFF_V), lambda b, h, i: (b, 0, h))
    kern = functools.partial(_attn_kernel, tq=tq, tk=tk, n_ctx_q=lc // tq, n_ctx_kv=lc // tk,
                             n_kv=ltot // tk, lam_init=lam_init)
    return pl.pallas_call(
        kern,
        grid=(bsz, heads, ltot // tq),
        in_specs=[qspec, kvspec, kvspec, qspec,
                  pl.BlockSpec(lam_p.shape, lambda b, h, i: (0, 0)),
                  pl.BlockSpec((1, DIFF_V), lambda b, h, i: (0, 0))],
        out_specs=qspec,
        out_shape=jax.ShapeDtypeStruct((bsz, ltot, d_diff), BF16),
        scratch_shapes=[pltpu.VMEM((2 * tq, 1), F32), pltpu.VMEM((2 * tq, 1), F32),
                        pltpu.VMEM((2 * tq, DIFF_V), F32)],
        name="diff_attention",
        compiler_params=_cparams(("parallel", "parallel", "arbitrary")),
    )(q, k, v, sgb, lam_p, subln.reshape(1, DIFF_V))


def _seq_position(i, tm, ncb, lc, ll):
    nblk = ncb + ll // tm
    is_ctx = i < ncb
    t0 = jnp.where(is_ctx, i, i - ncb) * tm
    lseq = jnp.where(is_ctx, lc, ll)
    has_prev = jnp.logical_and(i != 0, i != ncb)
    has_next = jnp.logical_and(i != ncb - 1, i != nblk - 1)
    return t0, lseq, has_prev, has_next


def _outproj_even_kernel(x_ref, u_ref, up_ref, un_ref, sga_ref, yb_ref, wpool_ref, pscale_ref, wout_ref,
                         gpost_ref, mod_ref, o_ref, *, tm, ncb, lc, ll, d_pool):
    i = pl.program_id(1)
    t0, lseq, has_prev, has_next = _seq_position(i, tm, ncb, lc, ll)
    u = u_ref[0]
    z = jnp.concatenate([jnp.where(has_prev, up_ref[0], 0.0), u, jnp.where(has_next, un_ref[0], 0.0)], axis=0)
    p2 = z[:-1] + z[1:]
    p4 = p2[:-2] + p2[2:]
    p8 = p4[:-4] + p4[4:]
    p16 = p8[:-8] + p8[8:]
    sums = (p2[7:7 + tm], p4[6:6 + tm], p8[4:4 + tm], p16[0:tm])
    lane = lax.broadcasted_iota(jnp.int32, u.shape, 1)
    grp = lane >> 6
    s = jnp.where(grp == 0, sums[0], jnp.where(grp == 1, sums[1], jnp.where(grp == 2, sums[2], sums[3])))
    half = jnp.left_shift(1, grp)
    t = lax.broadcasted_iota(jnp.int32, u.shape, 0) + t0
    cnt = jnp.minimum(t + half - 1, lseq - 1) + 1 - jnp.maximum(t - half, 0)
    dpool = s / cnt.astype(F32) - u
    ya = jnp.dot(dpool.astype(BF16), wpool_ref[...], preferred_element_type=F32) * pscale_ref[...] * sga_ref[0]
    y = (jnp.dot(ya.astype(BF16), wout_ref[0:d_pool, :], preferred_element_type=F32)
         + jnp.dot(yb_ref[0], wout_ref[d_pool:, :], preferred_element_type=F32))
    _postnorm_residual(x_ref, y, gpost_ref, mod_ref, o_ref)


def _halo_specs(tm, halo, width, ltot):
    r = tm // halo
    prev = pl.BlockSpec((1, halo, width), lambda b, i: (b, jnp.maximum(i * r - 1, 0), 0))
    nxt = pl.BlockSpec((1, halo, width), lambda b, i: (b, jnp.minimum((i + 1) * r, ltot // halo - 1), 0))
    return prev, nxt


def _outproj_even(xc, u, sga, yb, wpool, pscale, wout, gpost, mod, ncb, lc):
    bsz, ltot, d = xc.shape
    tm = ROW_TILE
    d_pool = u.shape[-1]
    rows, full2, modspec = _row_specs(tm, ncb)
    prev, nxt = _halo_specs(tm, POOL_HALO, d_pool, ltot)
    kern = functools.partial(_outproj_even_kernel, tm=tm, ncb=ncb, lc=lc, ll=ltot - lc, d_pool=d_pool)
    return pl.pallas_call(
        kern,
        grid=(bsz, ltot // tm),
        in_specs=[rows(d), rows(d_pool), prev, nxt, rows(d_pool), rows(yb.shape[-1]),
                  full2(*wpool.shape), full2(1, d_pool), full2(*wout.shape), full2(1, d), modspec(d)],
        out_specs=rows(d),
        out_shape=jax.ShapeDtypeStruct(xc.shape, F32),
        name="outproj_even",
        compiler_params=_cparams(("parallel", "parallel")),
    )(xc, u, u, u, sga, yb, wpool, pscale.reshape(1, d_pool), wout, gpost.reshape(1, d), mod)


def _inproj_odd_kernel(x_ref, g_ref, mod_ref, w_ref, glu_ref, scg_ref, qh_ref, ff_ref, fb_ref, iv_ref, sog_ref,
                       *, d_conv, d_hgrn):
    hb = _prenorm(x_ref, g_ref, mod_ref)

    def proj(c0, width):
        return jnp.dot(hb, w_ref[:, c0:c0 + width], preferred_element_type=F32)

    glu_ref[0] = proj(0, d_conv) * jax.nn.sigmoid(proj(d_conv, d_conv))
    scg_ref[0] = _silu(proj(2 * d_conv, d_conv)).astype(BF16)
    c0 = 3 * d_conv
    qh_ref[0] = _silu(proj(c0, d_hgrn)).astype(BF16)
    ff_ref[0] = proj(c0 + d_hgrn, d_hgrn)
    fb_ref[0] = proj(c0 + 2 * d_hgrn, d_hgrn)
    iv_ref[0] = proj(c0 + 3 * d_hgrn, d_hgrn).astype(BF16)
    sog_ref[0] = _silu(proj(c0 + 4 * d_hgrn, d_hgrn)).astype(BF16)


def _inproj_odd(xc, g, mod, w, ncb, d_conv, d_hgrn):
    bsz, ltot, d = xc.shape
    tm = ROW_TILE
    rows, full2, modspec = _row_specs(tm, ncb)
    shp = lambda width, dt: jax.ShapeDtypeStruct((bsz, ltot, width), dt)
    return pl.pallas_call(
        functools.partial(_inproj_odd_kernel, d_conv=d_conv, d_hgrn=d_hgrn),
        grid=(bsz, ltot // tm),
        in_specs=[rows(d), full2(1, d), modspec(d), full2(*w.shape)],
        out_specs=[rows(d_conv), rows(d_conv), rows(d_hgrn), rows(d_hgrn), rows(d_hgrn), rows(d_hgrn),
                   rows(d_hgrn)],
        out_shape=[shp(d_conv, F32), shp(d_conv, BF16), shp(d_hgrn, BF16), shp(d_hgrn, F32), shp(d_hgrn, F32),
                   shp(d_hgrn, BF16), shp(d_hgrn, BF16)],
        name="inproj_odd",
        compiler_params=_cparams(("parallel", "parallel")),
    )(xc, g.reshape(1, d), mod, w)


def _cumsum_rows(x, rev):
    n = x.shape[0]
    row = lax.broadcasted_iota(jnp.int32, x.shape, 0)
    sh = 1
    while sh < n:
        if rev:
            x = x + jnp.where(row < n - sh, pltpu.roll(x, n - sh, 0), 0.0)
        else:
            x = x + jnp.where(row >= sh, pltpu.roll(x, sh, 0), 0.0)
        sh *= 2
    return x


def _hgrn_chain(qf, logit, vb, lb, st_ref, rev):
    c = qf.shape[0]
    f = lb + (1.0 - lb) * jax.nn.sigmoid(logit)
    kk = 1.0 - f
    b = _cumsum_rows(jnp.log(f), rev)
    vf = vb.astype(F32)
    row = lax.broadcasted_iota(jnp.int32, (c, 1), 0)
    st = st_ref[...]
    inter = lax.dot_general((qf * jnp.exp(b)).astype(BF16), st.astype(BF16), (((1,), (1,)), ((), ())),
                            preferred_element_type=F32)
    ri = lax.broadcasted_iota(jnp.int32, (c, c), 0)
    ci = lax.broadcasted_iota(jnp.int32, (c, c), 1)
    att = jnp.zeros((c, c), F32)
    w = c // 2
    while w >= SUBLANES:
        refs = []
        for p in range(c // (2 * w)):
            r0 = p * 2 * w + (w if rev else w - 1)
            refs.append(jnp.broadcast_to(b[r0:r0 + 1, :], (2 * w, b.shape[1])))
        ref = refs[0] if len(refs) == 1 else jnp.concatenate(refs, axis=0)
        odd_blk = (row & w) != 0
        is_q = jnp.logical_not(odd_blk) if rev else odd_blk
        qs = jnp.where(is_q, jnp.exp(jnp.minimum(b - ref, 0.0)), 0.0) * qf
        ks = jnp.where(is_q, 0.0, jnp.exp(jnp.minimum(ref - b, 0.0))) * kk
        a = lax.dot_general(qs.astype(BF16), ks.astype(BF16), (((1,), (1,)), ((), ())),
                            preferred_element_type=F32)
        shift = int(math.log2(2 * w))
        att = att + jnp.where((ri >> shift) == (ci >> shift), a, 0.0)
        w //= 2
    out = inter + jnp.dot(att.astype(BF16), vb, preferred_element_type=F32)
    sub = row & (SUBLANES - 1)
    for d in range(SUBLANES):
        if d == 0:
            a = jnp.sum(qf * kk, axis=1, keepdims=True)
            out = out + a * vf
            continue
        sh = c - d if rev else d
        valid = (sub + d <= SUBLANES - 1) if rev else (sub >= d)
        e = jnp.exp(jnp.where(valid, b - pltpu.roll(b, sh, 0), 0.0))
        a = jnp.sum(qf * e * pltpu.roll(kk, sh, 0), axis=1, keepdims=True)
        out = out + jnp.where(valid, a, 0.0) * pltpu.roll(vf, sh, 0)
    bl = b[0:1, :] if rev else b[c - 1:c, :]
    kd = (kk * jnp.exp(bl - b)).astype(BF16)
    st_ref[...] = st * jnp.exp(bl) + lax.dot_general(vb, kd, (((0,), (0,)), ((), ())),
                                                    preferred_element_type=F32)
    return out


def _hgrn_kernel(lbraw_ref, qf_ref, ff_ref, vf_ref, qb_ref, fb_ref, vb_ref, of_ref, ob_ref, st_sc,
                 *, layer, bsz, heads):
    s = pl.program_id(0)

    @pl.when(s == 0)
    def _():
        st_sc[...] = jnp.zeros(st_sc.shape, F32)

    raw = lbraw_ref[...]
    depth = raw.shape[1]
    rows = [raw[:, j, :] for j in range(depth)]
    mx = functools.reduce(jnp.maximum, rows)
    ex = [jnp.exp(r - mx) for r in rows]
    den = functools.reduce(lambda a, bb: a + bb, ex)
    if layer >= 1:
        lbs = functools.reduce(lambda a, bb: a + bb, [e / den for e in ex[1:layer + 1]])
    else:
        lbs = jnp.zeros_like(mx)
    for bi in range(bsz):
        for h in range(heads):
            sl = slice(h * HGRN_HEAD, (h + 1) * HGRN_HEAD)
            of_ref[bi, :, sl] = _hgrn_chain(qf_ref[bi, :, sl].astype(F32), ff_ref[bi, :, sl], vf_ref[bi, :, sl],
                                            lbs[0:1, sl], st_sc.at[bi, 0, h], False)
            ob_ref[bi, :, sl] = _hgrn_chain(qb_ref[bi, :, sl].astype(F32), fb_ref[bi, :, sl], vb_ref[bi, :, sl],
                                            lbs[1:2, sl], st_sc.at[bi, 1, h], True)


def _hgrn(qh, ff, fb, iv, hgrn_lb, layer, lc):
    bsz, ltot, d_hgrn = qh.shape
    heads = d_hgrn // HGRN_HEAD
    c = HGRN_CHUNK
    assert lc % c == 0 and ltot % c == 0
    ncc, ntot = lc // c, ltot // c

    def bwd(s):
        return jnp.where(s < ncc, ncc - 1 - s, ntot - 1 + ncc - s)

    fspec = pl.BlockSpec((bsz, c, d_hgrn), lambda s: (0, s, 0))
    bspec = pl.BlockSpec((bsz, c, d_hgrn), lambda s: (0, bwd(s), 0))
    kern = functools.partial(_hgrn_kernel, layer=layer, bsz=bsz, heads=heads)
    return pl.pallas_call(
        kern,
        grid=(ntot,),
        in_specs=[pl.BlockSpec(hgrn_lb.shape, lambda s: (0, 0, 0)), fspec, fspec, fspec, bspec, bspec, bspec],
        out_specs=[fspec, bspec],
        out_shape=[jax.ShapeDtypeStruct((bsz, ltot, d_hgrn), F32)] * 2,
        scratch_shapes=[pltpu.VMEM((bsz, 2, heads, HGRN_HEAD, HGRN_HEAD), F32)],
        name="hgrn_scan",
        compiler_params=_cparams(("arbitrary",)),
    )(hgrn_lb, qh, ff, iv, qh, fb, iv)


def _outproj_odd_kernel(x_ref, glu_ref, gp_ref, gn_ref, scg_ref, cw_ref, cb_ref, lng_ref, lnb_ref,
                        of_ref, ob_ref, hn_ref, sog_ref, wout_ref, gpost_ref, mod_ref, o_ref, z_sc,
                        *, tm, ncb, lc, ll, d_conv):
    i = pl.program_id(1)
    _, _, has_prev, has_next = _seq_position(i, tm, ncb, lc, ll)
    h = CONV_HALO
    z_sc[0:h, :] = jnp.where(has_prev, gp_ref[0], 0.0)
    z_sc[h:h + tm, :] = glu_ref[0]
    z_sc[h + tm:, :] = jnp.where(has_next, gn_ref[0], 0.0)
    cw = cw_ref[...]
    base = h - CONV_WIDTH // 2
    acc = jnp.zeros((tm, d_conv), F32)
    for j in range(CONV_WIDTH):
        acc = acc + cw[j:j + 1, :] * z_sc[base + j:base + j + tm, :]
    zc = acc + cb_ref[...]
    mu = jnp.mean(zc, axis=-1, keepdims=True)
    zc = zc - mu
    var = jnp.mean(zc * zc, axis=-1, keepdims=True)
    zn = zc * lax.rsqrt(var + EPS) * lng_ref[...] + lnb_ref[...]
    yc = _silu(zn) * scg_ref[0].astype(F32)
    o = of_ref[0] + ob_ref[0]
    hn = hn_ref[...]
    parts = []
    for hd in range(o.shape[1] // HGRN_HEAD):
        sl = slice(hd * HGRN_HEAD, (hd + 1) * HGRN_HEAD)
        oh = o[:, sl]
        ms = jnp.mean(oh * oh, axis=-1, keepdims=True)
        parts.append(oh * lax.rsqrt(ms + EPS) * hn[:, sl])
    yd = jnp.concatenate(parts, axis=1) * sog_ref[0].astype(F32)
    y = (jnp.dot(yc.astype(BF16), wout_ref[0:d_conv, :], preferred_element_type=F32)
         + jnp.dot(yd.astype(BF16), wout_ref[d_conv:, :], preferred_element_type=F32))
    _postnorm_residual(x_ref, y, gpost_ref, mod_ref, o_ref)


def _outproj_odd(xc, glu, scg, cw, cb, lng, lnb, o_f, o_b, hnorm, sog, wout, gpost, mod, ncb, lc):
    bsz, ltot, d = xc.shape
    tm = ROW_TILE
    d_conv = glu.shape[-1]
    d_hgrn = o_f.shape[-1]
    rows, full2, modspec = _row_specs(tm, ncb)
    prev, nxt = _halo_specs(tm, CONV_HALO, d_conv, ltot)
    kern = functools.partial(_outproj_odd_kernel, tm=tm, ncb=ncb, lc=lc, ll=ltot - lc, d_conv=d_conv)
    return pl.pallas_call(
        kern,
        grid=(bsz, ltot // tm),
        in_specs=[rows(d), rows(d_conv), prev, nxt, rows(d_conv), full2(*cw.shape), full2(1, d_conv),
                  full2(1, d_conv), full2(1, d_conv), rows(d_hgrn), rows(d_hgrn), full2(1, d_hgrn),
                  rows(d_hgrn), full2(*wout.shape), full2(1, d), modspec(d)],
        out_specs=rows(d),
        out_shape=jax.ShapeDtypeStruct(xc.shape, F32),
        scratch_shapes=[pltpu.VMEM((tm + 2 * CONV_HALO, d_conv), F32)],
        name="outproj_odd",
        compiler_params=_cparams(("parallel", "parallel")),
    )(xc, glu, glu, glu, scg, cw, cb.reshape(1, d_conv), lng.reshape(1, d_conv), lnb.reshape(1, d_conv),
      o_f, o_b, hnorm.reshape(1, d_hgrn), sog, wout, gpost.reshape(1, d), mod)


def _rope_tables(lc, ll):
    t = jnp.arange(ll)
    inv = ROPE_BASE ** (-jnp.arange(ROPE_FREQS, dtype=F32) / ROPE_FREQS)
    ang = jnp.stack([t // GRID_W, t % GRID_W], axis=-1).astype(F32)[:, :, None] * inv
    cos = jnp.cos(ang)
    sin = jnp.sin(ang)
    cos64 = jnp.stack([cos, cos], axis=2).reshape(ll, DIFF_HEAD)
    sin64 = jnp.stack([-sin, sin], axis=2).reshape(ll, DIFF_HEAD)
    cos_t = jnp.concatenate([jnp.ones((lc, DIFF_HEAD), F32), cos64], axis=0)
    sin_t = jnp.concatenate([jnp.zeros((lc, DIFF_HEAD), F32), sin64], axis=0)
    return jnp.tile(cos_t, (1, LANES // DIFF_HEAD)), jnp.tile(sin_t, (1, LANES // DIFF_HEAD))


def _block_diag(w):
    g, a, b = w.shape
    out = jnp.zeros((g * a, g * b), w.dtype)
    for j in range(g):
        out = out.at[j * a:(j + 1) * a, j * b:(j + 1) * b].set(w[j])
    return out


def kernel(x, c, ctx, c_ctx, ada_w, ada_b, norm_pre, norm_post, w_in_even, w_out_even, pool_w, pool_scale,
           diff_lambda, diff_subln, w_in_odd, w_out_odd, conv_w, conv_b, conv_ln_g, conv_ln_b, hgrn_norm,
           hgrn_lb):
    bsz, ll, d = x.shape
    lc = ctx.shape[1]
    depth = ada_w.shape[0]
    d_pool = pool_scale.shape[-1]
    d_diff = w_out_even.shape[1] - d_pool
    d_conv = conv_b.shape[-1]
    d_hgrn = hgrn_norm.shape[-1]
    assert lc % ROW_TILE == 0 and ll % ROW_TILE == 0 and bsz + 1 <= SUBLANES
    ncb = lc // ROW_TILE

    cvec = jnp.zeros((SUBLANES, d), F32).at[:bsz].set(c).at[bsz].set(c_ctx)
    ada = _ada_call(cvec, ada_w, ada_b).reshape(depth, SUBLANES, 3, d)
    xc = jnp.concatenate([ctx, x], axis=1)
    cos, sin = _rope_tables(lc, ll)

    for l in range(depth):
        mod = jnp.stack([jnp.broadcast_to(ada[l, bsz], (bsz, 3, d)), ada[l, :bsz]], axis=1)
        j = l // 2
        if l % 2 == 0:
            lam_init = 0.8 - 0.6 * math.exp(-0.3 * l)
            u, sga, q, k, v, sgb = _inproj_even(xc, norm_pre[l], mod, w_in_even[j].astype(BF16), cos, sin, ncb,
                                                d_pool, d_diff)
            yb = _attention(q, k, v, sgb, diff_lambda[j], diff_subln[j], lc, lam_init)
            xc = _outproj_even(xc, u, sga, yb, _block_diag(pool_w[j]).astype(BF16), pool_scale[j],
                               w_out_even[j].astype(BF16), norm_post[l], mod, ncb, lc)
        else:
            glu, scg, qh, ff, fb, iv, sog = _inproj_odd(xc, norm_pre[l], mod, w_in_odd[j].astype(BF16), ncb,
                                                        d_conv, d_hgrn)
            o_f, o_b = _hgrn(qh, ff, fb, iv, hgrn_lb, l, lc)
            xc = _outproj_odd(xc, glu, scg, conv_w[j], conv_b[j], conv_ln_g[j], conv_ln_b[j], o_f, o_b,
                              hgrn_norm[j], sog, w_out_odd[j].astype(BF16), norm_post[l], mod, ncb, lc)
    return xc[:, lc:]
```

```python
import functools
import math

import numpy as np
import jax
import jax.numpy as jnp
from jax import lax
from jax.experimental import pallas as pl
from jax.experimental.pallas import tpu as pltpu

F32 = jnp.float32
BF16 = jnp.bfloat16
EPS = 1e-6

GRID_W = 64
POOL_WINDOWS = (2, 4, 8, 16)
POOL_GROUP = 64
DIFF_HEAD = 64
DIFF_V = 2 * DIFF_HEAD
DIFF_SCALE = DIFF_HEAD ** -0.5
ROPE_BASE = 10000.0
ROPE_FREQS = DIFF_HEAD // 4
CONV_WIDTH = 31
CONV_HALO = 16
POOL_HALO = 8
HGRN_HEAD = 128
LANES = 128
SUBLANES = 8

ROW_TILE = 256
Q_TILE = 128
Q_SUBTILES = 2
KV_GROUP = 4
ONES_ROWS = 16
LOG2E = 1.4426950408889634
KV_TILE = 256
HGRN_CHUNK = 64
VMEM_LIMIT = 48 * 1024 * 1024


def _cparams(sem):
    return pltpu.CompilerParams(dimension_semantics=sem, vmem_limit_bytes=VMEM_LIMIT)


def _silu(x):
    return x * jax.nn.sigmoid(x)


def _bdot(a, b):
    return jnp.dot(a.astype(BF16), b.astype(BF16), preferred_element_type=F32)


def _ada_kernel(c_ref, w_ref, b_ref, o_ref):
    c = c_ref[...]
    o_ref[0] = jnp.dot(_silu(c), w_ref[0], preferred_element_type=F32,
                       precision=lax.Precision.HIGHEST) + b_ref[0]


def _ada_call(cvec, ada_w, ada_b):
    depth, d, d3 = ada_w.shape
    nj = d3 // d
    return pl.pallas_call(
        _ada_kernel,
        grid=(depth, nj),
        in_specs=[pl.BlockSpec((SUBLANES, d), lambda l, j: (0, 0)),
                  pl.BlockSpec((1, d, d), lambda l, j: (l, 0, j)),
                  pl.BlockSpec((1, 1, d), lambda l, j: (l, 0, j))],
        out_specs=pl.BlockSpec((1, SUBLANES, d), lambda l, j: (l, 0, j)),
        out_shape=jax.ShapeDtypeStruct((depth, SUBLANES, d3), F32),
        name="ada_params",
        compiler_params=_cparams(("parallel", "parallel")),
    )(cvec, ada_w, ada_b.reshape(depth, 1, d3))


def _prenorm(x_ref, g_ref, mod_ref):
    x = x_ref[0]
    mod = mod_ref[0, 0]
    ms = jnp.mean(x * x, axis=-1, keepdims=True)
    h = x * lax.rsqrt(ms + EPS) * g_ref[...] * (1.0 + mod[1:2]) + mod[0:1]
    return h.astype(BF16)


def _postnorm_residual(x_ref, y, gpost_ref, mod_ref, o_ref):
    mod = mod_ref[0, 0]
    ms = jnp.mean(y * y, axis=-1, keepdims=True)
    yn = y * lax.rsqrt(ms + EPS) * gpost_ref[...]
    o_ref[0] = x_ref[0] + mod[2:3] * yn


def _row_specs(tm, ncb):
    def rows(width):
        return pl.BlockSpec((1, tm, width), lambda b, i: (b, i, 0))

    def full2(a, bdim):
        return pl.BlockSpec((a, bdim), lambda b, i: (0, 0))

    def mod(d):
        return pl.BlockSpec((1, 1, 3, d), lambda b, i: (b, jnp.where(i < ncb, 0, 1), 0, 0))

    return rows, full2, mod


def _inproj_even_kernel(x_ref, g_ref, mod_ref, w_ref, wt_ref, cos_ref, sin_ref, cost_ref, sint_ref,
                        u_ref, sga_ref, k_ref, sgb_ref, qt_ref, vt_ref, *, d_pool, d_diff):
    hb = _prenorm(x_ref, g_ref, mod_ref)

    def proj(c0, width):
        return jnp.dot(hb, w_ref[:, c0:c0 + width], preferred_element_type=F32)

    def proj_t(r0, height):
        return lax.dot_general(wt_ref[r0:r0 + height, :], hb, (((1,), (1,)), ((), ())),
                               preferred_element_type=F32)

    u_ref[0] = proj(0, d_pool)
    sga_ref[0] = _silu(proj(d_pool, d_pool))
    cos = cos_ref[...]
    sin = sin_ref[...]
    lane = lax.broadcasted_iota(jnp.int32, cos.shape, 1)
    first_half = (lane & ROPE_FREQS) == 0
    k0 = 2 * d_pool
    for s in range(d_diff // LANES):
        t = proj(k0 + s * LANES, LANES)
        partner = jnp.where(first_half, pltpu.roll(t, LANES - ROPE_FREQS, 1), pltpu.roll(t, ROPE_FREQS, 1))
        k_ref[0, :, s * LANES:(s + 1) * LANES] = (t * cos + partner * sin).astype(BF16)
    sgb_ref[0] = _silu(proj(k0 + d_diff, d_diff)).astype(BF16)
    cost = cost_ref[...]
    sint = sint_ref[...]
    f = ROPE_FREQS
    for s in range(d_diff // LANES):
        t = proj_t(s * LANES, LANES)
        partner = jnp.concatenate([t[(blk ^ 1) * f:((blk ^ 1) + 1) * f] for blk in range(LANES // f)], axis=0)
        qt_ref[0, s * LANES:(s + 1) * LANES, :] = ((t * cost + partner * sint) * (DIFF_SCALE * LOG2E)).astype(BF16)
    vt_ref[0] = proj_t(d_diff, d_diff).astype(BF16)


def _inproj_even(xc, g, mod, w, wt, cos, sin, ncb, d_pool, d_diff):
    bsz, ltot, d = xc.shape
    tm = ROW_TILE
    rows, full2, modspec = _row_specs(tm, ncb)
    tab = pl.BlockSpec((tm, LANES), lambda b, i: (i, 0))
    tab_t = pl.BlockSpec((LANES, tm), lambda b, i: (0, i))
    cols_t = pl.BlockSpec((1, d_diff, tm), lambda b, i: (b, 0, i))
    shp = lambda width, dt: jax.ShapeDtypeStruct((bsz, ltot, width), dt)
    shp_t = jax.ShapeDtypeStruct((bsz, d_diff, ltot), BF16)
    return pl.pallas_call(
        functools.partial(_inproj_even_kernel, d_pool=d_pool, d_diff=d_diff),
        grid=(bsz, ltot // tm),
        in_specs=[rows(d), full2(1, d), modspec(d), full2(*w.shape), full2(*wt.shape), tab, tab, tab_t, tab_t],
        out_specs=[rows(d_pool), rows(d_pool), rows(d_diff), rows(d_diff), cols_t, cols_t],
        out_shape=[shp(d_pool, F32), shp(d_pool, F32), shp(d_diff, BF16), shp(d_diff, BF16), shp_t, shp_t],
        name="inproj_even",
        compiler_params=_cparams(("parallel", "parallel")),
    )(xc, g.reshape(1, d), mod, w, wt, cos, sin, cos.T, sin.T)


def _attn_kernel(qt_ref, k_ref, vt_ref, sgb_ref, lam_ref, sub_ref, o_ref, m_sc, acc_sc, st_sc,
                 *, nsub, nq, tk, n_ctx_q, n_ctx_kv, n_kv, lam_init):
    i = pl.program_id(2)
    row = lax.broadcasted_iota(jnp.int32, (DIFF_V, nq), 0)
    ws = []
    for c in range(nsub):
        qt = qt_ref[0, :, c * nq:(c + 1) * nq]
        zero = jnp.zeros_like(qt)
        ws.append(jnp.concatenate([jnp.where(row < DIFF_HEAD, qt, zero), jnp.where(row >= DIFF_HEAD, qt, zero)],
                                  axis=1))

    ones = jnp.ones((ONES_ROWS, tk), BF16)

    def scores(off, slot):
        kc = k_ref[0, pl.ds(off, tk), :]
        for c in range(nsub):
            st_sc[slot, c] = jnp.dot(kc, ws[c], preferred_element_type=F32)

    def absorb(off, slot, ms, between=None):
        vte = jnp.concatenate([vt_ref[0, :, pl.ds(off, tk)], ones], axis=0)
        new, alphas, pvs = [], [], []
        for c in range(nsub):
            st = st_sc[slot, c]
            mx = jnp.max(st, axis=0, keepdims=True)
            m_new = mx if ms is None else jnp.maximum(ms[c], mx)
            alphas.append(None if ms is None else jnp.exp2(ms[c] - m_new))
            pvs.append(jnp.dot(vte, jnp.exp2(st - m_new).astype(BF16), preferred_element_type=F32))
            new.append(m_new)
        if between is not None:
            between()
        for c in range(nsub):
            acc_sc[c] = pvs[c] if ms is None else alphas[c] * acc_sc[c] + pvs[c]
        return tuple(new)

    ms = None
    for j in range(n_ctx_kv):
        scores(j * tk, 0)
        ms = absorb(j * tk, 0, ms)
    for c in range(nsub):
        m_sc[c] = ms[c]

    @pl.when(i >= n_ctx_q)
    def _():
        base = n_ctx_kv * tk
        last = (n_kv - 1) * tk
        g = KV_GROUP
        scores(base, 0)

        def group(jj, ms):
            offs = [pl.multiple_of(base + (g * jj + t) * tk, tk) for t in range(g)]
            offs.append(pl.multiple_of(jnp.minimum(base + g * (jj + 1) * tk, last), tk))
            scores(offs[1], 1)
            for t in range(g):
                nxt = t + 2
                ahead = (lambda o=offs[nxt], s=nxt % 2: scores(o, s)) if nxt <= g else None
                ms = absorb(offs[t], t % 2, ms, ahead)
            return ms

        lax.fori_loop(0, (n_kv - n_ctx_kv) // g, group, tuple(m_sc[c] for c in range(nsub)))

    lp = lam_ref[...]
    lam = (jnp.exp(jnp.sum(lp[0:1] * lp[1:2], axis=1, keepdims=True))
           - jnp.exp(jnp.sum(lp[2:3] * lp[3:4], axis=1, keepdims=True)) + lam_init)
    for c in range(nsub):
        acc = acc_sc[c]
        o_all = acc[:DIFF_V] / acc[DIFF_V:DIFF_V + 1]
        o = (o_all[:, :nq] - lam * o_all[:, nq:]).T
        ms = jnp.mean(o * o, axis=-1, keepdims=True)
        on = o * lax.rsqrt(ms + EPS) * sub_ref[...] * (1.0 - lam_init)
        o_ref[0, c * nq:(c + 1) * nq, :] = (on * sgb_ref[0, c * nq:(c + 1) * nq, :].astype(F32)).astype(BF16)


def _attention(qt, k, vt, sgb, lam_p, subln, lc, lam_init):
    bsz, ltot, d_diff = k.shape
    heads = d_diff // DIFF_V
    nq, nsub, tk = Q_TILE, Q_SUBTILES, KV_TILE
    nqs = nq * nsub
    assert lc % nqs == 0 and lc % tk == 0 and ltot % nqs == 0 and ltot % tk == 0
    assert KV_GROUP % 2 == 0 and (ltot - lc) // tk % KV_GROUP == 0
    rowspec = pl.BlockSpec((1, nqs, DIFF_V), lambda b, h, i: (b, i, h))
    kern = functools.partial(_attn_kernel, nsub=nsub, nq=nq, tk=tk, n_ctx_q=lc // nqs, n_ctx_kv=lc // tk,
                             n_kv=ltot // tk, lam_init=lam_init)
    return pl.pallas_call(
        kern,
        grid=(bsz, heads, ltot // nqs),
        in_specs=[pl.BlockSpec((1, DIFF_V, nqs), lambda b, h, i: (b, h, i)),
                  pl.BlockSpec((1, ltot, DIFF_V), lambda b, h, i: (b, 0, h)),
                  pl.BlockSpec((1, DIFF_V, ltot), lambda b, h, i: (b, h, 0)),
                  rowspec,
                  pl.BlockSpec(lam_p.shape, lambda b, h, i: (0, 0)),
                  pl.BlockSpec((1, DIFF_V), lambda b, h, i: (0, 0))],
        out_specs=rowspec,
        out_shape=jax.ShapeDtypeStruct((bsz, ltot, d_diff), BF16),
        scratch_shapes=[pltpu.VMEM((nsub, 1, 2 * nq), F32),
                        pltpu.VMEM((nsub, DIFF_V + ONES_ROWS, 2 * nq), F32),
                        pltpu.VMEM((2, nsub, tk, 2 * nq), F32)],
        name="diff_attention",
        compiler_params=_cparams(("parallel", "parallel", "arbitrary")),
    )(qt, k, vt, sgb, lam_p, subln.reshape(1, DIFF_V))


def _seq_position(i, tm, ncb, lc, ll):
    nblk = ncb + ll // tm
    is_ctx = i < ncb
    t0 = jnp.where(is_ctx, i, i - ncb) * tm
    lseq = jnp.where(is_ctx, lc, ll)
    has_prev = jnp.logical_and(i != 0, i != ncb)
    has_next = jnp.logical_and(i != ncb - 1, i != nblk - 1)
    return t0, lseq, has_prev, has_next


def _outproj_even_kernel(x_ref, u_ref, up_ref, un_ref, sga_ref, yb_ref, wpool_ref, pscale_ref, wout_ref,
                         gpost_ref, mod_ref, o_ref, *, tm, ncb, lc, ll, d_pool):
    i = pl.program_id(1)
    t0, lseq, has_prev, has_next = _seq_position(i, tm, ncb, lc, ll)
    u = u_ref[0]
    z = jnp.concatenate([jnp.where(has_prev, up_ref[0], 0.0), u, jnp.where(has_next, un_ref[0], 0.0)], axis=0)
    p2 = z[:-1] + z[1:]
    p4 = p2[:-2] + p2[2:]
    p8 = p4[:-4] + p4[4:]
    p16 = p8[:-8] + p8[8:]
    sums = (p2[7:7 + tm], p4[6:6 + tm], p8[4:4 + tm], p16[0:tm])
    lane = lax.broadcasted_iota(jnp.int32, u.shape, 1)
    grp = lane >> 6
    s = jnp.where(grp == 0, sums[0], jnp.where(grp == 1, sums[1], jnp.where(grp == 2, sums[2], sums[3])))
    half = jnp.left_shift(1, grp)
    t = lax.broadcasted_iota(jnp.int32, u.shape, 0) + t0
    cnt = jnp.minimum(t + half - 1, lseq - 1) + 1 - jnp.maximum(t - half, 0)
    dpool = s / cnt.astype(F32) - u
    ya = jnp.dot(dpool.astype(BF16), wpool_ref[...], preferred_element_type=F32) * pscale_ref[...] * sga_ref[0]
    y = (jnp.dot(ya.astype(BF16), wout_ref[0:d_pool, :], preferred_element_type=F32)
         + jnp.dot(yb_ref[0], wout_ref[d_pool:, :], preferred_element_type=F32))
    _postnorm_residual(x_ref, y, gpost_ref, mod_ref, o_ref)


def _halo_specs(tm, halo, width, ltot):
    r = tm // halo
    prev = pl.BlockSpec((1, halo, width), lambda b, i: (b, jnp.maximum(i * r - 1, 0), 0))
    nxt = pl.BlockSpec((1, halo, width), lambda b, i: (b, jnp.minimum((i + 1) * r, ltot // halo - 1), 0))
    return prev, nxt


def _outproj_even(xc, u, sga, yb, wpool, pscale, wout, gpost, mod, ncb, lc):
    bsz, ltot, d = xc.shape
    tm = ROW_TILE
    d_pool = u.shape[-1]
    rows, full2, modspec = _row_specs(tm, ncb)
    prev, nxt = _halo_specs(tm, POOL_HALO, d_pool, ltot)
    kern = functools.partial(_outproj_even_kernel, tm=tm, ncb=ncb, lc=lc, ll=ltot - lc, d_pool=d_pool)
    return pl.pallas_call(
        kern,
        grid=(bsz, ltot // tm),
        in_specs=[rows(d), rows(d_pool), prev, nxt, rows(d_pool), rows(yb.shape[-1]),
                  full2(*wpool.shape), full2(1, d_pool), full2(*wout.shape), full2(1, d), modspec(d)],
        out_specs=rows(d),
        out_shape=jax.ShapeDtypeStruct(xc.shape, F32),
        name="outproj_even",
        compiler_params=_cparams(("parallel", "parallel")),
    )(xc, u, u, u, sga, yb, wpool, pscale.reshape(1, d_pool), wout, gpost.reshape(1, d), mod)


def _inproj_odd_kernel(x_ref, g_ref, mod_ref, w_ref, glu_ref, scg_ref, qh_ref, ff_ref, fb_ref, iv_ref, sog_ref,
                       *, d_conv, d_hgrn):
    hb = _prenorm(x_ref, g_ref, mod_ref)

    def proj(c0, width):
        return jnp.dot(hb, w_ref[:, c0:c0 + width], preferred_element_type=F32)

    glu_ref[0] = proj(0, d_conv) * jax.nn.sigmoid(proj(d_conv, d_conv))
    scg_ref[0] = _silu(proj(2 * d_conv, d_conv)).astype(BF16)
    c0 = 3 * d_conv
    qh_ref[0] = _silu(proj(c0, d_hgrn)).astype(BF16)
    ff_ref[0] = proj(c0 + d_hgrn, d_hgrn)
    fb_ref[0] = proj(c0 + 2 * d_hgrn, d_hgrn)
    iv_ref[0] = proj(c0 + 3 * d_hgrn, d_hgrn).astype(BF16)
    sog_ref[0] = _silu(proj(c0 + 4 * d_hgrn, d_hgrn)).astype(BF16)


def _inproj_odd(xc, g, mod, w, ncb, d_conv, d_hgrn):
    bsz, ltot, d = xc.shape
    tm = ROW_TILE
    rows, full2, modspec = _row_specs(tm, ncb)
    shp = lambda width, dt: jax.ShapeDtypeStruct((bsz, ltot, width), dt)
    return pl.pallas_call(
        functools.partial(_inproj_odd_kernel, d_conv=d_conv, d_hgrn=d_hgrn),
        grid=(bsz, ltot // tm),
        in_specs=[rows(d), full2(1, d), modspec(d), full2(*w.shape)],
        out_specs=[rows(d_conv), rows(d_conv), rows(d_hgrn), rows(d_hgrn), rows(d_hgrn), rows(d_hgrn),
                   rows(d_hgrn)],
        out_shape=[shp(d_conv, F32), shp(d_conv, BF16), shp(d_hgrn, BF16), shp(d_hgrn, F32), shp(d_hgrn, F32),
                   shp(d_hgrn, BF16), shp(d_hgrn, BF16)],
        name="inproj_odd",
        compiler_params=_cparams(("parallel", "parallel")),
    )(xc, g.reshape(1, d), mod, w)


def _cumsum_rows(x, rev):
    n = x.shape[0]
    row = lax.broadcasted_iota(jnp.int32, x.shape, 0)
    sh = 1
    while sh < n:
        if rev:
            x = x + jnp.where(row < n - sh, pltpu.roll(x, n - sh, 0), 0.0)
        else:
            x = x + jnp.where(row >= sh, pltpu.roll(x, sh, 0), 0.0)
        sh *= 2
    return x


def _hgrn_chain(qf, logit, vb, lb, st_ref, rev):
    c = qf.shape[0]
    f = lb + (1.0 - lb) * jax.nn.sigmoid(logit)
    kk = 1.0 - f
    b = _cumsum_rows(jnp.log(f), rev)
    vf = vb.astype(F32)
    row = lax.broadcasted_iota(jnp.int32, (c, 1), 0)
    st = st_ref[...]
    inter = lax.dot_general((qf * jnp.exp(b)).astype(BF16), st.astype(BF16), (((1,), (1,)), ((), ())),
                            preferred_element_type=F32)
    ri = lax.broadcasted_iota(jnp.int32, (c, c), 0)
    ci = lax.broadcasted_iota(jnp.int32, (c, c), 1)
    att = jnp.zeros((c, c), F32)
    w = c // 2
    while w >= SUBLANES:
        refs = []
        for p in range(c // (2 * w)):
            r0 = p * 2 * w + (w if rev else w - 1)
            refs.append(jnp.broadcast_to(b[r0:r0 + 1, :], (2 * w, b.shape[1])))
        ref = refs[0] if len(refs) == 1 else jnp.concatenate(refs, axis=0)
        odd_blk = (row & w) != 0
        is_q = jnp.logical_not(odd_blk) if rev else odd_blk
        qs = jnp.where(is_q, jnp.exp(jnp.minimum(b - ref, 0.0)), 0.0) * qf
        ks = jnp.where(is_q, 0.0, jnp.exp(jnp.minimum(ref - b, 0.0))) * kk
        a = lax.dot_general(qs.astype(BF16), ks.astype(BF16), (((1,), (1,)), ((), ())),
                            preferred_element_type=F32)
        shift = int(math.log2(2 * w))
        att = att + jnp.where((ri >> shift) == (ci >> shift), a, 0.0)
        w //= 2
    out = inter + jnp.dot(att.astype(BF16), vb, preferred_element_type=F32)
    sub = row & (SUBLANES - 1)
    for d in range(SUBLANES):
        if d == 0:
            a = jnp.sum(qf * kk, axis=1, keepdims=True)
            out = out + a * vf
            continue
        sh = c - d if rev else d
        valid = (sub + d <= SUBLANES - 1) if rev else (sub >= d)
        e = jnp.exp(jnp.where(valid, b - pltpu.roll(b, sh, 0), 0.0))
        a = jnp.sum(qf * e * pltpu.roll(kk, sh, 0), axis=1, keepdims=True)
        out = out + jnp.where(valid, a, 0.0) * pltpu.roll(vf, sh, 0)
    bl = b[0:1, :] if rev else b[c - 1:c, :]
    kd = (kk * jnp.exp(bl - b)).astype(BF16)
    st_ref[...] = st * jnp.exp(bl) + lax.dot_general(vb, kd, (((0,), (0,)), ((), ())),
                                                    preferred_element_type=F32)
    return out


def _hgrn_kernel(lbraw_ref, qf_ref, ff_ref, vf_ref, qb_ref, fb_ref, vb_ref, of_ref, ob_ref, st_sc,
                 *, layer, bsz, heads):
    s = pl.program_id(0)

    @pl.when(s == 0)
    def _():
        st_sc[...] = jnp.zeros(st_sc.shape, F32)

    raw = lbraw_ref[...]
    depth = raw.shape[1]
    rows = [raw[:, j, :] for j in range(depth)]
    mx = functools.reduce(jnp.maximum, rows)
    ex = [jnp.exp(r - mx) for r in rows]
    den = functools.reduce(lambda a, bb: a + bb, ex)
    if layer >= 1:
        lbs = functools.reduce(lambda a, bb: a + bb, [e / den for e in ex[1:layer + 1]])
    else:
        lbs = jnp.zeros_like(mx)
    for bi in range(bsz):
        for h in range(heads):
            sl = slice(h * HGRN_HEAD, (h + 1) * HGRN_HEAD)
            of_ref[bi, :, sl] = _hgrn_chain(qf_ref[bi, :, sl].astype(F32), ff_ref[bi, :, sl], vf_ref[bi, :, sl],
                                            lbs[0:1, sl], st_sc.at[bi, 0, h], False)
            ob_ref[bi, :, sl] = _hgrn_chain(qb_ref[bi, :, sl].astype(F32), fb_ref[bi, :, sl], vb_ref[bi, :, sl],
                                            lbs[1:2, sl], st_sc.at[bi, 1, h], True)


def _hgrn(qh, ff, fb, iv, hgrn_lb, layer, lc):
    bsz, ltot, d_hgrn = qh.shape
    heads = d_hgrn // HGRN_HEAD
    c = HGRN_CHUNK
    assert lc % c == 0 and ltot % c == 0
    ncc, ntot = lc // c, ltot // c

    def bwd(s):
        return jnp.where(s < ncc, ncc - 1 - s, ntot - 1 + ncc - s)

    fspec = pl.BlockSpec((bsz, c, d_hgrn), lambda s: (0, s, 0))
    bspec = pl.BlockSpec((bsz, c, d_hgrn), lambda s: (0, bwd(s), 0))
    kern = functools.partial(_hgrn_kernel, layer=layer, bsz=bsz, heads=heads)
    return pl.pallas_call(
        kern,
        grid=(ntot,),
        in_specs=[pl.BlockSpec(hgrn_lb.shape, lambda s: (0, 0, 0)), fspec, fspec, fspec, bspec, bspec, bspec],
        out_specs=[fspec, bspec],
        out_shape=[jax.ShapeDtypeStruct((bsz, ltot, d_hgrn), F32)] * 2,
        scratch_shapes=[pltpu.VMEM((bsz, 2, heads, HGRN_HEAD, HGRN_HEAD), F32)],
        name="hgrn_scan",
        compiler_params=_cparams(("arbitrary",)),
    )(hgrn_lb, qh, ff, iv, qh, fb, iv)


def _outproj_odd_kernel(x_ref, glu_ref, gp_ref, gn_ref, scg_ref, cw_ref, cb_ref, lng_ref, lnb_ref,
                        of_ref, ob_ref, hn_ref, sog_ref, wout_ref, gpost_ref, mod_ref, o_ref, z_sc,
                        *, tm, ncb, lc, ll, d_conv):
    i = pl.program_id(1)
    _, _, has_prev, has_next = _seq_position(i, tm, ncb, lc, ll)
    h = CONV_HALO
    z_sc[0:h, :] = jnp.where(has_prev, gp_ref[0], 0.0)
    z_sc[h:h + tm, :] = glu_ref[0]
    z_sc[h + tm:, :] = jnp.where(has_next, gn_ref[0], 0.0)
    cw = cw_ref[...]
    base = h - CONV_WIDTH // 2
    acc = jnp.zeros((tm, d_conv), F32)
    for j in range(CONV_WIDTH):
        acc = acc + cw[j:j + 1, :] * z_sc[base + j:base + j + tm, :]
    zc = acc + cb_ref[...]
    mu = jnp.mean(zc, axis=-1, keepdims=True)
    zc = zc - mu
    var = jnp.mean(zc * zc, axis=-1, keepdims=True)
    zn = zc * lax.rsqrt(var + EPS) * lng_ref[...] + lnb_ref[...]
    yc = _silu(zn) * scg_ref[0].astype(F32)
    o = of_ref[0] + ob_ref[0]
    hn = hn_ref[...]
    parts = []
    for hd in range(o.shape[1] // HGRN_HEAD):
        sl = slice(hd * HGRN_HEAD, (hd + 1) * HGRN_HEAD)
        oh = o[:, sl]
        ms = jnp.mean(oh * oh, axis=-1, keepdims=True)
        parts.append(oh * lax.rsqrt(ms + EPS) * hn[:, sl])
    yd = jnp.concatenate(parts, axis=1) * sog_ref[0].astype(F32)
    y = (jnp.dot(yc.astype(BF16), wout_ref[0:d_conv, :], preferred_element_type=F32)
         + jnp.dot(yd.astype(BF16), wout_ref[d_conv:, :], preferred_element_type=F32))
    _postnorm_residual(x_ref, y, gpost_ref, mod_ref, o_ref)


def _outproj_odd(xc, glu, scg, cw, cb, lng, lnb, o_f, o_b, hnorm, sog, wout, gpost, mod, ncb, lc):
    bsz, ltot, d = xc.shape
    tm = ROW_TILE
    d_conv = glu.shape[-1]
    d_hgrn = o_f.shape[-1]
    rows, full2, modspec = _row_specs(tm, ncb)
    prev, nxt = _halo_specs(tm, CONV_HALO, d_conv, ltot)
    kern = functools.partial(_outproj_odd_kernel, tm=tm, ncb=ncb, lc=lc, ll=ltot - lc, d_conv=d_conv)
    return pl.pallas_call(
        kern,
        grid=(bsz, ltot // tm),
        in_specs=[rows(d), rows(d_conv), prev, nxt, rows(d_conv), full2(*cw.shape), full2(1, d_conv),
                  full2(1, d_conv), full2(1, d_conv), rows(d_hgrn), rows(d_hgrn), full2(1, d_hgrn),
                  rows(d_hgrn), full2(*wout.shape), full2(1, d), modspec(d)],
        out_specs=rows(d),
        out_shape=jax.ShapeDtypeStruct(xc.shape, F32),
        scratch_shapes=[pltpu.VMEM((tm + 2 * CONV_HALO, d_conv), F32)],
        name="outproj_odd",
        compiler_params=_cparams(("parallel", "parallel")),
    )(xc, glu, glu, glu, scg, cw, cb.reshape(1, d_conv), lng.reshape(1, d_conv), lnb.reshape(1, d_conv),
      o_f, o_b, hnorm.reshape(1, d_hgrn), sog, wout, gpost.reshape(1, d), mod)


def _rope_tables(lc, ll):
    t = jnp.arange(ll)
    inv = ROPE_BASE ** (-jnp.arange(ROPE_FREQS, dtype=F32) / ROPE_FREQS)
    ang = jnp.stack([t // GRID_W, t % GRID_W], axis=-1).astype(F32)[:, :, None] * inv
    cos = jnp.cos(ang)
    sin = jnp.sin(ang)
    cos64 = jnp.stack([cos, cos], axis=2).reshape(ll, DIFF_HEAD)
    sin64 = jnp.stack([-sin, sin], axis=2).reshape(ll, DIFF_HEAD)
    cos_t = jnp.concatenate([jnp.ones((lc, DIFF_HEAD), F32), cos64], axis=0)
    sin_t = jnp.concatenate([jnp.zeros((lc, DIFF_HEAD), F32), sin64], axis=0)
    return jnp.tile(cos_t, (1, LANES // DIFF_HEAD)), jnp.tile(sin_t, (1, LANES // DIFF_HEAD))


def _block_diag(w):
    g, a, b = w.shape
    out = jnp.zeros((g * a, g * b), w.dtype)
    for j in range(g):
        out = out.at[j * a:(j + 1) * a, j * b:(j + 1) * b].set(w[j])
    return out


def kernel(x, c, ctx, c_ctx, ada_w, ada_b, norm_pre, norm_post, w_in_even, w_out_even, pool_w, pool_scale,
           diff_lambda, diff_subln, w_in_odd, w_out_odd, conv_w, conv_b, conv_ln_g, conv_ln_b, hgrn_norm,
           hgrn_lb):
    bsz, ll, d = x.shape
    lc = ctx.shape[1]
    depth = ada_w.shape[0]
    d_pool = pool_scale.shape[-1]
    d_diff = w_out_even.shape[1] - d_pool
    d_conv = conv_b.shape[-1]
    d_hgrn = hgrn_norm.shape[-1]
    assert lc % ROW_TILE == 0 and ll % ROW_TILE == 0 and bsz + 1 <= SUBLANES
    ncb = lc // ROW_TILE

    cvec = jnp.zeros((SUBLANES, d), F32).at[:bsz].set(c).at[bsz].set(c_ctx)
    ada = _ada_call(cvec, ada_w, ada_b).reshape(depth, SUBLANES, 3, d)
    xc = jnp.concatenate([ctx, x], axis=1)
    cos, sin = _rope_tables(lc, ll)

    for l in range(depth):
        mod = jnp.stack([jnp.broadcast_to(ada[l, bsz], (bsz, 3, d)), ada[l, :bsz]], axis=1)
        j = l // 2
        if l % 2 == 0:
            lam_init = 0.8 - 0.6 * math.exp(-0.3 * l)
            wb = w_in_even[j].astype(BF16)
            q0, k0, v0, g0 = 2 * d_pool, 2 * d_pool + d_diff, 2 * d_pool + 2 * d_diff, 2 * d_pool + 3 * d_diff
            w_rows = jnp.concatenate([wb[:, :q0], wb[:, k0:v0], wb[:, g0:]], axis=1)
            w_cols_t = jnp.concatenate([wb[:, q0:k0], wb[:, v0:g0]], axis=1).T
            u, sga, k, sgb, qt, vt = _inproj_even(xc, norm_pre[l], mod, w_rows, w_cols_t, cos, sin, ncb,
                                                  d_pool, d_diff)
            yb = _attention(qt, k, vt, sgb, diff_lambda[j], diff_subln[j], lc, lam_init)
            xc = _outproj_even(xc, u, sga, yb, _block_diag(pool_w[j]).astype(BF16), pool_scale[j],
                               w_out_even[j].astype(BF16), norm_post[l], mod, ncb, lc)
        else:
            glu, scg, qh, ff, fb, iv, sog = _inproj_odd(xc, norm_pre[l], mod, w_in_odd[j].astype(BF16), ncb,
                                                        d_conv, d_hgrn)
            o_f, o_b = _hgrn(qh, ff, fb, iv, hgrn_lb, l, lc)
            xc = _outproj_odd(xc, glu, scg, conv_w[j], conv_b[j], conv_ln_g[j], conv_ln_b[j], o_f, o_b,
                              hgrn_norm[j], sog, w_out_odd[j].astype(BF16), norm_post[l], mod, ncb, lc)
    return xc[:, lc:]
```

```python
import functools
import math

import numpy as np
import jax
import jax.numpy as jnp
from jax import lax
from jax.experimental import pallas as pl
from jax.experimental.pallas import tpu as pltpu

F32 = jnp.float32
BF16 = jnp.bfloat16
EPS = 1e-6

GRID_W = 64
POOL_WINDOWS = (2, 4, 8, 16)
POOL_GROUP = 64
DIFF_HEAD = 64
DIFF_V = 2 * DIFF_HEAD
DIFF_SCALE = DIFF_HEAD ** -0.5
ROPE_BASE = 10000.0
ROPE_FREQS = DIFF_HEAD // 4
CONV_WIDTH = 31
CONV_HALO = 16
POOL_HALO = 8
HGRN_HEAD = 128
LANES = 128
SUBLANES = 8

ROW_TILE = 256
Q_TILE = 128
Q_SUBTILES = 2
KV_GROUP = 8
SCORE_SLOTS = 4
ONES_ROWS = 16
LOG2E = 1.4426950408889634
KV_TILE = 256
HGRN_CHUNK = 64
VMEM_LIMIT = 48 * 1024 * 1024


def _cparams(sem):
    return pltpu.CompilerParams(dimension_semantics=sem, vmem_limit_bytes=VMEM_LIMIT)


def _silu(x):
    return x * jax.nn.sigmoid(x)


def _bdot(a, b):
    return jnp.dot(a.astype(BF16), b.astype(BF16), preferred_element_type=F32)


def _ada_kernel(c_ref, w_ref, b_ref, o_ref):
    c = c_ref[...]
    o_ref[0] = jnp.dot(_silu(c), w_ref[0], preferred_element_type=F32,
                       precision=lax.Precision.HIGHEST) + b_ref[0]


def _ada_call(cvec, ada_w, ada_b):
    depth, d, d3 = ada_w.shape
    nj = d3 // d
    return pl.pallas_call(
        _ada_kernel,
        grid=(depth, nj),
        in_specs=[pl.BlockSpec((SUBLANES, d), lambda l, j: (0, 0)),
                  pl.BlockSpec((1, d, d), lambda l, j: (l, 0, j)),
                  pl.BlockSpec((1, 1, d), lambda l, j: (l, 0, j))],
        out_specs=pl.BlockSpec((1, SUBLANES, d), lambda l, j: (l, 0, j)),
        out_shape=jax.ShapeDtypeStruct((depth, SUBLANES, d3), F32),
        name="ada_params",
        compiler_params=_cparams(("parallel", "parallel")),
    )(cvec, ada_w, ada_b.reshape(depth, 1, d3))


def _prenorm(x_ref, g_ref, mod_ref):
    x = x_ref[0]
    mod = mod_ref[0, 0]
    ms = jnp.mean(x * x, axis=-1, keepdims=True)
    h = x * lax.rsqrt(ms + EPS) * g_ref[...] * (1.0 + mod[1:2]) + mod[0:1]
    return h.astype(BF16)


def _postnorm_residual(x_ref, y, gpost_ref, mod_ref, o_ref):
    mod = mod_ref[0, 0]
    ms = jnp.mean(y * y, axis=-1, keepdims=True)
    yn = y * lax.rsqrt(ms + EPS) * gpost_ref[...]
    o_ref[0] = x_ref[0] + mod[2:3] * yn


def _row_specs(tm, ncb):
    def rows(width):
        return pl.BlockSpec((1, tm, width), lambda b, i: (b, i, 0))

    def full2(a, bdim):
        return pl.BlockSpec((a, bdim), lambda b, i: (0, 0))

    def mod(d):
        return pl.BlockSpec((1, 1, 3, d), lambda b, i: (b, jnp.where(i < ncb, 0, 1), 0, 0))

    return rows, full2, mod


def _inproj_even_kernel(x_ref, g_ref, mod_ref, w_ref, wt_ref, cos_ref, sin_ref, cost_ref, sint_ref,
                        u_ref, sga_ref, k_ref, sgb_ref, qt_ref, vt_ref, *, d_pool, d_diff):
    hb = _prenorm(x_ref, g_ref, mod_ref)

    def proj(c0, width):
        return jnp.dot(hb, w_ref[:, c0:c0 + width], preferred_element_type=F32)

    def proj_t(r0, height):
        return lax.dot_general(wt_ref[r0:r0 + height, :], hb, (((1,), (1,)), ((), ())),
                               preferred_element_type=F32)

    u_ref[0] = proj(0, d_pool)
    sga_ref[0] = _silu(proj(d_pool, d_pool))
    cos = cos_ref[...]
    sin = sin_ref[...]
    lane = lax.broadcasted_iota(jnp.int32, cos.shape, 1)
    first_half = (lane & ROPE_FREQS) == 0
    k0 = 2 * d_pool
    for s in range(d_diff // LANES):
        t = proj(k0 + s * LANES, LANES)
        partner = jnp.where(first_half, pltpu.roll(t, LANES - ROPE_FREQS, 1), pltpu.roll(t, ROPE_FREQS, 1))
        k_ref[0, :, s * LANES:(s + 1) * LANES] = (t * cos + partner * sin).astype(BF16)
    sgb_ref[0] = _silu(proj(k0 + d_diff, d_diff)).astype(BF16)
    cost = cost_ref[...]
    sint = sint_ref[...]
    f = ROPE_FREQS
    for s in range(d_diff // LANES):
        t = proj_t(s * LANES, LANES)
        partner = jnp.concatenate([t[(blk ^ 1) * f:((blk ^ 1) + 1) * f] for blk in range(LANES // f)], axis=0)
        qt_ref[0, s * LANES:(s + 1) * LANES, :] = ((t * cost + partner * sint) * (DIFF_SCALE * LOG2E)).astype(BF16)
    vt_ref[0] = proj_t(d_diff, d_diff).astype(BF16)


def _inproj_even(xc, g, mod, w, wt, cos, sin, ncb, d_pool, d_diff):
    bsz, ltot, d = xc.shape
    tm = ROW_TILE
    rows, full2, modspec = _row_specs(tm, ncb)
    tab = pl.BlockSpec((tm, LANES), lambda b, i: (i, 0))
    tab_t = pl.BlockSpec((LANES, tm), lambda b, i: (0, i))
    cols_t = pl.BlockSpec((1, d_diff, tm), lambda b, i: (b, 0, i))
    shp = lambda width, dt: jax.ShapeDtypeStruct((bsz, ltot, width), dt)
    shp_t = jax.ShapeDtypeStruct((bsz, d_diff, ltot), BF16)
    return pl.pallas_call(
        functools.partial(_inproj_even_kernel, d_pool=d_pool, d_diff=d_diff),
        grid=(bsz, ltot // tm),
        in_specs=[rows(d), full2(1, d), modspec(d), full2(*w.shape), full2(*wt.shape), tab, tab, tab_t, tab_t],
        out_specs=[rows(d_pool), rows(d_pool), rows(d_diff), rows(d_diff), cols_t, cols_t],
        out_shape=[shp(d_pool, F32), shp(d_pool, F32), shp(d_diff, BF16), shp(d_diff, BF16), shp_t, shp_t],
        name="inproj_even",
        compiler_params=_cparams(("parallel", "parallel")),
    )(xc, g.reshape(1, d), mod, w, wt, cos, sin, cos.T, sin.T)


def _attn_kernel(qt_ref, k_ref, vt_ref, sgb_ref, lam_ref, sub_ref, o_ref, acc_sc, st_sc, mx_sc,
                 *, nsub, nq, tk, n_ctx_q, n_ctx_kv, n_kv, lam_init):
    i = pl.program_id(2)
    row = lax.broadcasted_iota(jnp.int32, (DIFF_V, nq), 0)
    ws = []
    for c in range(nsub):
        qt = qt_ref[0, :, c * nq:(c + 1) * nq]
        zero = jnp.zeros_like(qt)
        ws.append(jnp.concatenate([jnp.where(row < DIFF_HEAD, qt, zero), jnp.where(row >= DIFF_HEAD, qt, zero)],
                                  axis=1))

    ones = jnp.ones((ONES_ROWS, tk), BF16)

    def scores(off, slot):
        kc = k_ref[0, pl.ds(off, tk), :]
        for c in range(nsub):
            st = jnp.dot(kc, ws[c], preferred_element_type=F32)
            st_sc[slot, c] = st
            mx_sc[slot, c] = jnp.max(st, axis=0, keepdims=True)

    def finish(pending):
        for c, (alpha, pv) in enumerate(pending):
            acc_sc[c] = pv if alpha is None else alpha * acc_sc[c] + pv

    def absorb(off, slot, ms, pending=None, ahead=None):
        new = [mx_sc[slot, c] if ms is None else jnp.maximum(ms[c], mx_sc[slot, c]) for c in range(nsub)]
        alphas = [None if ms is None else jnp.exp2(ms[c] - new[c]) for c in range(nsub)]
        if ahead is not None:
            ahead()
        if pending is not None:
            finish(pending)
        vte = jnp.concatenate([vt_ref[0, :, pl.ds(off, tk)], ones], axis=0)
        pvs = [jnp.dot(vte, jnp.exp2(st_sc[slot, c] - new[c]).astype(BF16), preferred_element_type=F32)
               for c in range(nsub)]
        return tuple(new), tuple(zip(alphas, pvs))

    @pl.when(i < n_ctx_q)
    def _():
        ms = None
        for n in range(n_ctx_kv):
            scores(n * tk, 0)
            ms, pending = absorb(n * tk, 0, ms)
            finish(pending)

    @pl.when(i >= n_ctx_q)
    def _():
        last = (n_kv - 1) * tk
        g = KV_GROUP
        slots = SCORE_SLOTS
        scores(0, 0)
        scores(tk, 1)
        ms, pending = None, None
        for n in range(n_ctx_kv):
            ms, pending = absorb(n * tk, n % slots, ms, pending,
                                 lambda o=(n + 2) * tk, s=(n + 2) % slots: scores(o, s))
        finish(pending)

        def group(jj, ms):
            offs = [pl.multiple_of(jnp.minimum((n_ctx_kv + g * jj + t) * tk, last), tk) for t in range(g + 2)]
            pending = None
            for t in range(g):
                ms, pending = absorb(offs[t], (n_ctx_kv + t) % slots, ms, pending,
                                     lambda o=offs[t + 2], s=(n_ctx_kv + t + 2) % slots: scores(o, s))
            finish(pending)
            return ms

        lax.fori_loop(0, (n_kv - n_ctx_kv) // g, group, ms)

    lp = lam_ref[...]
    lam = (jnp.exp(jnp.sum(lp[0:1] * lp[1:2], axis=1, keepdims=True))
           - jnp.exp(jnp.sum(lp[2:3] * lp[3:4], axis=1, keepdims=True)) + lam_init)
    for c in range(nsub):
        acc = acc_sc[c]
        o_all = acc[:DIFF_V] / acc[DIFF_V:DIFF_V + 1]
        o = (o_all[:, :nq] - lam * o_all[:, nq:]).T
        ms = jnp.mean(o * o, axis=-1, keepdims=True)
        on = o * lax.rsqrt(ms + EPS) * sub_ref[...] * (1.0 - lam_init)
        o_ref[0, c * nq:(c + 1) * nq, :] = (on * sgb_ref[0, c * nq:(c + 1) * nq, :].astype(F32)).astype(BF16)


def _attention(qt, k, vt, sgb, lam_p, subln, lc, lam_init):
    bsz, ltot, d_diff = k.shape
    heads = d_diff // DIFF_V
    nq, nsub, tk = Q_TILE, Q_SUBTILES, KV_TILE
    nqs = nq * nsub
    assert lc % nqs == 0 and lc % tk == 0 and ltot % nqs == 0 and ltot % tk == 0
    assert KV_GROUP % SCORE_SLOTS == 0 and (ltot - lc) // tk % KV_GROUP == 0
    rowspec = pl.BlockSpec((1, nqs, DIFF_V), lambda b, h, i: (b, i, h))
    kern = functools.partial(_attn_kernel, nsub=nsub, nq=nq, tk=tk, n_ctx_q=lc // nqs, n_ctx_kv=lc // tk,
                             n_kv=ltot // tk, lam_init=lam_init)
    return pl.pallas_call(
        kern,
        grid=(bsz, heads, ltot // nqs),
        in_specs=[pl.BlockSpec((1, DIFF_V, nqs), lambda b, h, i: (b, h, i)),
                  pl.BlockSpec((1, ltot, DIFF_V), lambda b, h, i: (b, 0, h)),
                  pl.BlockSpec((1, DIFF_V, ltot), lambda b, h, i: (b, h, 0)),
                  rowspec,
                  pl.BlockSpec(lam_p.shape, lambda b, h, i: (0, 0)),
                  pl.BlockSpec((1, DIFF_V), lambda b, h, i: (0, 0))],
        out_specs=rowspec,
        out_shape=jax.ShapeDtypeStruct((bsz, ltot, d_diff), BF16),
        scratch_shapes=[pltpu.VMEM((nsub, DIFF_V + ONES_ROWS, 2 * nq), F32),
                        pltpu.VMEM((SCORE_SLOTS, nsub, tk, 2 * nq), F32),
                        pltpu.VMEM((SCORE_SLOTS, nsub, 1, 2 * nq), F32)],
        name="diff_attention",
        compiler_params=_cparams(("parallel", "parallel", "arbitrary")),
    )(qt, k, vt, sgb, lam_p, subln.reshape(1, DIFF_V))


def _seq_position(i, tm, ncb, lc, ll):
    nblk = ncb + ll // tm
    is_ctx = i < ncb
    t0 = jnp.where(is_ctx, i, i - ncb) * tm
    lseq = jnp.where(is_ctx, lc, ll)
    has_prev = jnp.logical_and(i != 0, i != ncb)
    has_next = jnp.logical_and(i != ncb - 1, i != nblk - 1)
    return t0, lseq, has_prev, has_next


def _outproj_even_kernel(x_ref, u_ref, up_ref, un_ref, sga_ref, yb_ref, wpool_ref, pscale_ref, wout_ref,
                         gpost_ref, mod_ref, o_ref, *, tm, ncb, lc, ll, d_pool):
    i = pl.program_id(1)
    t0, lseq, has_prev, has_next = _seq_position(i, tm, ncb, lc, ll)
    u = u_ref[0]
    z = jnp.concatenate([jnp.where(has_prev, up_ref[0], 0.0), u, jnp.where(has_next, un_ref[0], 0.0)], axis=0)
    p2 = z[:-1] + z[1:]
    p4 = p2[:-2] + p2[2:]
    p8 = p4[:-4] + p4[4:]
    p16 = p8[:-8] + p8[8:]
    sums = (p2[7:7 + tm], p4[6:6 + tm], p8[4:4 + tm], p16[0:tm])
    lane = lax.broadcasted_iota(jnp.int32, u.shape, 1)
    grp = lane >> 6
    s = jnp.where(grp == 0, sums[0], jnp.where(grp == 1, sums[1], jnp.where(grp == 2, sums[2], sums[3])))
    half = jnp.left_shift(1, grp)
    t = lax.broadcasted_iota(jnp.int32, u.shape, 0) + t0
    cnt = jnp.minimum(t + half - 1, lseq - 1) + 1 - jnp.maximum(t - half, 0)
    dpool = s / cnt.astype(F32) - u
    ya = jnp.dot(dpool.astype(BF16), wpool_ref[...], preferred_element_type=F32) * pscale_ref[...] * sga_ref[0]
    y = (jnp.dot(ya.astype(BF16), wout_ref[0:d_pool, :], preferred_element_type=F32)
         + jnp.dot(yb_ref[0], wout_ref[d_pool:, :], preferred_element_type=F32))
    _postnorm_residual(x_ref, y, gpost_ref, mod_ref, o_ref)


def _halo_specs(tm, halo, width, ltot):
    r = tm // halo
    prev = pl.BlockSpec((1, halo, width), lambda b, i: (b, jnp.maximum(i * r - 1, 0), 0))
    nxt = pl.BlockSpec((1, halo, width), lambda b, i: (b, jnp.minimum((i + 1) * r, ltot // halo - 1), 0))
    return prev, nxt


def _outproj_even(xc, u, sga, yb, wpool, pscale, wout, gpost, mod, ncb, lc):
    bsz, ltot, d = xc.shape
    tm = ROW_TILE
    d_pool = u.shape[-1]
    rows, full2, modspec = _row_specs(tm, ncb)
    prev, nxt = _halo_specs(tm, POOL_HALO, d_pool, ltot)
    kern = functools.partial(_outproj_even_kernel, tm=tm, ncb=ncb, lc=lc, ll=ltot - lc, d_pool=d_pool)
    return pl.pallas_call(
        kern,
        grid=(bsz, ltot // tm),
        in_specs=[rows(d), rows(d_pool), prev, nxt, rows(d_pool), rows(yb.shape[-1]),
                  full2(*wpool.shape), full2(1, d_pool), full2(*wout.shape), full2(1, d), modspec(d)],
        out_specs=rows(d),
        out_shape=jax.ShapeDtypeStruct(xc.shape, F32),
        name="outproj_even",
        compiler_params=_cparams(("parallel", "parallel")),
    )(xc, u, u, u, sga, yb, wpool, pscale.reshape(1, d_pool), wout, gpost.reshape(1, d), mod)


def _inproj_odd_kernel(x_ref, g_ref, mod_ref, w_ref, glu_ref, scg_ref, qh_ref, ff_ref, fb_ref, iv_ref, sog_ref,
                       *, d_conv, d_hgrn):
    hb = _prenorm(x_ref, g_ref, mod_ref)

    def proj(c0, width):
        return jnp.dot(hb, w_ref[:, c0:c0 + width], preferred_element_type=F32)

    glu_ref[0] = proj(0, d_conv) * jax.nn.sigmoid(proj(d_conv, d_conv))
    scg_ref[0] = _silu(proj(2 * d_conv, d_conv)).astype(BF16)
    c0 = 3 * d_conv
    qh_ref[0] = _silu(proj(c0, d_hgrn)).astype(BF16)
    ff_ref[0] = proj(c0 + d_hgrn, d_hgrn)
    fb_ref[0] = proj(c0 + 2 * d_hgrn, d_hgrn)
    iv_ref[0] = proj(c0 + 3 * d_hgrn, d_hgrn).astype(BF16)
    sog_ref[0] = _silu(proj(c0 + 4 * d_hgrn, d_hgrn)).astype(BF16)


def _inproj_odd(xc, g, mod, w, ncb, d_conv, d_hgrn):
    bsz, ltot, d = xc.shape
    tm = ROW_TILE
    rows, full2, modspec = _row_specs(tm, ncb)
    shp = lambda width, dt: jax.ShapeDtypeStruct((bsz, ltot, width), dt)
    return pl.pallas_call(
        functools.partial(_inproj_odd_kernel, d_conv=d_conv, d_hgrn=d_hgrn),
        grid=(bsz, ltot // tm),
        in_specs=[rows(d), full2(1, d), modspec(d), full2(*w.shape)],
        out_specs=[rows(d_conv), rows(d_conv), rows(d_hgrn), rows(d_hgrn), rows(d_hgrn), rows(d_hgrn),
                   rows(d_hgrn)],
        out_shape=[shp(d_conv, F32), shp(d_conv, BF16), shp(d_hgrn, BF16), shp(d_hgrn, F32), shp(d_hgrn, F32),
                   shp(d_hgrn, BF16), shp(d_hgrn, BF16)],
        name="inproj_odd",
        compiler_params=_cparams(("parallel", "parallel")),
    )(xc, g.reshape(1, d), mod, w)


def _cumsum_rows(x, rev):
    n = x.shape[0]
    row = lax.broadcasted_iota(jnp.int32, x.shape, 0)
    sh = 1
    while sh < n:
        if rev:
            x = x + jnp.where(row < n - sh, pltpu.roll(x, n - sh, 0), 0.0)
        else:
            x = x + jnp.where(row >= sh, pltpu.roll(x, sh, 0), 0.0)
        sh *= 2
    return x


def _hgrn_chain(qf, logit, vb, lb, st_ref, rev):
    c = qf.shape[0]
    f = lb + (1.0 - lb) * jax.nn.sigmoid(logit)
    kk = 1.0 - f
    b = _cumsum_rows(jnp.log(f), rev)
    vf = vb.astype(F32)
    row = lax.broadcasted_iota(jnp.int32, (c, 1), 0)
    st = st_ref[...]
    inter = lax.dot_general((qf * jnp.exp(b)).astype(BF16), st.astype(BF16), (((1,), (1,)), ((), ())),
                            preferred_element_type=F32)
    ri = lax.broadcasted_iota(jnp.int32, (c, c), 0)
    ci = lax.broadcasted_iota(jnp.int32, (c, c), 1)
    att = jnp.zeros((c, c), F32)
    w = c // 2
    while w >= SUBLANES:
        refs = []
        for p in range(c // (2 * w)):
            r0 = p * 2 * w + (w if rev else w - 1)
            refs.append(jnp.broadcast_to(b[r0:r0 + 1, :], (2 * w, b.shape[1])))
        ref = refs[0] if len(refs) == 1 else jnp.concatenate(refs, axis=0)
        odd_blk = (row & w) != 0
        is_q = jnp.logical_not(odd_blk) if rev else odd_blk
        qs = jnp.where(is_q, jnp.exp(jnp.minimum(b - ref, 0.0)), 0.0) * qf
        ks = jnp.where(is_q, 0.0, jnp.exp(jnp.minimum(ref - b, 0.0))) * kk
        a = lax.dot_general(qs.astype(BF16), ks.astype(BF16), (((1,), (1,)), ((), ())),
                            preferred_element_type=F32)
        shift = int(math.log2(2 * w))
        att = att + jnp.where((ri >> shift) == (ci >> shift), a, 0.0)
        w //= 2
    out = inter + jnp.dot(att.astype(BF16), vb, preferred_element_type=F32)
    sub = row & (SUBLANES - 1)
    for d in range(SUBLANES):
        if d == 0:
            a = jnp.sum(qf * kk, axis=1, keepdims=True)
            out = out + a * vf
            continue
        sh = c - d if rev else d
        valid = (sub + d <= SUBLANES - 1) if rev else (sub >= d)
        e = jnp.exp(jnp.where(valid, b - pltpu.roll(b, sh, 0), 0.0))
        a = jnp.sum(qf * e * pltpu.roll(kk, sh, 0), axis=1, keepdims=True)
        out = out + jnp.where(valid, a, 0.0) * pltpu.roll(vf, sh, 0)
    bl = b[0:1, :] if rev else b[c - 1:c, :]
    kd = (kk * jnp.exp(bl - b)).astype(BF16)
    st_ref[...] = st * jnp.exp(bl) + lax.dot_general(vb, kd, (((0,), (0,)), ((), ())),
                                                    preferred_element_type=F32)
    return out


def _hgrn_kernel(lbraw_ref, qf_ref, ff_ref, vf_ref, qb_ref, fb_ref, vb_ref, of_ref, ob_ref, st_sc,
                 *, layer, bsz, heads):
    s = pl.program_id(0)

    @pl.when(s == 0)
    def _():
        st_sc[...] = jnp.zeros(st_sc.shape, F32)

    raw = lbraw_ref[...]
    depth = raw.shape[1]
    rows = [raw[:, j, :] for j in range(depth)]
    mx = functools.reduce(jnp.maximum, rows)
    ex = [jnp.exp(r - mx) for r in rows]
    den = functools.reduce(lambda a, bb: a + bb, ex)
    if layer >= 1:
        lbs = functools.reduce(lambda a, bb: a + bb, [e / den for e in ex[1:layer + 1]])
    else:
        lbs = jnp.zeros_like(mx)
    for bi in range(bsz):
        for h in range(heads):
            sl = slice(h * HGRN_HEAD, (h + 1) * HGRN_HEAD)
            of_ref[bi, :, sl] = _hgrn_chain(qf_ref[bi, :, sl].astype(F32), ff_ref[bi, :, sl], vf_ref[bi, :, sl],
                                            lbs[0:1, sl], st_sc.at[bi, 0, h], False)
            ob_ref[bi, :, sl] = _hgrn_chain(qb_ref[bi, :, sl].astype(F32), fb_ref[bi, :, sl], vb_ref[bi, :, sl],
                                            lbs[1:2, sl], st_sc.at[bi, 1, h], True)


def _hgrn(qh, ff, fb, iv, hgrn_lb, layer, lc):
    bsz, ltot, d_hgrn = qh.shape
    heads = d_hgrn // HGRN_HEAD
    c = HGRN_CHUNK
    assert lc % c == 0 and ltot % c == 0
    ncc, ntot = lc // c, ltot // c

    def bwd(s):
        return jnp.where(s < ncc, ncc - 1 - s, ntot - 1 + ncc - s)

    fspec = pl.BlockSpec((bsz, c, d_hgrn), lambda s: (0, s, 0))
    bspec = pl.BlockSpec((bsz, c, d_hgrn), lambda s: (0, bwd(s), 0))
    kern = functools.partial(_hgrn_kernel, layer=layer, bsz=bsz, heads=heads)
    return pl.pallas_call(
        kern,
        grid=(ntot,),
        in_specs=[pl.BlockSpec(hgrn_lb.shape, lambda s: (0, 0, 0)), fspec, fspec, fspec, bspec, bspec, bspec],
        out_specs=[fspec, bspec],
        out_shape=[jax.ShapeDtypeStruct((bsz, ltot, d_hgrn), F32)] * 2,
        scratch_shapes=[pltpu.VMEM((bsz, 2, heads, HGRN_HEAD, HGRN_HEAD), F32)],
        name="hgrn_scan",
        compiler_params=_cparams(("arbitrary",)),
    )(hgrn_lb, qh, ff, iv, qh, fb, iv)


def _outproj_odd_kernel(x_ref, glu_ref, gp_ref, gn_ref, scg_ref, cw_ref, cb_ref, lng_ref, lnb_ref,
                        of_ref, ob_ref, hn_ref, sog_ref, wout_ref, gpost_ref, mod_ref, o_ref, z_sc,
                        *, tm, ncb, lc, ll, d_conv):
    i = pl.program_id(1)
    _, _, has_prev, has_next = _seq_position(i, tm, ncb, lc, ll)
    h = CONV_HALO
    z_sc[0:h, :] = jnp.where(has_prev, gp_ref[0], 0.0)
    z_sc[h:h + tm, :] = glu_ref[0]
    z_sc[h + tm:, :] = jnp.where(has_next, gn_ref[0], 0.0)
    cw = cw_ref[...]
    base = h - CONV_WIDTH // 2
    acc = jnp.zeros((tm, d_conv), F32)
    for j in range(CONV_WIDTH):
        acc = acc + cw[j:j + 1, :] * z_sc[base + j:base + j + tm, :]
    zc = acc + cb_ref[...]
    mu = jnp.mean(zc, axis=-1, keepdims=True)
    zc = zc - mu
    var = jnp.mean(zc * zc, axis=-1, keepdims=True)
    zn = zc * lax.rsqrt(var + EPS) * lng_ref[...] + lnb_ref[...]
    yc = _silu(zn) * scg_ref[0].astype(F32)
    o = of_ref[0] + ob_ref[0]
    hn = hn_ref[...]
    parts = []
    for hd in range(o.shape[1] // HGRN_HEAD):
        sl = slice(hd * HGRN_HEAD, (hd + 1) * HGRN_HEAD)
        oh = o[:, sl]
        ms = jnp.mean(oh * oh, axis=-1, keepdims=True)
        parts.append(oh * lax.rsqrt(ms + EPS) * hn[:, sl])
    yd = jnp.concatenate(parts, axis=1) * sog_ref[0].astype(F32)
    y = (jnp.dot(yc.astype(BF16), wout_ref[0:d_conv, :], preferred_element_type=F32)
         + jnp.dot(yd.astype(BF16), wout_ref[d_conv:, :], preferred_element_type=F32))
    _postnorm_residual(x_ref, y, gpost_ref, mod_ref, o_ref)


def _outproj_odd(xc, glu, scg, cw, cb, lng, lnb, o_f, o_b, hnorm, sog, wout, gpost, mod, ncb, lc):
    bsz, ltot, d = xc.shape
    tm = ROW_TILE
    d_conv = glu.shape[-1]
    d_hgrn = o_f.shape[-1]
    rows, full2, modspec = _row_specs(tm, ncb)
    prev, nxt = _halo_specs(tm, CONV_HALO, d_conv, ltot)
    kern = functools.partial(_outproj_odd_kernel, tm=tm, ncb=ncb, lc=lc, ll=ltot - lc, d_conv=d_conv)
    return pl.pallas_call(
        kern,
        grid=(bsz, ltot // tm),
        in_specs=[rows(d), rows(d_conv), prev, nxt, rows(d_conv), full2(*cw.shape), full2(1, d_conv),
                  full2(1, d_conv), full2(1, d_conv), rows(d_hgrn), rows(d_hgrn), full2(1, d_hgrn),
                  rows(d_hgrn), full2(*wout.shape), full2(1, d), modspec(d)],
        out_specs=rows(d),
        out_shape=jax.ShapeDtypeStruct(xc.shape, F32),
        scratch_shapes=[pltpu.VMEM((tm + 2 * CONV_HALO, d_conv), F32)],
        name="outproj_odd",
        compiler_params=_cparams(("parallel", "parallel")),
    )(xc, glu, glu, glu, scg, cw, cb.reshape(1, d_conv), lng.reshape(1, d_conv), lnb.reshape(1, d_conv),
      o_f, o_b, hnorm.reshape(1, d_hgrn), sog, wout, gpost.reshape(1, d), mod)


def _rope_tables(lc, ll):
    t = jnp.arange(ll)
    inv = ROPE_BASE ** (-jnp.arange(ROPE_FREQS, dtype=F32) / ROPE_FREQS)
    ang = jnp.stack([t // GRID_W, t % GRID_W], axis=-1).astype(F32)[:, :, None] * inv
    cos = jnp.cos(ang)
    sin = jnp.sin(ang)
    cos64 = jnp.stack([cos, cos], axis=2).reshape(ll, DIFF_HEAD)
    sin64 = jnp.stack([-sin, sin], axis=2).reshape(ll, DIFF_HEAD)
    cos_t = jnp.concatenate([jnp.ones((lc, DIFF_HEAD), F32), cos64], axis=0)
    sin_t = jnp.concatenate([jnp.zeros((lc, DIFF_HEAD), F32), sin64], axis=0)
    return jnp.tile(cos_t, (1, LANES // DIFF_HEAD)), jnp.tile(sin_t, (1, LANES // DIFF_HEAD))


def _block_diag(w):
    g, a, b = w.shape
    out = jnp.zeros((g * a, g * b), w.dtype)
    for j in range(g):
        out = out.at[j * a:(j + 1) * a, j * b:(j + 1) * b].set(w[j])
    return out


def kernel(x, c, ctx, c_ctx, ada_w, ada_b, norm_pre, norm_post, w_in_even, w_out_even, pool_w, pool_scale,
           diff_lambda, diff_subln, w_in_odd, w_out_odd, conv_w, conv_b, conv_ln_g, conv_ln_b, hgrn_norm,
           hgrn_lb):
    bsz, ll, d = x.shape
    lc = ctx.shape[1]
    depth = ada_w.shape[0]
    d_pool = pool_scale.shape[-1]
    d_diff = w_out_even.shape[1] - d_pool
    d_conv = conv_b.shape[-1]
    d_hgrn = hgrn_norm.shape[-1]
    assert lc % ROW_TILE == 0 and ll % ROW_TILE == 0 and bsz + 1 <= SUBLANES
    ncb = lc // ROW_TILE

    cvec = jnp.zeros((SUBLANES, d), F32).at[:bsz].set(c).at[bsz].set(c_ctx)
    ada = _ada_call(cvec, ada_w, ada_b).reshape(depth, SUBLANES, 3, d)
    xc = jnp.concatenate([ctx, x], axis=1)
    cos, sin = _rope_tables(lc, ll)

    for l in range(depth):
        mod = jnp.stack([jnp.broadcast_to(ada[l, bsz], (bsz, 3, d)), ada[l, :bsz]], axis=1)
        j = l // 2
        if l % 2 == 0:
            lam_init = 0.8 - 0.6 * math.exp(-0.3 * l)
            wb = w_in_even[j].astype(BF16)
            q0, k0, v0, g0 = 2 * d_pool, 2 * d_pool + d_diff, 2 * d_pool + 2 * d_diff, 2 * d_pool + 3 * d_diff
            w_rows = jnp.concatenate([wb[:, :q0], wb[:, k0:v0], wb[:, g0:]], axis=1)
            w_cols_t = jnp.concatenate([wb[:, q0:k0], wb[:, v0:g0]], axis=1).T
            u, sga, k, sgb, qt, vt = _inproj_even(xc, norm_pre[l], mod, w_rows, w_cols_t, cos, sin, ncb,
                                                  d_pool, d_diff)
            yb = _attention(qt, k, vt, sgb, diff_lambda[j], diff_subln[j], lc, lam_init)
            xc = _outproj_even(xc, u, sga, yb, _block_diag(pool_w[j]).astype(BF16), pool_scale[j],
                               w_out_even[j].astype(BF16), norm_post[l], mod, ncb, lc)
        else:
            glu, scg, qh, ff, fb, iv, sog = _inproj_odd(xc, norm_pre[l], mod, w_in_odd[j].astype(BF16), ncb,
                                                        d_conv, d_hgrn)
            o_f, o_b = _hgrn(qh, ff, fb, iv, hgrn_lb, l, lc)
            xc = _outproj_odd(xc, glu, scg, conv_w[j], conv_b[j], conv_ln_g[j], conv_ln_b[j], o_f, o_b,
                              hgrn_norm[j], sog, w_out_odd[j].astype(BF16), norm_post[l], mod, ncb, lc)
    return xc[:, lc:]
```

```python
import functools
import math

import numpy as np
import jax
import jax.numpy as jnp
from jax import lax
from jax.experimental import pallas as pl
from jax.experimental.pallas import tpu as pltpu

F32 = jnp.float32
BF16 = jnp.bfloat16
EPS = 1e-6

GRID_W = 64
POOL_WINDOWS = (2, 4, 8, 16)
POOL_GROUP = 64
DIFF_HEAD = 64
DIFF_V = 2 * DIFF_HEAD
DIFF_SCALE = DIFF_HEAD ** -0.5
ROPE_BASE = 10000.0
ROPE_FREQS = DIFF_HEAD // 4
CONV_WIDTH = 31
CONV_HALO = 16
POOL_HALO = 8
HGRN_HEAD = 128
LANES = 128
SUBLANES = 8

ROW_TILE = 256
Q_TILE = 128
Q_SUBTILES = 2
KV_GROUP = 32
SCORE_SLOTS = 4
ONES_ROWS = 16
LOG2E = 1.4426950408889634
KV_TILE = 256
HGRN_CHUNK = 64
HGRN_INTERLEAVE = 8
VMEM_LIMIT = 48 * 1024 * 1024


def _cparams(sem):
    return pltpu.CompilerParams(dimension_semantics=sem, vmem_limit_bytes=VMEM_LIMIT)


def _silu(x):
    return x * jax.nn.sigmoid(x)


def _bdot(a, b):
    return jnp.dot(a.astype(BF16), b.astype(BF16), preferred_element_type=F32)


def _ada_kernel(c_ref, w_ref, b_ref, o_ref):
    c = c_ref[...]
    o_ref[0] = jnp.dot(_silu(c), w_ref[0], preferred_element_type=F32,
                       precision=lax.Precision.HIGHEST) + b_ref[0]


def _ada_call(cvec, ada_w, ada_b):
    depth, d, d3 = ada_w.shape
    nj = d3 // d
    return pl.pallas_call(
        _ada_kernel,
        grid=(depth, nj),
        in_specs=[pl.BlockSpec((SUBLANES, d), lambda l, j: (0, 0)),
                  pl.BlockSpec((1, d, d), lambda l, j: (l, 0, j)),
                  pl.BlockSpec((1, 1, d), lambda l, j: (l, 0, j))],
        out_specs=pl.BlockSpec((1, SUBLANES, d), lambda l, j: (l, 0, j)),
        out_shape=jax.ShapeDtypeStruct((depth, SUBLANES, d3), F32),
        name="ada_params",
        compiler_params=_cparams(("parallel", "parallel")),
    )(cvec, ada_w, ada_b.reshape(depth, 1, d3))


def _prenorm(x_ref, g_ref, mod_ref):
    x = x_ref[0]
    mod = mod_ref[0, 0]
    ms = jnp.mean(x * x, axis=-1, keepdims=True)
    h = x * lax.rsqrt(ms + EPS) * g_ref[...] * (1.0 + mod[1:2]) + mod[0:1]
    return h.astype(BF16)


def _postnorm_residual(x_ref, y, gpost_ref, mod_ref, o_ref):
    mod = mod_ref[0, 0]
    ms = jnp.mean(y * y, axis=-1, keepdims=True)
    yn = y * lax.rsqrt(ms + EPS) * gpost_ref[...]
    o_ref[0] = x_ref[0] + mod[2:3] * yn


def _row_specs(tm, ncb):
    def rows(width):
        return pl.BlockSpec((1, tm, width), lambda b, i: (b, i, 0))

    def full2(a, bdim):
        return pl.BlockSpec((a, bdim), lambda b, i: (0, 0))

    def mod(d):
        return pl.BlockSpec((1, 1, 3, d), lambda b, i: (b, jnp.where(i < ncb, 0, 1), 0, 0))

    return rows, full2, mod


def _inproj_even_kernel(x_ref, g_ref, mod_ref, w_ref, wt_ref, cos_ref, sin_ref, cost_ref, sint_ref,
                        u_ref, sga_ref, k_ref, sgb_ref, qt_ref, vt_ref, *, d_pool, d_diff):
    hb = _prenorm(x_ref, g_ref, mod_ref)

    def proj(c0, width):
        return jnp.dot(hb, w_ref[:, c0:c0 + width], preferred_element_type=F32)

    def proj_t(r0, height):
        return lax.dot_general(wt_ref[r0:r0 + height, :], hb, (((1,), (1,)), ((), ())),
                               preferred_element_type=F32)

    u_ref[0] = proj(0, d_pool)
    sga_ref[0] = _silu(proj(d_pool, d_pool))
    cos = cos_ref[...]
    sin = sin_ref[...]
    lane = lax.broadcasted_iota(jnp.int32, cos.shape, 1)
    first_half = (lane & ROPE_FREQS) == 0
    k0 = 2 * d_pool
    for s in range(d_diff // LANES):
        t = proj(k0 + s * LANES, LANES)
        partner = jnp.where(first_half, pltpu.roll(t, LANES - ROPE_FREQS, 1), pltpu.roll(t, ROPE_FREQS, 1))
        k_ref[0, :, s * LANES:(s + 1) * LANES] = (t * cos + partner * sin).astype(BF16)
    sgb_ref[0] = _silu(proj(k0 + d_diff, d_diff)).astype(BF16)
    cost = cost_ref[...]
    sint = sint_ref[...]
    f = ROPE_FREQS
    for s in range(d_diff // LANES):
        t = proj_t(s * LANES, LANES)
        partner = jnp.concatenate([t[(blk ^ 1) * f:((blk ^ 1) + 1) * f] for blk in range(LANES // f)], axis=0)
        qt_ref[0, s * LANES:(s + 1) * LANES, :] = ((t * cost + partner * sint) * (DIFF_SCALE * LOG2E)).astype(BF16)
    vt_ref[0] = proj_t(d_diff, d_diff).astype(BF16)


def _inproj_even(xc, g, mod, w, wt, cos, sin, ncb, d_pool, d_diff):
    bsz, ltot, d = xc.shape
    tm = ROW_TILE
    rows, full2, modspec = _row_specs(tm, ncb)
    tab = pl.BlockSpec((tm, LANES), lambda b, i: (i, 0))
    tab_t = pl.BlockSpec((LANES, tm), lambda b, i: (0, i))
    cols_t = pl.BlockSpec((1, d_diff, tm), lambda b, i: (b, 0, i))
    shp = lambda width, dt: jax.ShapeDtypeStruct((bsz, ltot, width), dt)
    shp_t = jax.ShapeDtypeStruct((bsz, d_diff, ltot), BF16)
    return pl.pallas_call(
        functools.partial(_inproj_even_kernel, d_pool=d_pool, d_diff=d_diff),
        grid=(bsz, ltot // tm),
        in_specs=[rows(d), full2(1, d), modspec(d), full2(*w.shape), full2(*wt.shape), tab, tab, tab_t, tab_t],
        out_specs=[rows(d_pool), rows(d_pool), rows(d_diff), rows(d_diff), cols_t, cols_t],
        out_shape=[shp(d_pool, F32), shp(d_pool, F32), shp(d_diff, BF16), shp(d_diff, BF16), shp_t, shp_t],
        name="inproj_even",
        compiler_params=_cparams(("parallel", "parallel")),
    )(xc, g.reshape(1, d), mod, w, wt, cos, sin, cos.T, sin.T)


def _attn_kernel(qt_ref, k_ref, vt_ref, sgb_ref, lam_ref, sub_ref, o_ref, acc_sc, st_sc, mx_sc,
                 *, nsub, nq, tk, n_ctx_q, n_ctx_kv, n_kv, group, lam_init):
    i = pl.program_id(2)
    row = lax.broadcasted_iota(jnp.int32, (DIFF_V, nq), 0)
    ws = []
    for c in range(nsub):
        qt = qt_ref[0, :, c * nq:(c + 1) * nq]
        zero = jnp.zeros_like(qt)
        ws.append(jnp.concatenate([jnp.where(row < DIFF_HEAD, qt, zero), jnp.where(row >= DIFF_HEAD, qt, zero)],
                                  axis=1))

    ones = jnp.ones((ONES_ROWS, tk), BF16)

    def scores(off, slot):
        kc = k_ref[0, pl.ds(off, tk), :]
        for c in range(nsub):
            st = jnp.dot(kc, ws[c], preferred_element_type=F32)
            st_sc[slot, c] = st
            mx_sc[slot, c] = jnp.max(st, axis=0, keepdims=True)

    def finish(pending):
        for c, (alpha, pv) in enumerate(pending):
            acc_sc[c] = pv if alpha is None else alpha * acc_sc[c] + pv

    def absorb(off, slot, ms, pending=None, ahead=None):
        new = [mx_sc[slot, c] if ms is None else jnp.maximum(ms[c], mx_sc[slot, c]) for c in range(nsub)]
        alphas = [None if ms is None else jnp.exp2(ms[c] - new[c]) for c in range(nsub)]
        if ahead is not None:
            ahead()
        if pending is not None:
            finish(pending)
        vte = jnp.concatenate([vt_ref[0, :, pl.ds(off, tk)], ones], axis=0)
        pvs = [jnp.dot(vte, jnp.exp2(st_sc[slot, c] - new[c]).astype(BF16), preferred_element_type=F32)
               for c in range(nsub)]
        return tuple(new), tuple(zip(alphas, pvs))

    @pl.when(i < n_ctx_q)
    def _():
        ms = None
        for n in range(n_ctx_kv):
            scores(n * tk, 0)
            ms, pending = absorb(n * tk, 0, ms)
            finish(pending)

    @pl.when(i >= n_ctx_q)
    def _():
        last = (n_kv - 1) * tk
        g = group
        slots = SCORE_SLOTS
        scores(0, 0)
        scores(tk, 1)
        ms, pending = None, None
        for n in range(n_ctx_kv):
            ms, pending = absorb(n * tk, n % slots, ms, pending,
                                 lambda o=(n + 2) * tk, s=(n + 2) % slots: scores(o, s))
        finish(pending)

        def trip(jj, ms):
            offs = [pl.multiple_of(jnp.minimum((n_ctx_kv + g * jj + t) * tk, last), tk) for t in range(g + 2)]
            pending = None
            for t in range(g):
                ms, pending = absorb(offs[t], (n_ctx_kv + t) % slots, ms, pending,
                                     lambda o=offs[t + 2], s=(n_ctx_kv + t + 2) % slots: scores(o, s))
            finish(pending)
            return ms

        lax.fori_loop(0, (n_kv - n_ctx_kv) // g, trip, ms)

    lp = lam_ref[...]
    lam = (jnp.exp(jnp.sum(lp[0:1] * lp[1:2], axis=1, keepdims=True))
           - jnp.exp(jnp.sum(lp[2:3] * lp[3:4], axis=1, keepdims=True)) + lam_init)
    for c in range(nsub):
        acc = acc_sc[c]
        o_all = acc[:DIFF_V] / acc[DIFF_V:DIFF_V + 1]
        o = (o_all[:, :nq] - lam * o_all[:, nq:]).T
        ms = jnp.mean(o * o, axis=-1, keepdims=True)
        on = o * lax.rsqrt(ms + EPS) * sub_ref[...] * (1.0 - lam_init)
        o_ref[0, c * nq:(c + 1) * nq, :] = (on * sgb_ref[0, c * nq:(c + 1) * nq, :].astype(F32)).astype(BF16)


def _attention(qt, k, vt, sgb, lam_p, subln, lc, lam_init):
    bsz, ltot, d_diff = k.shape
    heads = d_diff // DIFF_V
    nq, nsub, tk = Q_TILE, Q_SUBTILES, KV_TILE
    nqs = nq * nsub
    assert lc % nqs == 0 and lc % tk == 0 and ltot % nqs == 0 and ltot % tk == 0
    group = math.gcd(KV_GROUP, (ltot - lc) // tk)
    assert group % SCORE_SLOTS == 0
    rowspec = pl.BlockSpec((1, nqs, DIFF_V), lambda b, h, i: (b, i, h))
    kern = functools.partial(_attn_kernel, nsub=nsub, nq=nq, tk=tk, n_ctx_q=lc // nqs, n_ctx_kv=lc // tk,
                             n_kv=ltot // tk, group=group, lam_init=lam_init)
    return pl.pallas_call(
        kern,
        grid=(bsz, heads, ltot // nqs),
        in_specs=[pl.BlockSpec((1, DIFF_V, nqs), lambda b, h, i: (b, h, i)),
                  pl.BlockSpec((1, ltot, DIFF_V), lambda b, h, i: (b, 0, h)),
                  pl.BlockSpec((1, DIFF_V, ltot), lambda b, h, i: (b, h, 0)),
                  rowspec,
                  pl.BlockSpec(lam_p.shape, lambda b, h, i: (0, 0)),
                  pl.BlockSpec((1, DIFF_V), lambda b, h, i: (0, 0))],
        out_specs=rowspec,
        out_shape=jax.ShapeDtypeStruct((bsz, ltot, d_diff), BF16),
        scratch_shapes=[pltpu.VMEM((nsub, DIFF_V + ONES_ROWS, 2 * nq), F32),
                        pltpu.VMEM((SCORE_SLOTS, nsub, tk, 2 * nq), F32),
                        pltpu.VMEM((SCORE_SLOTS, nsub, 1, 2 * nq), F32)],
        name="diff_attention",
        compiler_params=_cparams(("parallel", "parallel", "arbitrary")),
    )(qt, k, vt, sgb, lam_p, subln.reshape(1, DIFF_V))


def _seq_position(i, tm, ncb, lc, ll):
    nblk = ncb + ll // tm
    is_ctx = i < ncb
    t0 = jnp.where(is_ctx, i, i - ncb) * tm
    lseq = jnp.where(is_ctx, lc, ll)
    has_prev = jnp.logical_and(i != 0, i != ncb)
    has_next = jnp.logical_and(i != ncb - 1, i != nblk - 1)
    return t0, lseq, has_prev, has_next


def _outproj_even_kernel(x_ref, u_ref, up_ref, un_ref, sga_ref, yb_ref, wpool_ref, pscale_ref, wout_ref,
                         gpost_ref, mod_ref, o_ref, *, tm, ncb, lc, ll, d_pool):
    i = pl.program_id(1)
    t0, lseq, has_prev, has_next = _seq_position(i, tm, ncb, lc, ll)
    u = u_ref[0]
    z = jnp.concatenate([jnp.where(has_prev, up_ref[0], 0.0), u, jnp.where(has_next, un_ref[0], 0.0)], axis=0)
    p2 = z[:-1] + z[1:]
    p4 = p2[:-2] + p2[2:]
    p8 = p4[:-4] + p4[4:]
    p16 = p8[:-8] + p8[8:]
    sums = (p2[7:7 + tm], p4[6:6 + tm], p8[4:4 + tm], p16[0:tm])
    lane = lax.broadcasted_iota(jnp.int32, u.shape, 1)
    grp = lane >> 6
    s = jnp.where(grp == 0, sums[0], jnp.where(grp == 1, sums[1], jnp.where(grp == 2, sums[2], sums[3])))
    half = jnp.left_shift(1, grp)
    t = lax.broadcasted_iota(jnp.int32, u.shape, 0) + t0
    cnt = jnp.minimum(t + half - 1, lseq - 1) + 1 - jnp.maximum(t - half, 0)
    dpool = s / cnt.astype(F32) - u
    ya = jnp.dot(dpool.astype(BF16), wpool_ref[...], preferred_element_type=F32) * pscale_ref[...] * sga_ref[0]
    y = (jnp.dot(ya.astype(BF16), wout_ref[0:d_pool, :], preferred_element_type=F32)
         + jnp.dot(yb_ref[0], wout_ref[d_pool:, :], preferred_element_type=F32))
    _postnorm_residual(x_ref, y, gpost_ref, mod_ref, o_ref)


def _halo_specs(tm, halo, width, ltot):
    r = tm // halo
    prev = pl.BlockSpec((1, halo, width), lambda b, i: (b, jnp.maximum(i * r - 1, 0), 0))
    nxt = pl.BlockSpec((1, halo, width), lambda b, i: (b, jnp.minimum((i + 1) * r, ltot // halo - 1), 0))
    return prev, nxt


def _outproj_even(xc, u, sga, yb, wpool, pscale, wout, gpost, mod, ncb, lc):
    bsz, ltot, d = xc.shape
    tm = ROW_TILE
    d_pool = u.shape[-1]
    rows, full2, modspec = _row_specs(tm, ncb)
    prev, nxt = _halo_specs(tm, POOL_HALO, d_pool, ltot)
    kern = functools.partial(_outproj_even_kernel, tm=tm, ncb=ncb, lc=lc, ll=ltot - lc, d_pool=d_pool)
    return pl.pallas_call(
        kern,
        grid=(bsz, ltot // tm),
        in_specs=[rows(d), rows(d_pool), prev, nxt, rows(d_pool), rows(yb.shape[-1]),
                  full2(*wpool.shape), full2(1, d_pool), full2(*wout.shape), full2(1, d), modspec(d)],
        out_specs=rows(d),
        out_shape=jax.ShapeDtypeStruct(xc.shape, F32),
        name="outproj_even",
        compiler_params=_cparams(("parallel", "parallel")),
    )(xc, u, u, u, sga, yb, wpool, pscale.reshape(1, d_pool), wout, gpost.reshape(1, d), mod)


def _inproj_odd_kernel(x_ref, g_ref, mod_ref, w_ref, glu_ref, scg_ref, qh_ref, ff_ref, fb_ref, iv_ref, sog_ref,
                       *, d_conv, d_hgrn):
    hb = _prenorm(x_ref, g_ref, mod_ref)

    def proj(c0, width):
        return jnp.dot(hb, w_ref[:, c0:c0 + width], preferred_element_type=F32)

    glu_ref[0] = proj(0, d_conv) * jax.nn.sigmoid(proj(d_conv, d_conv))
    scg_ref[0] = _silu(proj(2 * d_conv, d_conv)).astype(BF16)
    c0 = 3 * d_conv
    qh_ref[0] = _silu(proj(c0, d_hgrn)).astype(BF16)
    ff_ref[0] = proj(c0 + d_hgrn, d_hgrn)
    fb_ref[0] = proj(c0 + 2 * d_hgrn, d_hgrn)
    iv_ref[0] = proj(c0 + 3 * d_hgrn, d_hgrn).astype(BF16)
    sog_ref[0] = _silu(proj(c0 + 4 * d_hgrn, d_hgrn)).astype(BF16)


def _inproj_odd(xc, g, mod, w, ncb, d_conv, d_hgrn):
    bsz, ltot, d = xc.shape
    tm = ROW_TILE
    rows, full2, modspec = _row_specs(tm, ncb)
    shp = lambda width, dt: jax.ShapeDtypeStruct((bsz, ltot, width), dt)
    return pl.pallas_call(
        functools.partial(_inproj_odd_kernel, d_conv=d_conv, d_hgrn=d_hgrn),
        grid=(bsz, ltot // tm),
        in_specs=[rows(d), full2(1, d), modspec(d), full2(*w.shape)],
        out_specs=[rows(d_conv), rows(d_conv), rows(d_hgrn), rows(d_hgrn), rows(d_hgrn), rows(d_hgrn),
                   rows(d_hgrn)],
        out_shape=[shp(d_conv, F32), shp(d_conv, BF16), shp(d_hgrn, BF16), shp(d_hgrn, F32), shp(d_hgrn, F32),
                   shp(d_hgrn, BF16), shp(d_hgrn, BF16)],
        name="inproj_odd",
        compiler_params=_cparams(("parallel", "parallel")),
    )(xc, g.reshape(1, d), mod, w)


def _hgrn_consts(c, rev):
    ri = lax.broadcasted_iota(jnp.int32, (c, c), 0)
    ci = lax.broadcasted_iota(jnp.int32, (c, c), 1)
    tri = jnp.where((ci >= ri) if rev else (ci <= ri), 1.0, 0.0).astype(BF16)
    masks = {}
    w = c // 2
    while w >= SUBLANES:
        shift = int(math.log2(2 * w))
        same_pair = (ri >> shift) == (ci >> shift)
        r_odd, c_odd = (ri & w) != 0, (ci & w) != 0
        owns = jnp.logical_and(c_odd, jnp.logical_not(r_odd)) if rev else jnp.logical_and(r_odd, jnp.logical_not(c_odd))
        masks[w] = jnp.logical_and(same_pair, owns)
        w //= 2
    return tri, masks


def _cumsum_rows(x, tri):
    hi = x.astype(BF16)
    r1 = x - hi.astype(F32)
    mid = r1.astype(BF16)
    lo = (r1 - mid.astype(F32)).astype(BF16)
    n = x.shape[1]
    r = jnp.dot(tri, jnp.concatenate([hi, mid, lo], axis=1), preferred_element_type=F32)
    return r[:, :n] + r[:, n:2 * n] + r[:, 2 * n:]


class _Chain:
    def __init__(self, q_ref, logit_ref, v_ref, o_ref, idx, lb, st_ref, rev, consts):
        self.refs = (q_ref, logit_ref, v_ref, o_ref, idx, st_ref)
        self.lb, self.rev = lb, rev
        self.tri, self.masks = consts

    def gates(self):
        q_ref, logit_ref, v_ref, _, idx, _ = self.refs
        self.qf = q_ref[idx].astype(F32)
        self.vb = v_ref[idx]
        f = self.lb + (1.0 - self.lb) * jax.nn.sigmoid(logit_ref[idx])
        self.kk = 1.0 - f
        self.b = _cumsum_rows(jnp.log2(f), self.tri)

    def products(self):
        qf, kk, b, rev = self.qf, self.kk, self.b, self.rev
        c, n = qf.shape
        self.st = self.refs[5][...]
        self.inter = lax.dot_general((qf * jnp.exp2(b)).astype(BF16), self.st.astype(BF16),
                                     (((1,), (1,)), ((), ())), preferred_element_type=F32)
        self.levels = []
        w = c // 2
        while w >= SUBLANES:
            refs = []
            for p in range(c // (2 * w)):
                r0 = p * 2 * w + (w if rev else w - 1)
                refs.append(jnp.broadcast_to(b[r0:r0 + 1, :], (2 * w, n)))
            ref = refs[0] if len(refs) == 1 else jnp.concatenate(refs, axis=0)
            qs = jnp.exp2(jnp.minimum(b - ref, 0.0)) * qf
            ks = jnp.exp2(jnp.minimum(ref - b, 0.0)) * kk
            self.levels.append((w, lax.dot_general(qs.astype(BF16), ks.astype(BF16), (((1,), (1,)), ((), ())),
                                                   preferred_element_type=F32)))
            w //= 2
        self.bl = b[0:1, :] if rev else b[c - 1:c, :]
        kd = (kk * jnp.exp2(self.bl - b)).astype(BF16)
        self.st_add = lax.dot_general(self.vb, kd, (((0,), (0,)), ((), ())), preferred_element_type=F32)

    def diagonal(self):
        c, n = self.qf.shape
        blocks = lambda x: x.reshape(c // SUBLANES, SUBLANES, n)
        b3, k3, v3, q3 = blocks(self.b), blocks(self.kk), blocks(self.vb.astype(F32)), blocks(self.qf)
        sub = lax.broadcasted_iota(jnp.int32, (1, SUBLANES, 1), 1)
        diag = jnp.sum(q3 * k3, axis=2, keepdims=True) * v3
        for d in range(1, SUBLANES):
            sh = SUBLANES - d if self.rev else d
            valid = (sub + d <= SUBLANES - 1) if self.rev else (sub >= d)
            e = jnp.exp2(jnp.where(valid, b3 - pltpu.roll(b3, sh, 1), 0.0))
            a = jnp.sum(q3 * e * pltpu.roll(k3, sh, 1), axis=2, keepdims=True)
            diag = diag + jnp.where(valid, a, 0.0) * pltpu.roll(v3, sh, 1)
        self.diag = diag.reshape(c, n)

    def intra(self):
        att = functools.reduce(lambda x, y: x + y, [jnp.where(self.masks[w], a, 0.0) for w, a in self.levels])
        self.intra_out = jnp.dot(att.astype(BF16), self.vb, preferred_element_type=F32)
        self.refs[5][...] = self.st * jnp.exp2(self.bl) + self.st_add

    def finish(self):
        self.refs[3][self.refs[4]] = self.inter + self.intra_out + self.diag


def _hgrn_kernel(lbraw_ref, qf_ref, ff_ref, vf_ref, qb_ref, fb_ref, vb_ref, of_ref, ob_ref, st_sc,
                 *, layer, bsz, heads):
    s = pl.program_id(0)

    @pl.when(s == 0)
    def _():
        st_sc[...] = jnp.zeros(st_sc.shape, F32)

    raw = lbraw_ref[...]
    depth = raw.shape[1]
    rows = [raw[:, j, :] for j in range(depth)]
    mx = functools.reduce(jnp.maximum, rows)
    ex = [jnp.exp(r - mx) for r in rows]
    den = functools.reduce(lambda a, bb: a + bb, ex)
    if layer >= 1:
        lbs = functools.reduce(lambda a, bb: a + bb, [e / den for e in ex[1:layer + 1]])
    else:
        lbs = jnp.zeros_like(mx)
    c = qf_ref.shape[1]
    fwd, bwd = _hgrn_consts(c, False), _hgrn_consts(c, True)
    chains = []
    for bi in range(bsz):
        for h in range(heads):
            sl = slice(h * HGRN_HEAD, (h + 1) * HGRN_HEAD)
            idx = (bi, slice(None), sl)
            chains.append(_Chain(qf_ref, ff_ref, vf_ref, of_ref, idx, lbs[0:1, sl], st_sc.at[bi, 0, h], False, fwd))
            chains.append(_Chain(qb_ref, fb_ref, vb_ref, ob_ref, idx, lbs[1:2, sl], st_sc.at[bi, 1, h], True, bwd))
    for g0 in range(0, len(chains), HGRN_INTERLEAVE):
        group = chains[g0:g0 + HGRN_INTERLEAVE]
        for stage in (_Chain.gates, _Chain.products, _Chain.diagonal, _Chain.intra, _Chain.finish):
            for ch in group:
                stage(ch)


def _hgrn(qh, ff, fb, iv, hgrn_lb, layer, lc):
    bsz, ltot, d_hgrn = qh.shape
    heads = d_hgrn // HGRN_HEAD
    c = HGRN_CHUNK
    assert lc % c == 0 and ltot % c == 0
    ncc, ntot = lc // c, ltot // c

    def bwd(s):
        return jnp.where(s < ncc, ncc - 1 - s, ntot - 1 + ncc - s)

    fspec = pl.BlockSpec((bsz, c, d_hgrn), lambda s: (0, s, 0))
    bspec = pl.BlockSpec((bsz, c, d_hgrn), lambda s: (0, bwd(s), 0))
    kern = functools.partial(_hgrn_kernel, layer=layer, bsz=bsz, heads=heads)
    return pl.pallas_call(
        kern,
        grid=(ntot,),
        in_specs=[pl.BlockSpec(hgrn_lb.shape, lambda s: (0, 0, 0)), fspec, fspec, fspec, bspec, bspec, bspec],
        out_specs=[fspec, bspec],
        out_shape=[jax.ShapeDtypeStruct((bsz, ltot, d_hgrn), F32)] * 2,
        scratch_shapes=[pltpu.VMEM((bsz, 2, heads, HGRN_HEAD, HGRN_HEAD), F32)],
        name="hgrn_scan",
        compiler_params=_cparams(("arbitrary",)),
    )(hgrn_lb, qh, ff, iv, qh, fb, iv)


def _outproj_odd_kernel(x_ref, glu_ref, gp_ref, gn_ref, scg_ref, cw_ref, cb_ref, lng_ref, lnb_ref,
                        of_ref, ob_ref, hn_ref, sog_ref, wout_ref, gpost_ref, mod_ref, o_ref, z_sc, zs_sc,
                        *, tm, ncb, lc, ll, d_conv):
    i = pl.program_id(1)
    _, _, has_prev, has_next = _seq_position(i, tm, ncb, lc, ll)
    h = CONV_HALO
    z_sc[0:h, :] = jnp.where(has_prev, gp_ref[0], 0.0)
    z_sc[h:h + tm, :] = glu_ref[0]
    z_sc[h + tm:, :] = jnp.where(has_next, gn_ref[0], 0.0)
    cw = cw_ref[...]
    base = h - CONV_WIDTH // 2
    span = zs_sc.shape[1]
    for r in range(SUBLANES):
        zs_sc[r] = z_sc[r:r + span, :]
    acc = jnp.zeros((tm, d_conv), F32)
    for j in range(CONV_WIDTH):
        r = (base + j) % SUBLANES
        a0 = base + j - r
        acc = acc + cw[j:j + 1, :] * zs_sc[r, a0:a0 + tm, :]
    zc = acc + cb_ref[...]
    mu = jnp.mean(zc, axis=-1, keepdims=True)
    zc = zc - mu
    var = jnp.mean(zc * zc, axis=-1, keepdims=True)
    zn = zc * lax.rsqrt(var + EPS) * lng_ref[...] + lnb_ref[...]
    yc = _silu(zn) * scg_ref[0].astype(F32)
    o = of_ref[0] + ob_ref[0]
    hn = hn_ref[...]
    parts = []
    for hd in range(o.shape[1] // HGRN_HEAD):
        sl = slice(hd * HGRN_HEAD, (hd + 1) * HGRN_HEAD)
        oh = o[:, sl]
        ms = jnp.mean(oh * oh, axis=-1, keepdims=True)
        parts.append(oh * lax.rsqrt(ms + EPS) * hn[:, sl])
    yd = jnp.concatenate(parts, axis=1) * sog_ref[0].astype(F32)
    y = (jnp.dot(yc.astype(BF16), wout_ref[0:d_conv, :], preferred_element_type=F32)
         + jnp.dot(yd.astype(BF16), wout_ref[d_conv:, :], preferred_element_type=F32))
    _postnorm_residual(x_ref, y, gpost_ref, mod_ref, o_ref)


def _outproj_odd(xc, glu, scg, cw, cb, lng, lnb, o_f, o_b, hnorm, sog, wout, gpost, mod, ncb, lc):
    bsz, ltot, d = xc.shape
    tm = ROW_TILE
    d_conv = glu.shape[-1]
    d_hgrn = o_f.shape[-1]
    rows, full2, modspec = _row_specs(tm, ncb)
    prev, nxt = _halo_specs(tm, CONV_HALO, d_conv, ltot)
    conv_span = (CONV_HALO + CONV_WIDTH // 2) // SUBLANES * SUBLANES
    assert SUBLANES - 1 + tm + conv_span <= tm + 2 * CONV_HALO
    kern = functools.partial(_outproj_odd_kernel, tm=tm, ncb=ncb, lc=lc, ll=ltot - lc, d_conv=d_conv)
    return pl.pallas_call(
        kern,
        grid=(bsz, ltot // tm),
        in_specs=[rows(d), rows(d_conv), prev, nxt, rows(d_conv), full2(*cw.shape), full2(1, d_conv),
                  full2(1, d_conv), full2(1, d_conv), rows(d_hgrn), rows(d_hgrn), full2(1, d_hgrn),
                  rows(d_hgrn), full2(*wout.shape), full2(1, d), modspec(d)],
        out_specs=rows(d),
        out_shape=jax.ShapeDtypeStruct(xc.shape, F32),
        scratch_shapes=[pltpu.VMEM((tm + 2 * CONV_HALO, d_conv), F32),
                        pltpu.VMEM((SUBLANES, tm + conv_span, d_conv), F32)],
        name="outproj_odd",
        compiler_params=_cparams(("parallel", "parallel")),
    )(xc, glu, glu, glu, scg, cw, cb.reshape(1, d_conv), lng.reshape(1, d_conv), lnb.reshape(1, d_conv),
      o_f, o_b, hnorm.reshape(1, d_hgrn), sog, wout, gpost.reshape(1, d), mod)


def _rope_tables(lc, ll):
    t = jnp.arange(ll)
    inv = ROPE_BASE ** (-jnp.arange(ROPE_FREQS, dtype=F32) / ROPE_FREQS)
    ang = jnp.stack([t // GRID_W, t % GRID_W], axis=-1).astype(F32)[:, :, None] * inv
    cos = jnp.cos(ang)
    sin = jnp.sin(ang)
    cos64 = jnp.stack([cos, cos], axis=2).reshape(ll, DIFF_HEAD)
    sin64 = jnp.stack([-sin, sin], axis=2).reshape(ll, DIFF_HEAD)
    cos_t = jnp.concatenate([jnp.ones((lc, DIFF_HEAD), F32), cos64], axis=0)
    sin_t = jnp.concatenate([jnp.zeros((lc, DIFF_HEAD), F32), sin64], axis=0)
    return jnp.tile(cos_t, (1, LANES // DIFF_HEAD)), jnp.tile(sin_t, (1, LANES // DIFF_HEAD))


def _block_diag(w):
    g, a, b = w.shape
    out = jnp.zeros((g * a, g * b), w.dtype)
    for j in range(g):
        out = out.at[j * a:(j + 1) * a, j * b:(j + 1) * b].set(w[j])
    return out


def kernel(x, c, ctx, c_ctx, ada_w, ada_b, norm_pre, norm_post, w_in_even, w_out_even, pool_w, pool_scale,
           diff_lambda, diff_subln, w_in_odd, w_out_odd, conv_w, conv_b, conv_ln_g, conv_ln_b, hgrn_norm,
           hgrn_lb):
    bsz, ll, d = x.shape
    lc = ctx.shape[1]
    depth = ada_w.shape[0]
    d_pool = pool_scale.shape[-1]
    d_diff = w_out_even.shape[1] - d_pool
    d_conv = conv_b.shape[-1]
    d_hgrn = hgrn_norm.shape[-1]
    assert lc % ROW_TILE == 0 and ll % ROW_TILE == 0 and bsz + 1 <= SUBLANES
    ncb = lc // ROW_TILE

    cvec = jnp.zeros((SUBLANES, d), F32).at[:bsz].set(c).at[bsz].set(c_ctx)
    ada = _ada_call(cvec, ada_w, ada_b).reshape(depth, SUBLANES, 3, d)
    xc = jnp.concatenate([ctx, x], axis=1)
    cos, sin = _rope_tables(lc, ll)

    for l in range(depth):
        mod = jnp.stack([jnp.broadcast_to(ada[l, bsz], (bsz, 3, d)), ada[l, :bsz]], axis=1)
        j = l // 2
        if l % 2 == 0:
            lam_init = 0.8 - 0.6 * math.exp(-0.3 * l)
            wb = w_in_even[j].astype(BF16)
            q0, k0, v0, g0 = 2 * d_pool, 2 * d_pool + d_diff, 2 * d_pool + 2 * d_diff, 2 * d_pool + 3 * d_diff
            w_rows = jnp.concatenate([wb[:, :q0], wb[:, k0:v0], wb[:, g0:]], axis=1)
            w_cols_t = jnp.concatenate([wb[:, q0:k0], wb[:, v0:g0]], axis=1).T
            u, sga, k, sgb, qt, vt = _inproj_even(xc, norm_pre[l], mod, w_rows, w_cols_t, cos, sin, ncb,
                                                  d_pool, d_diff)
            yb = _attention(qt, k, vt, sgb, diff_lambda[j], diff_subln[j], lc, lam_init)
            xc = _outproj_even(xc, u, sga, yb, _block_diag(pool_w[j]).astype(BF16), pool_scale[j],
                               w_out_even[j].astype(BF16), norm_post[l], mod, ncb, lc)
        else:
            glu, scg, qh, ff, fb, iv, sog = _inproj_odd(xc, norm_pre[l], mod, w_in_odd[j].astype(BF16), ncb,
                                                        d_conv, d_hgrn)
            o_f, o_b = _hgrn(qh, ff, fb, iv, hgrn_lb, l, lc)
            xc = _outproj_odd(xc, glu, scg, conv_w[j], conv_b[j], conv_ln_g[j], conv_ln_b[j], o_f, o_b,
                              hgrn_norm[j], sog, w_out_odd[j].astype(BF16), norm_post[l], mod, ncb, lc)
    return xc[:, lc:]
```

```python
import functools
import math

import numpy as np
import jax
import jax.numpy as jnp
from jax import lax
from jax.experimental import pallas as pl
from jax.experimental.pallas import tpu as pltpu

F32 = jnp.float32
BF16 = jnp.bfloat16
EPS = 1e-6

GRID_W = 64
POOL_WINDOWS = (2, 4, 8, 16)
POOL_GROUP = 64
DIFF_HEAD = 64
DIFF_V = 2 * DIFF_HEAD
DIFF_SCALE = DIFF_HEAD ** -0.5
ROPE_BASE = 10000.0
ROPE_FREQS = DIFF_HEAD // 4
CONV_WIDTH = 31
CONV_HALO = 16
POOL_HALO = 8
HGRN_HEAD = 128
LANES = 128
SUBLANES = 8

ROW_TILE = 256
Q_TILE = 128
Q_SUBTILES = 2
KV_GROUP = 64
SCORE_SLOTS = 4
ONES_ROWS = 16
LOG2E = 1.4426950408889634
KV_TILE = 256
HGRN_CHUNK = 64
HGRN_DIAG = 2
HGRN_INTERLEAVE = 16
VMEM_LIMIT = 48 * 1024 * 1024


def _cparams(sem):
    return pltpu.CompilerParams(dimension_semantics=sem, vmem_limit_bytes=VMEM_LIMIT)


def _silu(x):
    return x * jax.nn.sigmoid(x)


def _bdot(a, b):
    return jnp.dot(a.astype(BF16), b.astype(BF16), preferred_element_type=F32)


def _ada_kernel(c_ref, w_ref, b_ref, o_ref):
    c = c_ref[...]
    o_ref[0] = jnp.dot(_silu(c), w_ref[0], preferred_element_type=F32,
                       precision=lax.Precision.HIGHEST) + b_ref[0]


def _ada_call(cvec, ada_w, ada_b):
    depth, d, d3 = ada_w.shape
    nj = d3 // d
    return pl.pallas_call(
        _ada_kernel,
        grid=(depth, nj),
        in_specs=[pl.BlockSpec((SUBLANES, d), lambda l, j: (0, 0)),
                  pl.BlockSpec((1, d, d), lambda l, j: (l, 0, j)),
                  pl.BlockSpec((1, 1, d), lambda l, j: (l, 0, j))],
        out_specs=pl.BlockSpec((1, SUBLANES, d), lambda l, j: (l, 0, j)),
        out_shape=jax.ShapeDtypeStruct((depth, SUBLANES, d3), F32),
        name="ada_params",
        compiler_params=_cparams(("parallel", "parallel")),
    )(cvec, ada_w, ada_b.reshape(depth, 1, d3))


def _prenorm(x_ref, g_ref, mod_ref):
    x = x_ref[0]
    mod = mod_ref[0, 0]
    ms = jnp.mean(x * x, axis=-1, keepdims=True)
    h = x * lax.rsqrt(ms + EPS) * g_ref[...] * (1.0 + mod[1:2]) + mod[0:1]
    return h.astype(BF16)


def _postnorm_residual(x_ref, y, gpost_ref, mod_ref, o_ref):
    mod = mod_ref[0, 0]
    ms = jnp.mean(y * y, axis=-1, keepdims=True)
    yn = y * lax.rsqrt(ms + EPS) * gpost_ref[...]
    o_ref[0] = x_ref[0] + mod[2:3] * yn


def _row_specs(tm, ncb):
    def rows(width):
        return pl.BlockSpec((1, tm, width), lambda b, i: (b, i, 0))

    def full2(a, bdim):
        return pl.BlockSpec((a, bdim), lambda b, i: (0, 0))

    def mod(d):
        return pl.BlockSpec((1, 1, 3, d), lambda b, i: (b, jnp.where(i < ncb, 0, 1), 0, 0))

    return rows, full2, mod


def _inproj_even_kernel(x_ref, g_ref, mod_ref, w_ref, wt_ref, cos_ref, sin_ref, cost_ref, sint_ref,
                        u_ref, sga_ref, k_ref, sgb_ref, qt_ref, vt_ref, *, d_pool, d_diff):
    hb = _prenorm(x_ref, g_ref, mod_ref)

    def proj(c0, width):
        return jnp.dot(hb, w_ref[:, c0:c0 + width], preferred_element_type=F32)

    def proj_t(r0, height):
        return lax.dot_general(wt_ref[r0:r0 + height, :], hb, (((1,), (1,)), ((), ())),
                               preferred_element_type=F32)

    u_ref[0] = proj(0, d_pool)
    sga_ref[0] = _silu(proj(d_pool, d_pool))
    cos = cos_ref[...]
    sin = sin_ref[...]
    lane = lax.broadcasted_iota(jnp.int32, cos.shape, 1)
    first_half = (lane & ROPE_FREQS) == 0
    k0 = 2 * d_pool
    for s in range(d_diff // LANES):
        t = proj(k0 + s * LANES, LANES)
        partner = jnp.where(first_half, pltpu.roll(t, LANES - ROPE_FREQS, 1), pltpu.roll(t, ROPE_FREQS, 1))
        k_ref[0, :, s * LANES:(s + 1) * LANES] = (t * cos + partner * sin).astype(BF16)
    sgb_ref[0] = _silu(proj(k0 + d_diff, d_diff)).astype(BF16)
    cost = cost_ref[...]
    sint = sint_ref[...]
    f = ROPE_FREQS
    for s in range(d_diff // LANES):
        t = proj_t(s * LANES, LANES)
        partner = jnp.concatenate([t[(blk ^ 1) * f:((blk ^ 1) + 1) * f] for blk in range(LANES // f)], axis=0)
        qt_ref[0, s * LANES:(s + 1) * LANES, :] = ((t * cost + partner * sint) * (DIFF_SCALE * LOG2E)).astype(BF16)
    vt_ref[0] = proj_t(d_diff, d_diff).astype(BF16)


def _inproj_even(xc, g, mod, w, wt, cos, sin, ncb, d_pool, d_diff):
    bsz, ltot, d = xc.shape
    tm = ROW_TILE
    rows, full2, modspec = _row_specs(tm, ncb)
    tab = pl.BlockSpec((tm, LANES), lambda b, i: (i, 0))
    tab_t = pl.BlockSpec((LANES, tm), lambda b, i: (0, i))
    cols_t = pl.BlockSpec((1, d_diff, tm), lambda b, i: (b, 0, i))
    shp = lambda width, dt: jax.ShapeDtypeStruct((bsz, ltot, width), dt)
    shp_t = jax.ShapeDtypeStruct((bsz, d_diff, ltot), BF16)
    return pl.pallas_call(
        functools.partial(_inproj_even_kernel, d_pool=d_pool, d_diff=d_diff),
        grid=(bsz, ltot // tm),
        in_specs=[rows(d), full2(1, d), modspec(d), full2(*w.shape), full2(*wt.shape), tab, tab, tab_t, tab_t],
        out_specs=[rows(d_pool), rows(d_pool), rows(d_diff), rows(d_diff), cols_t, cols_t],
        out_shape=[shp(d_pool, F32), shp(d_pool, F32), shp(d_diff, BF16), shp(d_diff, BF16), shp_t, shp_t],
        name="inproj_even",
        compiler_params=_cparams(("parallel", "parallel")),
    )(xc, g.reshape(1, d), mod, w, wt, cos, sin, cos.T, sin.T)


def _attn_kernel(qt_ref, k_ref, vt_ref, sgb_ref, lam_ref, sub_ref, o_ref, acc_sc, st_sc, mx_sc,
                 *, nsub, nq, tk, n_ctx_q, n_ctx_kv, n_kv, group, lam_init):
    i = pl.program_id(2)
    row = lax.broadcasted_iota(jnp.int32, (DIFF_V, nq), 0)
    ws = []
    for c in range(nsub):
        qt = qt_ref[0, :, c * nq:(c + 1) * nq]
        zero = jnp.zeros_like(qt)
        ws.append(jnp.concatenate([jnp.where(row < DIFF_HEAD, qt, zero), jnp.where(row >= DIFF_HEAD, qt, zero)],
                                  axis=1))

    ones = jnp.ones((ONES_ROWS, tk), BF16)

    def scores(off, slot):
        kc = k_ref[0, pl.ds(off, tk), :]
        for c in range(nsub):
            st = jnp.dot(kc, ws[c], preferred_element_type=F32)
            st_sc[slot, c] = st
            mx_sc[slot, c] = jnp.max(st, axis=0, keepdims=True)

    def finish(pending):
        for c, (alpha, pv) in enumerate(pending):
            acc_sc[c] = pv if alpha is None else alpha * acc_sc[c] + pv

    def absorb(off, slot, ms, pending=None, ahead=None):
        new = [mx_sc[slot, c] if ms is None else jnp.maximum(ms[c], mx_sc[slot, c]) for c in range(nsub)]
        alphas = [None if ms is None else jnp.exp2(ms[c] - new[c]) for c in range(nsub)]
        if ahead is not None:
            ahead()
        if pending is not None:
            finish(pending)
        vte = jnp.concatenate([vt_ref[0, :, pl.ds(off, tk)], ones], axis=0)
        pvs = [jnp.dot(vte, jnp.exp2(st_sc[slot, c] - new[c]).astype(BF16), preferred_element_type=F32)
               for c in range(nsub)]
        return tuple(new), tuple(zip(alphas, pvs))

    @pl.when(i < n_ctx_q)
    def _():
        ms = None
        for n in range(n_ctx_kv):
            scores(n * tk, 0)
            ms, pending = absorb(n * tk, 0, ms)
            finish(pending)

    @pl.when(i >= n_ctx_q)
    def _():
        last = (n_kv - 1) * tk
        g = group
        slots = SCORE_SLOTS
        n_static = n_kv if n_kv - n_ctx_kv <= g else n_ctx_kv
        scores(0, 0)
        scores(tk, 1)
        ms, pending = None, None
        for n in range(n_static):
            ahead = (lambda o=(n + 2) * tk, s=(n + 2) % slots: scores(o, s)) if n + 2 < n_kv else None
            ms, pending = absorb(n * tk, n % slots, ms, pending, ahead)
        finish(pending)
        if n_static == n_kv:
            return

        def trip(jj, ms):
            offs = [pl.multiple_of(jnp.minimum((n_ctx_kv + g * jj + t) * tk, last), tk) for t in range(g + 2)]
            pending = None
            for t in range(g):
                ms, pending = absorb(offs[t], (n_ctx_kv + t) % slots, ms, pending,
                                     lambda o=offs[t + 2], s=(n_ctx_kv + t + 2) % slots: scores(o, s))
            finish(pending)
            return ms

        lax.fori_loop(0, (n_kv - n_ctx_kv) // g, trip, ms)

    lp = lam_ref[...]
    lam = (jnp.exp(jnp.sum(lp[0:1] * lp[1:2], axis=1, keepdims=True))
           - jnp.exp(jnp.sum(lp[2:3] * lp[3:4], axis=1, keepdims=True)) + lam_init)
    for c in range(nsub):
        acc = acc_sc[c]
        o_all = acc[:DIFF_V] / acc[DIFF_V:DIFF_V + 1]
        o = (o_all[:, :nq] - lam * o_all[:, nq:]).T
        ms = jnp.mean(o * o, axis=-1, keepdims=True)
        on = o * lax.rsqrt(ms + EPS) * sub_ref[...] * (1.0 - lam_init)
        o_ref[0, c * nq:(c + 1) * nq, :] = (on * sgb_ref[0, c * nq:(c + 1) * nq, :].astype(F32)).astype(BF16)


def _attention(qt, k, vt, sgb, lam_p, subln, lc, lam_init):
    bsz, ltot, d_diff = k.shape
    heads = d_diff // DIFF_V
    nq, nsub, tk = Q_TILE, Q_SUBTILES, KV_TILE
    nqs = nq * nsub
    assert lc % nqs == 0 and lc % tk == 0 and ltot % nqs == 0 and ltot % tk == 0
    group = math.gcd(KV_GROUP, (ltot - lc) // tk)
    assert group % SCORE_SLOTS == 0
    rowspec = pl.BlockSpec((1, nqs, DIFF_V), lambda b, h, i: (b, i, h))
    kern = functools.partial(_attn_kernel, nsub=nsub, nq=nq, tk=tk, n_ctx_q=lc // nqs, n_ctx_kv=lc // tk,
                             n_kv=ltot // tk, group=group, lam_init=lam_init)
    return pl.pallas_call(
        kern,
        grid=(bsz, heads, ltot // nqs),
        in_specs=[pl.BlockSpec((1, DIFF_V, nqs), lambda b, h, i: (b, h, i)),
                  pl.BlockSpec((1, ltot, DIFF_V), lambda b, h, i: (b, 0, h)),
                  pl.BlockSpec((1, DIFF_V, ltot), lambda b, h, i: (b, h, 0)),
                  rowspec,
                  pl.BlockSpec(lam_p.shape, lambda b, h, i: (0, 0)),
                  pl.BlockSpec((1, DIFF_V), lambda b, h, i: (0, 0))],
        out_specs=rowspec,
        out_shape=jax.ShapeDtypeStruct((bsz, ltot, d_diff), BF16),
        scratch_shapes=[pltpu.VMEM((nsub, DIFF_V + ONES_ROWS, 2 * nq), F32),
                        pltpu.VMEM((SCORE_SLOTS, nsub, tk, 2 * nq), F32),
                        pltpu.VMEM((SCORE_SLOTS, nsub, 1, 2 * nq), F32)],
        name="diff_attention",
        compiler_params=_cparams(("parallel", "parallel", "arbitrary")),
    )(qt, k, vt, sgb, lam_p, subln.reshape(1, DIFF_V))


def _seq_position(i, tm, ncb, lc, ll):
    nblk = ncb + ll // tm
    is_ctx = i < ncb
    t0 = jnp.where(is_ctx, i, i - ncb) * tm
    lseq = jnp.where(is_ctx, lc, ll)
    has_prev = jnp.logical_and(i != 0, i != ncb)
    has_next = jnp.logical_and(i != ncb - 1, i != nblk - 1)
    return t0, lseq, has_prev, has_next


def _outproj_even_kernel(x_ref, u_ref, up_ref, un_ref, sga_ref, yb_ref, wpool_ref, pscale_ref, wout_ref,
                         gpost_ref, mod_ref, o_ref, *, tm, ncb, lc, ll, d_pool):
    i = pl.program_id(1)
    t0, lseq, has_prev, has_next = _seq_position(i, tm, ncb, lc, ll)
    u = u_ref[0]
    z = jnp.concatenate([jnp.where(has_prev, up_ref[0], 0.0), u, jnp.where(has_next, un_ref[0], 0.0)], axis=0)
    p2 = z[:-1] + z[1:]
    p4 = p2[:-2] + p2[2:]
    p8 = p4[:-4] + p4[4:]
    p16 = p8[:-8] + p8[8:]
    sums = (p2[7:7 + tm], p4[6:6 + tm], p8[4:4 + tm], p16[0:tm])
    lane = lax.broadcasted_iota(jnp.int32, u.shape, 1)
    grp = lane >> 6
    s = jnp.where(grp == 0, sums[0], jnp.where(grp == 1, sums[1], jnp.where(grp == 2, sums[2], sums[3])))
    half = jnp.left_shift(1, grp)
    t = lax.broadcasted_iota(jnp.int32, u.shape, 0) + t0
    cnt = jnp.minimum(t + half - 1, lseq - 1) + 1 - jnp.maximum(t - half, 0)
    dpool = s / cnt.astype(F32) - u
    ya = jnp.dot(dpool.astype(BF16), wpool_ref[...], preferred_element_type=F32) * pscale_ref[...] * sga_ref[0]
    y = (jnp.dot(ya.astype(BF16), wout_ref[0:d_pool, :], preferred_element_type=F32)
         + jnp.dot(yb_ref[0], wout_ref[d_pool:, :], preferred_element_type=F32))
    _postnorm_residual(x_ref, y, gpost_ref, mod_ref, o_ref)


def _halo_specs(tm, halo, width, ltot):
    r = tm // halo
    prev = pl.BlockSpec((1, halo, width), lambda b, i: (b, jnp.maximum(i * r - 1, 0), 0))
    nxt = pl.BlockSpec((1, halo, width), lambda b, i: (b, jnp.minimum((i + 1) * r, ltot // halo - 1), 0))
    return prev, nxt


def _outproj_even(xc, u, sga, yb, wpool, pscale, wout, gpost, mod, ncb, lc):
    bsz, ltot, d = xc.shape
    tm = ROW_TILE
    d_pool = u.shape[-1]
    rows, full2, modspec = _row_specs(tm, ncb)
    prev, nxt = _halo_specs(tm, POOL_HALO, d_pool, ltot)
    kern = functools.partial(_outproj_even_kernel, tm=tm, ncb=ncb, lc=lc, ll=ltot - lc, d_pool=d_pool)
    return pl.pallas_call(
        kern,
        grid=(bsz, ltot // tm),
        in_specs=[rows(d), rows(d_pool), prev, nxt, rows(d_pool), rows(yb.shape[-1]),
                  full2(*wpool.shape), full2(1, d_pool), full2(*wout.shape), full2(1, d), modspec(d)],
        out_specs=rows(d),
        out_shape=jax.ShapeDtypeStruct(xc.shape, F32),
        name="outproj_even",
        compiler_params=_cparams(("parallel", "parallel")),
    )(xc, u, u, u, sga, yb, wpool, pscale.reshape(1, d_pool), wout, gpost.reshape(1, d), mod)


def _inproj_odd_kernel(x_ref, g_ref, mod_ref, w_ref, glu_ref, scg_ref, qh_ref, ff_ref, fb_ref, iv_ref, sog_ref,
                       *, d_conv, d_hgrn):
    hb = _prenorm(x_ref, g_ref, mod_ref)

    def proj(c0, width):
        return jnp.dot(hb, w_ref[:, c0:c0 + width], preferred_element_type=F32)

    glu_ref[0] = proj(0, d_conv) * jax.nn.sigmoid(proj(d_conv, d_conv))
    scg_ref[0] = _silu(proj(2 * d_conv, d_conv)).astype(BF16)
    c0 = 3 * d_conv
    qh_ref[0] = _silu(proj(c0, d_hgrn)).astype(BF16)
    ff_ref[0] = proj(c0 + d_hgrn, d_hgrn)
    fb_ref[0] = proj(c0 + 2 * d_hgrn, d_hgrn)
    iv_ref[0] = proj(c0 + 3 * d_hgrn, d_hgrn).astype(BF16)
    sog_ref[0] = _silu(proj(c0 + 4 * d_hgrn, d_hgrn)).astype(BF16)


def _inproj_odd(xc, g, mod, w, ncb, d_conv, d_hgrn):
    bsz, ltot, d = xc.shape
    tm = ROW_TILE
    rows, full2, modspec = _row_specs(tm, ncb)
    shp = lambda width, dt: jax.ShapeDtypeStruct((bsz, ltot, width), dt)
    return pl.pallas_call(
        functools.partial(_inproj_odd_kernel, d_conv=d_conv, d_hgrn=d_hgrn),
        grid=(bsz, ltot // tm),
        in_specs=[rows(d), full2(1, d), modspec(d), full2(*w.shape)],
        out_specs=[rows(d_conv), rows(d_conv), rows(d_hgrn), rows(d_hgrn), rows(d_hgrn), rows(d_hgrn),
                   rows(d_hgrn)],
        out_shape=[shp(d_conv, F32), shp(d_conv, BF16), shp(d_hgrn, BF16), shp(d_hgrn, F32), shp(d_hgrn, F32),
                   shp(d_hgrn, BF16), shp(d_hgrn, BF16)],
        name="inproj_odd",
        compiler_params=_cparams(("parallel", "parallel")),
    )(xc, g.reshape(1, d), mod, w)


def _hgrn_consts(c, rev):
    ri = lax.broadcasted_iota(jnp.int32, (c, c), 0)
    ci = lax.broadcasted_iota(jnp.int32, (c, c), 1)
    tri = jnp.where((ci >= ri) if rev else (ci <= ri), 1.0, 0.0).astype(BF16)
    masks = {}
    w = c // 2
    while w >= HGRN_DIAG:
        shift = int(math.log2(2 * w))
        same_pair = (ri >> shift) == (ci >> shift)
        r_odd, c_odd = (ri & w) != 0, (ci & w) != 0
        owns = jnp.logical_and(c_odd, jnp.logical_not(r_odd)) if rev else jnp.logical_and(r_odd, jnp.logical_not(c_odd))
        masks[w] = jnp.logical_and(same_pair, owns)
        w //= 2
    return tri, masks


def _cumsum_rows(x, tri):
    hi = x.astype(BF16)
    r1 = x - hi.astype(F32)
    mid = r1.astype(BF16)
    lo = (r1 - mid.astype(F32)).astype(BF16)
    n = x.shape[1]
    r = jnp.dot(tri, jnp.concatenate([hi, mid, lo], axis=1), preferred_element_type=F32)
    return r[:, :n] + r[:, n:2 * n] + r[:, 2 * n:]


class _Chain:
    def __init__(self, q_ref, logit_ref, v_ref, o_ref, idx, lb, st_ref, rev, consts):
        self.refs = (q_ref, logit_ref, v_ref, o_ref, idx, st_ref)
        self.lb, self.rev = lb, rev
        self.tri, self.masks = consts

    def gates(self):
        q_ref, logit_ref, v_ref, _, idx, _ = self.refs
        self.qf = q_ref[idx].astype(F32)
        self.vb = v_ref[idx]
        f = self.lb + (1.0 - self.lb) * jax.nn.sigmoid(logit_ref[idx])
        self.kk = 1.0 - f
        self.b = _cumsum_rows(jnp.log2(f), self.tri)

    def products(self):
        qf, kk, b, rev = self.qf, self.kk, self.b, self.rev
        c, n = qf.shape
        self.st = self.refs[5][...]
        self.inter = lax.dot_general((qf * jnp.exp2(b)).astype(BF16), self.st.astype(BF16),
                                     (((1,), (1,)), ((), ())), preferred_element_type=F32)
        self.levels = []
        b3 = b.reshape(c // SUBLANES, SUBLANES, n)
        sub = lax.broadcasted_iota(jnp.int32, (1, SUBLANES, 1), 1)
        w = c // 2
        while w >= HGRN_DIAG:
            if 2 * w >= SUBLANES:
                refs = []
                for p in range(c // (2 * w)):
                    r0 = p * 2 * w + (w if rev else w - 1)
                    refs.append(jnp.broadcast_to(b[r0:r0 + 1, :], (2 * w, n)))
                ref = refs[0] if len(refs) == 1 else jnp.concatenate(refs, axis=0)
            else:
                ref3 = None
                for p in range(SUBLANES // (2 * w)):
                    r0 = p * 2 * w + (w if rev else w - 1)
                    piece = jnp.broadcast_to(b3[:, r0:r0 + 1, :], b3.shape)
                    ref3 = piece if ref3 is None else jnp.where(sub >= p * 2 * w, piece, ref3)
                ref = ref3.reshape(c, n)
            qs = jnp.exp2(jnp.minimum(b - ref, 0.0)) * qf
            ks = jnp.exp2(jnp.minimum(ref - b, 0.0)) * kk
            self.levels.append((w, lax.dot_general(qs.astype(BF16), ks.astype(BF16), (((1,), (1,)), ((), ())),
                                                   preferred_element_type=F32)))
            w //= 2
        self.bl = b[0:1, :] if rev else b[c - 1:c, :]
        kd = (kk * jnp.exp2(self.bl - b)).astype(BF16)
        self.st_add = lax.dot_general(self.vb, kd, (((0,), (0,)), ((), ())), preferred_element_type=F32)

    def diagonal(self):
        c, n = self.qf.shape
        blocks = lambda x: x.reshape(c // SUBLANES, SUBLANES, n)
        b3, k3, v3, q3 = blocks(self.b), blocks(self.kk), blocks(self.vb.astype(F32)), blocks(self.qf)
        sub = lax.broadcasted_iota(jnp.int32, (1, SUBLANES, 1), 1) & (HGRN_DIAG - 1)
        diag = jnp.sum(q3 * k3, axis=2, keepdims=True) * v3
        for d in range(1, HGRN_DIAG):
            sh = SUBLANES - d if self.rev else d
            valid = (sub + d <= HGRN_DIAG - 1) if self.rev else (sub >= d)
            e = jnp.exp2(jnp.where(valid, b3 - pltpu.roll(b3, sh, 1), 0.0))
            a = jnp.sum(q3 * e * pltpu.roll(k3, sh, 1), axis=2, keepdims=True)
            diag = diag + jnp.where(valid, a, 0.0) * pltpu.roll(v3, sh, 1)
        self.diag = diag.reshape(c, n)

    def intra(self):
        att = functools.reduce(lambda x, y: x + y, [jnp.where(self.masks[w], a, 0.0) for w, a in self.levels])
        self.intra_out = jnp.dot(att.astype(BF16), self.vb, preferred_element_type=F32)
        self.refs[5][...] = self.st * jnp.exp2(self.bl) + self.st_add

    def finish(self):
        self.refs[3][self.refs[4]] = self.inter + self.intra_out + self.diag


def _hgrn_kernel(lbraw_ref, qf_ref, ff_ref, vf_ref, qb_ref, fb_ref, vb_ref, of_ref, ob_ref, st_sc,
                 *, layer, bsz, heads):
    s = pl.program_id(0)

    @pl.when(s == 0)
    def _():
        st_sc[...] = jnp.zeros(st_sc.shape, F32)

    raw = lbraw_ref[...]
    depth = raw.shape[1]
    rows = [raw[:, j, :] for j in range(depth)]
    mx = functools.reduce(jnp.maximum, rows)
    ex = [jnp.exp(r - mx) for r in rows]
    den = functools.reduce(lambda a, bb: a + bb, ex)
    if layer >= 1:
        lbs = functools.reduce(lambda a, bb: a + bb, [e / den for e in ex[1:layer + 1]])
    else:
        lbs = jnp.zeros_like(mx)
    c = qf_ref.shape[1]
    fwd, bwd = _hgrn_consts(c, False), _hgrn_consts(c, True)
    chains = []
    for bi in range(bsz):
        for h in range(heads):
            sl = slice(h * HGRN_HEAD, (h + 1) * HGRN_HEAD)
            idx = (bi, slice(None), sl)
            chains.append(_Chain(qf_ref, ff_ref, vf_ref, of_ref, idx, lbs[0:1, sl], st_sc.at[bi, 0, h], False, fwd))
            chains.append(_Chain(qb_ref, fb_ref, vb_ref, ob_ref, idx, lbs[1:2, sl], st_sc.at[bi, 1, h], True, bwd))
    for g0 in range(0, len(chains), HGRN_INTERLEAVE):
        group = chains[g0:g0 + HGRN_INTERLEAVE]
        for stage in (_Chain.gates, _Chain.products, _Chain.intra, _Chain.diagonal, _Chain.finish):
            for ch in group:
                stage(ch)


def _hgrn(qh, ff, fb, iv, hgrn_lb, layer, lc):
    bsz, ltot, d_hgrn = qh.shape
    heads = d_hgrn // HGRN_HEAD
    c = HGRN_CHUNK
    assert lc % c == 0 and ltot % c == 0
    ncc, ntot = lc // c, ltot // c

    def bwd(s):
        return jnp.where(s < ncc, ncc - 1 - s, ntot - 1 + ncc - s)

    fspec = pl.BlockSpec((bsz, c, d_hgrn), lambda s: (0, s, 0))
    bspec = pl.BlockSpec((bsz, c, d_hgrn), lambda s: (0, bwd(s), 0))
    kern = functools.partial(_hgrn_kernel, layer=layer, bsz=bsz, heads=heads)
    return pl.pallas_call(
        kern,
        grid=(ntot,),
        in_specs=[pl.BlockSpec(hgrn_lb.shape, lambda s: (0, 0, 0)), fspec, fspec, fspec, bspec, bspec, bspec],
        out_specs=[fspec, bspec],
        out_shape=[jax.ShapeDtypeStruct((bsz, ltot, d_hgrn), F32)] * 2,
        scratch_shapes=[pltpu.VMEM((bsz, 2, heads, HGRN_HEAD, HGRN_HEAD), F32)],
        name="hgrn_scan",
        compiler_params=_cparams(("arbitrary",)),
    )(hgrn_lb, qh, ff, iv, qh, fb, iv)


def _outproj_odd_kernel(x_ref, glu_ref, gp_ref, gn_ref, scg_ref, cw_ref, cb_ref, lng_ref, lnb_ref,
                        of_ref, ob_ref, hn_ref, sog_ref, wout_ref, gpost_ref, mod_ref, o_ref, z_sc, zs_sc,
                        *, tm, ncb, lc, ll, d_conv):
    i = pl.program_id(1)
    _, _, has_prev, has_next = _seq_position(i, tm, ncb, lc, ll)
    h = CONV_HALO
    z_sc[0:h, :] = jnp.where(has_prev, gp_ref[0], 0.0)
    z_sc[h:h + tm, :] = glu_ref[0]
    z_sc[h + tm:, :] = jnp.where(has_next, gn_ref[0], 0.0)
    cw = cw_ref[...]
    base = h - CONV_WIDTH // 2
    span = zs_sc.shape[1]
    for r in range(SUBLANES):
        zs_sc[r] = z_sc[r:r + span, :]
    acc = jnp.zeros((tm, d_conv), F32)
    for j in range(CONV_WIDTH):
        r = (base + j) % SUBLANES
        a0 = base + j - r
        acc = acc + cw[j:j + 1, :] * zs_sc[r, a0:a0 + tm, :]
    zc = acc + cb_ref[...]
    mu = jnp.mean(zc, axis=-1, keepdims=True)
    zc = zc - mu
    var = jnp.mean(zc * zc, axis=-1, keepdims=True)
    zn = zc * lax.rsqrt(var + EPS) * lng_ref[...] + lnb_ref[...]
    yc = _silu(zn) * scg_ref[0].astype(F32)
    o = of_ref[0] + ob_ref[0]
    hn = hn_ref[...]
    parts = []
    for hd in range(o.shape[1] // HGRN_HEAD):
        sl = slice(hd * HGRN_HEAD, (hd + 1) * HGRN_HEAD)
        oh = o[:, sl]
        ms = jnp.mean(oh * oh, axis=-1, keepdims=True)
        parts.append(oh * lax.rsqrt(ms + EPS) * hn[:, sl])
    yd = jnp.concatenate(parts, axis=1) * sog_ref[0].astype(F32)
    y = (jnp.dot(yc.astype(BF16), wout_ref[0:d_conv, :], preferred_element_type=F32)
         + jnp.dot(yd.astype(BF16), wout_ref[d_conv:, :], preferred_element_type=F32))
    _postnorm_residual(x_ref, y, gpost_ref, mod_ref, o_ref)


def _outproj_odd(xc, glu, scg, cw, cb, lng, lnb, o_f, o_b, hnorm, sog, wout, gpost, mod, ncb, lc):
    bsz, ltot, d = xc.shape
    tm = ROW_TILE
    d_conv = glu.shape[-1]
    d_hgrn = o_f.shape[-1]
    rows, full2, modspec = _row_specs(tm, ncb)
    prev, nxt = _halo_specs(tm, CONV_HALO, d_conv, ltot)
    conv_span = (CONV_HALO + CONV_WIDTH // 2) // SUBLANES * SUBLANES
    assert SUBLANES - 1 + tm + conv_span <= tm + 2 * CONV_HALO
    kern = functools.partial(_outproj_odd_kernel, tm=tm, ncb=ncb, lc=lc, ll=ltot - lc, d_conv=d_conv)
    return pl.pallas_call(
        kern,
        grid=(bsz, ltot // tm),
        in_specs=[rows(d), rows(d_conv), prev, nxt, rows(d_conv), full2(*cw.shape), full2(1, d_conv),
                  full2(1, d_conv), full2(1, d_conv), rows(d_hgrn), rows(d_hgrn), full2(1, d_hgrn),
                  rows(d_hgrn), full2(*wout.shape), full2(1, d), modspec(d)],
        out_specs=rows(d),
        out_shape=jax.ShapeDtypeStruct(xc.shape, F32),
        scratch_shapes=[pltpu.VMEM((tm + 2 * CONV_HALO, d_conv), F32),
                        pltpu.VMEM((SUBLANES, tm + conv_span, d_conv), F32)],
        name="outproj_odd",
        compiler_params=_cparams(("parallel", "parallel")),
    )(xc, glu, glu, glu, scg, cw, cb.reshape(1, d_conv), lng.reshape(1, d_conv), lnb.reshape(1, d_conv),
      o_f, o_b, hnorm.reshape(1, d_hgrn), sog, wout, gpost.reshape(1, d), mod)


def _rope_tables(lc, ll):
    t = jnp.arange(ll)
    inv = ROPE_BASE ** (-jnp.arange(ROPE_FREQS, dtype=F32) / ROPE_FREQS)
    ang = jnp.stack([t // GRID_W, t % GRID_W], axis=-1).astype(F32)[:, :, None] * inv
    cos = jnp.cos(ang)
    sin = jnp.sin(ang)
    cos64 = jnp.stack([cos, cos], axis=2).reshape(ll, DIFF_HEAD)
    sin64 = jnp.stack([-sin, sin], axis=2).reshape(ll, DIFF_HEAD)
    cos_t = jnp.concatenate([jnp.ones((lc, DIFF_HEAD), F32), cos64], axis=0)
    sin_t = jnp.concatenate([jnp.zeros((lc, DIFF_HEAD), F32), sin64], axis=0)
    return jnp.tile(cos_t, (1, LANES // DIFF_HEAD)), jnp.tile(sin_t, (1, LANES // DIFF_HEAD))


def _block_diag(w):
    g, a, b = w.shape
    out = jnp.zeros((g * a, g * b), w.dtype)
    for j in range(g):
        out = out.at[j * a:(j + 1) * a, j * b:(j + 1) * b].set(w[j])
    return out


def kernel(x, c, ctx, c_ctx, ada_w, ada_b, norm_pre, norm_post, w_in_even, w_out_even, pool_w, pool_scale,
           diff_lambda, diff_subln, w_in_odd, w_out_odd, conv_w, conv_b, conv_ln_g, conv_ln_b, hgrn_norm,
           hgrn_lb):
    bsz, ll, d = x.shape
    lc = ctx.shape[1]
    depth = ada_w.shape[0]
    d_pool = pool_scale.shape[-1]
    d_diff = w_out_even.shape[1] - d_pool
    d_conv = conv_b.shape[-1]
    d_hgrn = hgrn_norm.shape[-1]
    assert lc % ROW_TILE == 0 and ll % ROW_TILE == 0 and bsz + 1 <= SUBLANES
    ncb = lc // ROW_TILE

    cvec = jnp.zeros((SUBLANES, d), F32).at[:bsz].set(c).at[bsz].set(c_ctx)
    ada = _ada_call(cvec, ada_w, ada_b).reshape(depth, SUBLANES, 3, d)
    xc = jnp.concatenate([ctx, x], axis=1)
    cos, sin = _rope_tables(lc, ll)

    for l in range(depth):
        mod = jnp.stack([jnp.broadcast_to(ada[l, bsz], (bsz, 3, d)), ada[l, :bsz]], axis=1)
        j = l // 2
        if l % 2 == 0:
            lam_init = 0.8 - 0.6 * math.exp(-0.3 * l)
            wb = w_in_even[j].astype(BF16)
            q0, k0, v0, g0 = 2 * d_pool, 2 * d_pool + d_diff, 2 * d_pool + 2 * d_diff, 2 * d_pool + 3 * d_diff
            w_rows = jnp.concatenate([wb[:, :q0], wb[:, k0:v0], wb[:, g0:]], axis=1)
            w_cols_t = jnp.concatenate([wb[:, q0:k0], wb[:, v0:g0]], axis=1).T
            u, sga, k, sgb, qt, vt = _inproj_even(xc, norm_pre[l], mod, w_rows, w_cols_t, cos, sin, ncb,
                                                  d_pool, d_diff)
            yb = _attention(qt, k, vt, sgb, diff_lambda[j], diff_subln[j], lc, lam_init)
            xc = _outproj_even(xc, u, sga, yb, _block_diag(pool_w[j]).astype(BF16), pool_scale[j],
                               w_out_even[j].astype(BF16), norm_post[l], mod, ncb, lc)
        else:
            glu, scg, qh, ff, fb, iv, sog = _inproj_odd(xc, norm_pre[l], mod, w_in_odd[j].astype(BF16), ncb,
                                                        d_conv, d_hgrn)
            o_f, o_b = _hgrn(qh, ff, fb, iv, hgrn_lb, l, lc)
            xc = _outproj_odd(xc, glu, scg, conv_w[j], conv_b[j], conv_ln_g[j], conv_ln_b[j], o_f, o_b,
                              hgrn_norm[j], sog, w_out_odd[j].astype(BF16), norm_post[l], mod, ncb, lc)
    return xc[:, lc:]
```

```python
import functools
import math

import numpy as np
import jax
import jax.numpy as jnp
from jax import lax
from jax.experimental import pallas as pl
from jax.experimental.pallas import tpu as pltpu

F32 = jnp.float32
BF16 = jnp.bfloat16
EPS = 1e-6

GRID_W = 64
POOL_WINDOWS = (2, 4, 8, 16)
POOL_GROUP = 64
DIFF_HEAD = 64
DIFF_V = 2 * DIFF_HEAD
DIFF_SCALE = DIFF_HEAD ** -0.5
ROPE_BASE = 10000.0
ROPE_FREQS = DIFF_HEAD // 4
CONV_WIDTH = 31
CONV_HALO = 16
POOL_HALO = 8
HGRN_HEAD = 128
LANES = 128
SUBLANES = 8

ROW_TILE = 256
Q_TILE = 128
Q_SUBTILES = 2
KV_GROUP = 64
SCORE_SLOTS = 4
ONES_ROWS = 16
LOG2E = 1.4426950408889634
KV_TILE = 256
HGRN_CHUNK = 64
HGRN_DIAG = 2
HGRN_INTERLEAVE = 16
VMEM_LIMIT = 48 * 1024 * 1024


def _cparams(sem):
    return pltpu.CompilerParams(dimension_semantics=sem, vmem_limit_bytes=VMEM_LIMIT)


def _silu(x):
    return x * jax.nn.sigmoid(x)


def _bdot(a, b):
    return jnp.dot(a.astype(BF16), b.astype(BF16), preferred_element_type=F32)


def _ada_kernel(c_ref, w_ref, b_ref, o_ref):
    c = c_ref[...]
    o_ref[0] = jnp.dot(_silu(c), w_ref[0], preferred_element_type=F32,
                       precision=lax.Precision.HIGHEST) + b_ref[0]


def _ada_call(cvec, ada_w, ada_b):
    depth, d, d3 = ada_w.shape
    nj = d3 // d
    return pl.pallas_call(
        _ada_kernel,
        grid=(depth, nj),
        in_specs=[pl.BlockSpec((SUBLANES, d), lambda l, j: (0, 0)),
                  pl.BlockSpec((1, d, d), lambda l, j: (l, 0, j)),
                  pl.BlockSpec((1, 1, d), lambda l, j: (l, 0, j))],
        out_specs=pl.BlockSpec((1, SUBLANES, d), lambda l, j: (l, 0, j)),
        out_shape=jax.ShapeDtypeStruct((depth, SUBLANES, d3), F32),
        name="ada_params",
        compiler_params=_cparams(("parallel", "parallel")),
    )(cvec, ada_w, ada_b.reshape(depth, 1, d3))


def _prenorm(x_ref, g_ref, mod_ref):
    x = x_ref[0]
    mod = mod_ref[0, 0]
    ms = jnp.mean(x * x, axis=-1, keepdims=True)
    h = x * lax.rsqrt(ms + EPS) * g_ref[...] * (1.0 + mod[1:2]) + mod[0:1]
    return h.astype(BF16)


def _pipelined(stages):
    nxt = stages[0][0]()
    for g, (_, consume) in enumerate(stages):
        cur = nxt
        if g + 1 < len(stages):
            nxt = stages[g + 1][0]()
        consume(cur)


def _postnorm_residual(x_ref, y, gpost_ref, mod_ref, o_ref):
    mod = mod_ref[0, 0]
    ms = jnp.mean(y * y, axis=-1, keepdims=True)
    yn = y * lax.rsqrt(ms + EPS) * gpost_ref[...]
    o_ref[0] = x_ref[0] + mod[2:3] * yn


def _row_specs(tm, ncb):
    def rows(width):
        return pl.BlockSpec((1, tm, width), lambda b, i: (b, i, 0))

    def full2(a, bdim):
        return pl.BlockSpec((a, bdim), lambda b, i: (0, 0))

    def mod(d):
        return pl.BlockSpec((1, 1, 3, d), lambda b, i: (b, jnp.where(i < ncb, 0, 1), 0, 0))

    return rows, full2, mod


def _inproj_even_kernel(x_ref, g_ref, mod_ref, w_ref, wt_ref, cos_ref, sin_ref, cost_ref, sint_ref,
                        u_ref, sga_ref, k_ref, sgb_ref, qt_ref, vt_ref, *, d_pool, d_diff):
    hb = _prenorm(x_ref, g_ref, mod_ref)
    wide = 2 * LANES

    def proj(c0, width):
        return lambda: jnp.dot(hb, w_ref[:, c0:c0 + width], preferred_element_type=F32)

    def proj_t(r0, height):
        return lambda: lax.dot_general(wt_ref[r0:r0 + height, :], hb, (((1,), (1,)), ((), ())),
                                       preferred_element_type=F32)

    def pool(r):
        u_ref[0] = r[:, :d_pool]
        sga_ref[0] = _silu(r[:, d_pool:])

    lane = lax.broadcasted_iota(jnp.int32, cos_ref.shape, 1)
    first_half = (lane & ROPE_FREQS) == 0

    def keys(c0):
        def put(r):
            for s in range(wide // LANES):
                t = r[:, s * LANES:(s + 1) * LANES]
                partner = jnp.where(first_half, pltpu.roll(t, LANES - ROPE_FREQS, 1), pltpu.roll(t, ROPE_FREQS, 1))
                k_ref[0, :, c0 + s * LANES:c0 + (s + 1) * LANES] = (t * cos_ref[...] + partner * sin_ref[...]).astype(BF16)
        return put

    def gate(c0):
        def put(r):
            sgb_ref[0, :, c0:c0 + wide] = _silu(r).astype(BF16)
        return put

    def queries(r0):
        f = ROPE_FREQS

        def put(r):
            for s in range(wide // LANES):
                t = r[s * LANES:(s + 1) * LANES]
                partner = jnp.concatenate([t[(blk ^ 1) * f:((blk ^ 1) + 1) * f] for blk in range(LANES // f)], axis=0)
                qt_ref[0, r0 + s * LANES:r0 + (s + 1) * LANES, :] = (
                    (t * cost_ref[...] + partner * sint_ref[...]) * (DIFF_SCALE * LOG2E)).astype(BF16)
        return put

    def values(r0):
        def put(r):
            vt_ref[0, r0:r0 + wide, :] = r.astype(BF16)
        return put

    k0 = 2 * d_pool
    stages = [(proj(0, 2 * d_pool), pool)]
    stages += [(proj(k0 + c, wide), keys(c)) for c in range(0, d_diff, wide)]
    stages += [(proj(k0 + d_diff + c, wide), gate(c)) for c in range(0, d_diff, wide)]
    stages += [(proj_t(r, wide), queries(r)) for r in range(0, d_diff, wide)]
    stages += [(proj_t(d_diff + r, wide), values(r)) for r in range(0, d_diff, wide)]
    _pipelined(stages)


def _inproj_even(xc, g, mod, w, wt, cos, sin, ncb, d_pool, d_diff):
    bsz, ltot, d = xc.shape
    tm = ROW_TILE
    rows, full2, modspec = _row_specs(tm, ncb)
    tab = pl.BlockSpec((tm, LANES), lambda b, i: (i, 0))
    tab_t = pl.BlockSpec((LANES, tm), lambda b, i: (0, i))
    cols_t = pl.BlockSpec((1, d_diff, tm), lambda b, i: (b, 0, i))
    shp = lambda width, dt: jax.ShapeDtypeStruct((bsz, ltot, width), dt)
    shp_t = jax.ShapeDtypeStruct((bsz, d_diff, ltot), BF16)
    return pl.pallas_call(
        functools.partial(_inproj_even_kernel, d_pool=d_pool, d_diff=d_diff),
        grid=(bsz, ltot // tm),
        in_specs=[rows(d), full2(1, d), modspec(d), full2(*w.shape), full2(*wt.shape), tab, tab, tab_t, tab_t],
        out_specs=[rows(d_pool), rows(d_pool), rows(d_diff), rows(d_diff), cols_t, cols_t],
        out_shape=[shp(d_pool, F32), shp(d_pool, F32), shp(d_diff, BF16), shp(d_diff, BF16), shp_t, shp_t],
        name="inproj_even",
        compiler_params=_cparams(("parallel", "parallel")),
    )(xc, g.reshape(1, d), mod, w, wt, cos, sin, cos.T, sin.T)


def _attn_kernel(qt_ref, k_ref, vt_ref, sgb_ref, lam_ref, sub_ref, o_ref, acc_sc, st_sc, mx_sc,
                 *, nsub, nq, tk, n_ctx_q, n_ctx_kv, n_kv, group, lam_init):
    i = pl.program_id(2)
    row = lax.broadcasted_iota(jnp.int32, (DIFF_V, nq), 0)
    ws = []
    for c in range(nsub):
        qt = qt_ref[0, :, c * nq:(c + 1) * nq]
        zero = jnp.zeros_like(qt)
        ws.append(jnp.concatenate([jnp.where(row < DIFF_HEAD, qt, zero), jnp.where(row >= DIFF_HEAD, qt, zero)],
                                  axis=1))

    ones = jnp.ones((ONES_ROWS, tk), BF16)

    def scores(off, slot):
        kc = k_ref[0, pl.ds(off, tk), :]
        for c in range(nsub):
            st = jnp.dot(kc, ws[c], preferred_element_type=F32)
            st_sc[slot, c] = st
            mx_sc[slot, c] = jnp.max(st, axis=0, keepdims=True)

    def finish(pending):
        for c, (alpha, pv) in enumerate(pending):
            acc_sc[c] = pv if alpha is None else alpha * acc_sc[c] + pv

    def absorb(off, slot, ms, pending=None, ahead=None):
        new = [mx_sc[slot, c] if ms is None else jnp.maximum(ms[c], mx_sc[slot, c]) for c in range(nsub)]
        alphas = [None if ms is None else jnp.exp2(ms[c] - new[c]) for c in range(nsub)]
        if ahead is not None:
            ahead()
        if pending is not None:
            finish(pending)
        vte = jnp.concatenate([vt_ref[0, :, pl.ds(off, tk)], ones], axis=0)
        pvs = [jnp.dot(vte, jnp.exp2(st_sc[slot, c] - new[c]).astype(BF16), preferred_element_type=F32)
               for c in range(nsub)]
        return tuple(new), tuple(zip(alphas, pvs))

    @pl.when(i < n_ctx_q)
    def _():
        ms = None
        for n in range(n_ctx_kv):
            scores(n * tk, 0)
            ms, pending = absorb(n * tk, 0, ms)
            finish(pending)

    @pl.when(i >= n_ctx_q)
    def _():
        last = (n_kv - 1) * tk
        g = group
        slots = SCORE_SLOTS
        n_static = n_kv if n_kv - n_ctx_kv <= g else n_ctx_kv
        scores(0, 0)
        scores(tk, 1)
        ms, pending = None, None
        for n in range(n_static):
            ahead = (lambda o=(n + 2) * tk, s=(n + 2) % slots: scores(o, s)) if n + 2 < n_kv else None
            ms, pending = absorb(n * tk, n % slots, ms, pending, ahead)
        finish(pending)
        if n_static == n_kv:
            return

        def trip(jj, ms):
            offs = [pl.multiple_of(jnp.minimum((n_ctx_kv + g * jj + t) * tk, last), tk) for t in range(g + 2)]
            pending = None
            for t in range(g):
                ms, pending = absorb(offs[t], (n_ctx_kv + t) % slots, ms, pending,
                                     lambda o=offs[t + 2], s=(n_ctx_kv + t + 2) % slots: scores(o, s))
            finish(pending)
            return ms

        lax.fori_loop(0, (n_kv - n_ctx_kv) // g, trip, ms)

    lp = lam_ref[...]
    lam = (jnp.exp(jnp.sum(lp[0:1] * lp[1:2], axis=1, keepdims=True))
           - jnp.exp(jnp.sum(lp[2:3] * lp[3:4], axis=1, keepdims=True)) + lam_init)
    accs = [acc_sc[c] for c in range(nsub)]
    o_alls = [acc[:DIFF_V] / acc[DIFF_V:DIFF_V + 1] for acc in accs]
    os = [(o_all[:, :nq] - lam * o_all[:, nq:]).T for o_all in o_alls]
    mss = [jnp.mean(o * o, axis=-1, keepdims=True) for o in os]
    gain = sub_ref[...] * (1.0 - lam_init)
    for c in range(nsub):
        on = os[c] * lax.rsqrt(mss[c] + EPS) * gain
        o_ref[0, c * nq:(c + 1) * nq, :] = (on * sgb_ref[0, c * nq:(c + 1) * nq, :].astype(F32)).astype(BF16)


def _attention(qt, k, vt, sgb, lam_p, subln, lc, lam_init):
    bsz, ltot, d_diff = k.shape
    heads = d_diff // DIFF_V
    nq, nsub, tk = Q_TILE, Q_SUBTILES, KV_TILE
    nqs = nq * nsub
    assert lc % nqs == 0 and lc % tk == 0 and ltot % nqs == 0 and ltot % tk == 0
    group = math.gcd(KV_GROUP, (ltot - lc) // tk)
    assert group % SCORE_SLOTS == 0
    rowspec = pl.BlockSpec((1, nqs, DIFF_V), lambda b, h, i: (b, i, h))
    kern = functools.partial(_attn_kernel, nsub=nsub, nq=nq, tk=tk, n_ctx_q=lc // nqs, n_ctx_kv=lc // tk,
                             n_kv=ltot // tk, group=group, lam_init=lam_init)
    return pl.pallas_call(
        kern,
        grid=(bsz, heads, ltot // nqs),
        in_specs=[pl.BlockSpec((1, DIFF_V, nqs), lambda b, h, i: (b, h, i)),
                  pl.BlockSpec((1, ltot, DIFF_V), lambda b, h, i: (b, 0, h)),
                  pl.BlockSpec((1, DIFF_V, ltot), lambda b, h, i: (b, h, 0)),
                  rowspec,
                  pl.BlockSpec(lam_p.shape, lambda b, h, i: (0, 0)),
                  pl.BlockSpec((1, DIFF_V), lambda b, h, i: (0, 0))],
        out_specs=rowspec,
        out_shape=jax.ShapeDtypeStruct((bsz, ltot, d_diff), BF16),
        scratch_shapes=[pltpu.VMEM((nsub, DIFF_V + ONES_ROWS, 2 * nq), F32),
                        pltpu.VMEM((SCORE_SLOTS, nsub, tk, 2 * nq), F32),
                        pltpu.VMEM((SCORE_SLOTS, nsub, 1, 2 * nq), F32)],
        name="diff_attention",
        compiler_params=_cparams(("parallel", "parallel", "arbitrary")),
    )(qt, k, vt, sgb, lam_p, subln.reshape(1, DIFF_V))


def _seq_position(i, tm, ncb, lc, ll):
    nblk = ncb + ll // tm
    is_ctx = i < ncb
    t0 = jnp.where(is_ctx, i, i - ncb) * tm
    lseq = jnp.where(is_ctx, lc, ll)
    has_prev = jnp.logical_and(i != 0, i != ncb)
    has_next = jnp.logical_and(i != ncb - 1, i != nblk - 1)
    return t0, lseq, has_prev, has_next


def _outproj_even_kernel(x_ref, u_ref, up_ref, un_ref, sga_ref, yb_ref, wpool_ref, pscale_ref, wout_ref,
                         gpost_ref, mod_ref, o_ref, *, tm, ncb, lc, ll, d_pool):
    i = pl.program_id(1)
    t0, lseq, has_prev, has_next = _seq_position(i, tm, ncb, lc, ll)
    y_attn = jnp.dot(yb_ref[0], wout_ref[d_pool:, :], preferred_element_type=F32)
    u = u_ref[0]
    z = jnp.concatenate([jnp.where(has_prev, up_ref[0], 0.0), u, jnp.where(has_next, un_ref[0], 0.0)], axis=0)
    p2 = z[:-1] + z[1:]
    p4 = p2[:-2] + p2[2:]
    p8 = p4[:-4] + p4[4:]
    p16 = p8[:-8] + p8[8:]
    sums = (p2[7:7 + tm], p4[6:6 + tm], p8[4:4 + tm], p16[0:tm])
    lane = lax.broadcasted_iota(jnp.int32, u.shape, 1)
    grp = lane >> 6
    s = jnp.where(grp == 0, sums[0], jnp.where(grp == 1, sums[1], jnp.where(grp == 2, sums[2], sums[3])))
    half = jnp.left_shift(1, grp)
    t = lax.broadcasted_iota(jnp.int32, u.shape, 0) + t0
    cnt = jnp.minimum(t + half - 1, lseq - 1) + 1 - jnp.maximum(t - half, 0)
    dpool = s / cnt.astype(F32) - u
    ya = jnp.dot(dpool.astype(BF16), wpool_ref[...], preferred_element_type=F32) * pscale_ref[...] * sga_ref[0]
    y = jnp.dot(ya.astype(BF16), wout_ref[0:d_pool, :], preferred_element_type=F32) + y_attn
    _postnorm_residual(x_ref, y, gpost_ref, mod_ref, o_ref)


def _halo_specs(tm, halo, width, ltot):
    r = tm // halo
    prev = pl.BlockSpec((1, halo, width), lambda b, i: (b, jnp.maximum(i * r - 1, 0), 0))
    nxt = pl.BlockSpec((1, halo, width), lambda b, i: (b, jnp.minimum((i + 1) * r, ltot // halo - 1), 0))
    return prev, nxt


def _outproj_even(xc, u, sga, yb, wpool, pscale, wout, gpost, mod, ncb, lc):
    bsz, ltot, d = xc.shape
    tm = ROW_TILE
    d_pool = u.shape[-1]
    rows, full2, modspec = _row_specs(tm, ncb)
    prev, nxt = _halo_specs(tm, POOL_HALO, d_pool, ltot)
    kern = functools.partial(_outproj_even_kernel, tm=tm, ncb=ncb, lc=lc, ll=ltot - lc, d_pool=d_pool)
    return pl.pallas_call(
        kern,
        grid=(bsz, ltot // tm),
        in_specs=[rows(d), rows(d_pool), prev, nxt, rows(d_pool), rows(yb.shape[-1]),
                  full2(*wpool.shape), full2(1, d_pool), full2(*wout.shape), full2(1, d), modspec(d)],
        out_specs=rows(d),
        out_shape=jax.ShapeDtypeStruct(xc.shape, F32),
        name="outproj_even",
        compiler_params=_cparams(("parallel", "parallel")),
    )(xc, u, u, u, sga, yb, wpool, pscale.reshape(1, d_pool), wout, gpost.reshape(1, d), mod)


def _hgrn_lower_bounds(raw, layer):
    rows = [raw[:, j, :] for j in range(raw.shape[1])]
    mx = functools.reduce(jnp.maximum, rows)
    ex = [jnp.exp(r - mx) for r in rows]
    den = functools.reduce(lambda a, bb: a + bb, ex)
    if layer == 0:
        return jnp.zeros_like(mx)
    return functools.reduce(lambda a, bb: a + bb, [e / den for e in ex[1:layer + 1]])


def _chunk_tri(n, c, rev):
    ri = lax.broadcasted_iota(jnp.int32, (n, n), 0)
    ci = lax.broadcasted_iota(jnp.int32, (n, n), 1)
    shift = int(math.log2(c))
    keep = jnp.logical_and((ri >> shift) == (ci >> shift), (ci >= ri) if rev else (ci <= ri))
    return jnp.where(keep, 1.0, 0.0).astype(BF16)


def _cumsum_rows(x, tri):
    hi = x.astype(BF16)
    r1 = x - hi.astype(F32)
    mid = r1.astype(BF16)
    lo = (r1 - mid.astype(F32)).astype(BF16)
    n = x.shape[1]
    r = jnp.dot(tri, jnp.concatenate([hi, mid, lo], axis=1), preferred_element_type=F32)
    return r[:, :n] + r[:, n:2 * n] + r[:, 2 * n:]


def _inproj_odd_kernel(x_ref, g_ref, mod_ref, w_ref, lbraw_ref, glu_ref, scg_ref, qh_ref, bf_ref, kf_ref, bb_ref,
                       kb_ref, iv_ref, sog_ref, *, d_conv, d_hgrn, layer):
    hb = _prenorm(x_ref, g_ref, mod_ref)
    tm = hb.shape[0]
    lbs = _hgrn_lower_bounds(lbraw_ref[...], layer)

    def glu(r):
        glu_ref[0] = r[:, :d_conv] * jax.nn.sigmoid(r[:, d_conv:])

    def silu_to(ref):
        def put(r):
            ref[0] = _silu(r).astype(BF16)
        return put

    def gate(dirn, b_ref, k_ref):
        def put(r):
            lb = lbs[dirn:dirn + 1]
            f = lb + (1.0 - lb) * jax.nn.sigmoid(r)
            k_ref[0] = (1.0 - f).astype(BF16)
            b_ref[0] = _cumsum_rows(jnp.log2(f), _chunk_tri(tm, HGRN_CHUNK, bool(dirn)))
        return put

    def value(r):
        iv_ref[0] = r.astype(BF16)

    def proj(c0, width):
        return lambda: jnp.dot(hb, w_ref[:, c0:c0 + width], preferred_element_type=F32)

    c0 = 3 * d_conv
    _pipelined([(proj(0, 2 * d_conv), glu), (proj(2 * d_conv, d_conv), silu_to(scg_ref)),
                (proj(c0, d_hgrn), silu_to(qh_ref)), (proj(c0 + d_hgrn, d_hgrn), gate(0, bf_ref, kf_ref)),
                (proj(c0 + 2 * d_hgrn, d_hgrn), gate(1, bb_ref, kb_ref)),
                (proj(c0 + 3 * d_hgrn, d_hgrn), value), (proj(c0 + 4 * d_hgrn, d_hgrn), silu_to(sog_ref))])


def _inproj_odd(xc, g, mod, w, hgrn_lb, layer, ncb, d_conv, d_hgrn):
    bsz, ltot, d = xc.shape
    tm = ROW_TILE
    assert tm % HGRN_CHUNK == 0
    rows, full2, modspec = _row_specs(tm, ncb)
    shp = lambda width, dt: jax.ShapeDtypeStruct((bsz, ltot, width), dt)
    return pl.pallas_call(
        functools.partial(_inproj_odd_kernel, d_conv=d_conv, d_hgrn=d_hgrn, layer=layer),
        grid=(bsz, ltot // tm),
        in_specs=[rows(d), full2(1, d), modspec(d), full2(*w.shape),
                  pl.BlockSpec(hgrn_lb.shape, lambda b, i: (0, 0, 0))],
        out_specs=[rows(d_conv), rows(d_conv), rows(d_hgrn), rows(d_hgrn), rows(d_hgrn), rows(d_hgrn),
                   rows(d_hgrn), rows(d_hgrn), rows(d_hgrn)],
        out_shape=[shp(d_conv, F32), shp(d_conv, BF16), shp(d_hgrn, BF16), shp(d_hgrn, F32), shp(d_hgrn, BF16),
                   shp(d_hgrn, F32), shp(d_hgrn, BF16), shp(d_hgrn, BF16), shp(d_hgrn, BF16)],
        name="inproj_odd",
        compiler_params=_cparams(("parallel", "parallel")),
    )(xc, g.reshape(1, d), mod, w, hgrn_lb)


def _hgrn_masks(c, rev):
    ri = lax.broadcasted_iota(jnp.int32, (c, c), 0)
    ci = lax.broadcasted_iota(jnp.int32, (c, c), 1)
    masks = {}
    w = c // 2
    while w >= HGRN_DIAG:
        shift = int(math.log2(2 * w))
        same_pair = (ri >> shift) == (ci >> shift)
        r_odd, c_odd = (ri & w) != 0, (ci & w) != 0
        owns = jnp.logical_and(c_odd, jnp.logical_not(r_odd)) if rev else jnp.logical_and(r_odd, jnp.logical_not(c_odd))
        masks[w] = jnp.logical_and(same_pair, owns)
        w //= 2
    return masks


class _Chain:
    def __init__(self, q_ref, b_ref, k_ref, v_ref, o_ref, idx, st_ref, rev, masks):
        self.refs = (q_ref, b_ref, k_ref, v_ref, o_ref, idx, st_ref)
        self.rev, self.masks = rev, masks

    def gates(self):
        q_ref, b_ref, k_ref, v_ref, _, idx, _ = self.refs
        self.qf = q_ref[idx].astype(F32)
        self.vb = v_ref[idx]
        self.kk = k_ref[idx].astype(F32)
        self.b = b_ref[idx]

    def products(self):
        qf, kk, b, rev = self.qf, self.kk, self.b, self.rev
        c, n = qf.shape
        self.st = self.refs[6][...]
        self.inter = lax.dot_general((qf * jnp.exp2(b)).astype(BF16), self.st.astype(BF16),
                                     (((1,), (1,)), ((), ())), preferred_element_type=F32)
        self.levels = []
        b3 = b.reshape(c // SUBLANES, SUBLANES, n)
        sub = lax.broadcasted_iota(jnp.int32, (1, SUBLANES, 1), 1)
        w = c // 2
        while w >= HGRN_DIAG:
            if 2 * w >= SUBLANES:
                refs = []
                for p in range(c // (2 * w)):
                    r0 = p * 2 * w + (w if rev else w - 1)
                    refs.append(jnp.broadcast_to(b[r0:r0 + 1, :], (2 * w, n)))
                ref = refs[0] if len(refs) == 1 else jnp.concatenate(refs, axis=0)
            else:
                ref3 = None
                for p in range(SUBLANES // (2 * w)):
                    r0 = p * 2 * w + (w if rev else w - 1)
                    piece = jnp.broadcast_to(b3[:, r0:r0 + 1, :], b3.shape)
                    ref3 = piece if ref3 is None else jnp.where(sub >= p * 2 * w, piece, ref3)
                ref = ref3.reshape(c, n)
            qs = jnp.exp2(jnp.minimum(b - ref, 0.0)) * qf
            ks = jnp.exp2(jnp.minimum(ref - b, 0.0)) * kk
            self.levels.append((w, lax.dot_general(qs.astype(BF16), ks.astype(BF16), (((1,), (1,)), ((), ())),
                                                   preferred_element_type=F32)))
            w //= 2
        self.bl = b[0:1, :] if rev else b[c - 1:c, :]
        kd = (kk * jnp.exp2(self.bl - b)).astype(BF16)
        self.st_add = lax.dot_general(self.vb, kd, (((0,), (0,)), ((), ())), preferred_element_type=F32)

    def diagonal(self):
        c, n = self.qf.shape
        blocks = lambda x: x.reshape(c // SUBLANES, SUBLANES, n)
        b3, k3, v3, q3 = blocks(self.b), blocks(self.kk), blocks(self.vb.astype(F32)), blocks(self.qf)
        sub = lax.broadcasted_iota(jnp.int32, (1, SUBLANES, 1), 1) & (HGRN_DIAG - 1)
        diag = jnp.sum(q3 * k3, axis=2, keepdims=True) * v3
        for d in range(1, HGRN_DIAG):
            sh = SUBLANES - d if self.rev else d
            valid = (sub + d <= HGRN_DIAG - 1) if self.rev else (sub >= d)
            e = jnp.exp2(jnp.where(valid, b3 - pltpu.roll(b3, sh, 1), 0.0))
            a = jnp.sum(q3 * e * pltpu.roll(k3, sh, 1), axis=2, keepdims=True)
            diag = diag + jnp.where(valid, a, 0.0) * pltpu.roll(v3, sh, 1)
        self.diag = diag.reshape(c, n)

    def intra(self):
        att = functools.reduce(lambda x, y: x + y, [jnp.where(self.masks[w], a, 0.0) for w, a in self.levels])
        self.intra_out = jnp.dot(att.astype(BF16), self.vb, preferred_element_type=F32)
        self.refs[6][...] = self.st * jnp.exp2(self.bl) + self.st_add

    def finish(self):
        self.refs[4][self.refs[5]] = self.inter + self.intra_out + self.diag


def _hgrn_kernel(qf_ref, bf_ref, kf_ref, vf_ref, qb_ref, bb_ref, kb_ref, vb_ref, of_ref, ob_ref, st_sc,
                 *, bsz, heads):
    s = pl.program_id(0)

    @pl.when(s == 0)
    def _():
        st_sc[...] = jnp.zeros(st_sc.shape, F32)

    c = qf_ref.shape[1]
    fwd, bwd = _hgrn_masks(c, False), _hgrn_masks(c, True)
    chains = []
    for bi in range(bsz):
        for h in range(heads):
            sl = slice(h * HGRN_HEAD, (h + 1) * HGRN_HEAD)
            idx = (bi, slice(None), sl)
            chains.append(_Chain(qf_ref, bf_ref, kf_ref, vf_ref, of_ref, idx, st_sc.at[bi, 0, h], False, fwd))
            chains.append(_Chain(qb_ref, bb_ref, kb_ref, vb_ref, ob_ref, idx, st_sc.at[bi, 1, h], True, bwd))
    for g0 in range(0, len(chains), HGRN_INTERLEAVE):
        group = chains[g0:g0 + HGRN_INTERLEAVE]
        for stage in (_Chain.gates, _Chain.products, _Chain.intra, _Chain.diagonal, _Chain.finish):
            for ch in group:
                stage(ch)


def _hgrn(qh, b_f, k_f, b_b, k_b, iv, lc):
    bsz, ltot, d_hgrn = qh.shape
    heads = d_hgrn // HGRN_HEAD
    c = HGRN_CHUNK
    assert lc % c == 0 and ltot % c == 0
    ncc, ntot = lc // c, ltot // c

    def bwd(s):
        return jnp.where(s < ncc, ncc - 1 - s, ntot - 1 + ncc - s)

    fspec = pl.BlockSpec((bsz, c, d_hgrn), lambda s: (0, s, 0))
    bspec = pl.BlockSpec((bsz, c, d_hgrn), lambda s: (0, bwd(s), 0))
    kern = functools.partial(_hgrn_kernel, bsz=bsz, heads=heads)
    return pl.pallas_call(
        kern,
        grid=(ntot,),
        in_specs=[fspec, fspec, fspec, fspec, bspec, bspec, bspec, bspec],
        out_specs=[fspec, bspec],
        out_shape=[jax.ShapeDtypeStruct((bsz, ltot, d_hgrn), F32)] * 2,
        scratch_shapes=[pltpu.VMEM((bsz, 2, heads, HGRN_HEAD, HGRN_HEAD), F32)],
        name="hgrn_scan",
        compiler_params=_cparams(("arbitrary",)),
    )(qh, b_f, k_f, iv, qh, b_b, k_b, iv)


def _outproj_odd_kernel(x_ref, glu_ref, gp_ref, gn_ref, scg_ref, cw_ref, cb_ref, lng_ref, lnb_ref,
                        of_ref, ob_ref, hn_ref, sog_ref, wout_ref, gpost_ref, mod_ref, o_ref, z_sc, zs_sc,
                        *, tm, ncb, lc, ll, d_conv):
    i = pl.program_id(1)
    _, _, has_prev, has_next = _seq_position(i, tm, ncb, lc, ll)
    h = CONV_HALO
    z_sc[0:h, :] = jnp.where(has_prev, gp_ref[0], 0.0)
    z_sc[h:h + tm, :] = glu_ref[0]
    z_sc[h + tm:, :] = jnp.where(has_next, gn_ref[0], 0.0)
    cw = cw_ref[...]
    base = h - CONV_WIDTH // 2
    span = zs_sc.shape[1]
    for r in range(SUBLANES):
        zs_sc[r] = z_sc[r:r + span, :]
    acc = jnp.zeros((tm, d_conv), F32)
    for j in range(CONV_WIDTH):
        r = (base + j) % SUBLANES
        a0 = base + j - r
        acc = acc + cw[j:j + 1, :] * zs_sc[r, a0:a0 + tm, :]
    zc = acc + cb_ref[...]
    mu = jnp.mean(zc, axis=-1, keepdims=True)
    zc = zc - mu
    var = jnp.mean(zc * zc, axis=-1, keepdims=True)
    zn = zc * lax.rsqrt(var + EPS) * lng_ref[...] + lnb_ref[...]
    yc = _silu(zn) * scg_ref[0].astype(F32)
    o = of_ref[0] + ob_ref[0]
    hn = hn_ref[...]
    parts = []
    for hd in range(o.shape[1] // HGRN_HEAD):
        sl = slice(hd * HGRN_HEAD, (hd + 1) * HGRN_HEAD)
        oh = o[:, sl]
        ms = jnp.mean(oh * oh, axis=-1, keepdims=True)
        parts.append(oh * lax.rsqrt(ms + EPS) * hn[:, sl])
    yd = jnp.concatenate(parts, axis=1) * sog_ref[0].astype(F32)
    y = (jnp.dot(yc.astype(BF16), wout_ref[0:d_conv, :], preferred_element_type=F32)
         + jnp.dot(yd.astype(BF16), wout_ref[d_conv:, :], preferred_element_type=F32))
    _postnorm_residual(x_ref, y, gpost_ref, mod_ref, o_ref)


def _outproj_odd(xc, glu, scg, cw, cb, lng, lnb, o_f, o_b, hnorm, sog, wout, gpost, mod, ncb, lc):
    bsz, ltot, d = xc.shape
    tm = ROW_TILE
    d_conv = glu.shape[-1]
    d_hgrn = o_f.shape[-1]
    rows, full2, modspec = _row_specs(tm, ncb)
    prev, nxt = _halo_specs(tm, CONV_HALO, d_conv, ltot)
    conv_span = (CONV_HALO + CONV_WIDTH // 2) // SUBLANES * SUBLANES
    assert SUBLANES - 1 + tm + conv_span <= tm + 2 * CONV_HALO
    kern = functools.partial(_outproj_odd_kernel, tm=tm, ncb=ncb, lc=lc, ll=ltot - lc, d_conv=d_conv)
    return pl.pallas_call(
        kern,
        grid=(bsz, ltot // tm),
        in_specs=[rows(d), rows(d_conv), prev, nxt, rows(d_conv), full2(*cw.shape), full2(1, d_conv),
                  full2(1, d_conv), full2(1, d_conv), rows(d_hgrn), rows(d_hgrn), full2(1, d_hgrn),
                  rows(d_hgrn), full2(*wout.shape), full2(1, d), modspec(d)],
        out_specs=rows(d),
        out_shape=jax.ShapeDtypeStruct(xc.shape, F32),
        scratch_shapes=[pltpu.VMEM((tm + 2 * CONV_HALO, d_conv), F32),
                        pltpu.VMEM((SUBLANES, tm + conv_span, d_conv), F32)],
        name="outproj_odd",
        compiler_params=_cparams(("parallel", "parallel")),
    )(xc, glu, glu, glu, scg, cw, cb.reshape(1, d_conv), lng.reshape(1, d_conv), lnb.reshape(1, d_conv),
      o_f, o_b, hnorm.reshape(1, d_hgrn), sog, wout, gpost.reshape(1, d), mod)


def _rope_tables(lc, ll):
    t = jnp.arange(ll)
    inv = ROPE_BASE ** (-jnp.arange(ROPE_FREQS, dtype=F32) / ROPE_FREQS)
    ang = jnp.stack([t // GRID_W, t % GRID_W], axis=-1).astype(F32)[:, :, None] * inv
    cos = jnp.cos(ang)
    sin = jnp.sin(ang)
    cos64 = jnp.stack([cos, cos], axis=2).reshape(ll, DIFF_HEAD)
    sin64 = jnp.stack([-sin, sin], axis=2).reshape(ll, DIFF_HEAD)
    cos_t = jnp.concatenate([jnp.ones((lc, DIFF_HEAD), F32), cos64], axis=0)
    sin_t = jnp.concatenate([jnp.zeros((lc, DIFF_HEAD), F32), sin64], axis=0)
    return jnp.tile(cos_t, (1, LANES // DIFF_HEAD)), jnp.tile(sin_t, (1, LANES // DIFF_HEAD))


def _block_diag(w):
    g, a, b = w.shape
    out = jnp.zeros((g * a, g * b), w.dtype)
    for j in range(g):
        out = out.at[j * a:(j + 1) * a, j * b:(j + 1) * b].set(w[j])
    return out


def kernel(x, c, ctx, c_ctx, ada_w, ada_b, norm_pre, norm_post, w_in_even, w_out_even, pool_w, pool_scale,
           diff_lambda, diff_subln, w_in_odd, w_out_odd, conv_w, conv_b, conv_ln_g, conv_ln_b, hgrn_norm,
           hgrn_lb):
    bsz, ll, d = x.shape
    lc = ctx.shape[1]
    depth = ada_w.shape[0]
    d_pool = pool_scale.shape[-1]
    d_diff = w_out_even.shape[1] - d_pool
    d_conv = conv_b.shape[-1]
    d_hgrn = hgrn_norm.shape[-1]
    assert lc % ROW_TILE == 0 and ll % ROW_TILE == 0 and bsz + 1 <= SUBLANES
    ncb = lc // ROW_TILE

    cvec = jnp.zeros((SUBLANES, d), F32).at[:bsz].set(c).at[bsz].set(c_ctx)
    ada = _ada_call(cvec, ada_w, ada_b).reshape(depth, SUBLANES, 3, d)
    xc = jnp.concatenate([ctx, x], axis=1)
    cos, sin = _rope_tables(lc, ll)

    for l in range(depth):
        mod = jnp.stack([jnp.broadcast_to(ada[l, bsz], (bsz, 3, d)), ada[l, :bsz]], axis=1)
        j = l // 2
        if l % 2 == 0:
            lam_init = 0.8 - 0.6 * math.exp(-0.3 * l)
            wb = w_in_even[j].astype(BF16)
            q0, k0, v0, g0 = 2 * d_pool, 2 * d_pool + d_diff, 2 * d_pool + 2 * d_diff, 2 * d_pool + 3 * d_diff
            w_rows = jnp.concatenate([wb[:, :q0], wb[:, k0:v0], wb[:, g0:]], axis=1)
            w_cols_t = jnp.concatenate([wb[:, q0:k0], wb[:, v0:g0]], axis=1).T
            u, sga, k, sgb, qt, vt = _inproj_even(xc, norm_pre[l], mod, w_rows, w_cols_t, cos, sin, ncb,
                                                  d_pool, d_diff)
            yb = _attention(qt, k, vt, sgb, diff_lambda[j], diff_subln[j], lc, lam_init)
            xc = _outproj_even(xc, u, sga, yb, _block_diag(pool_w[j]).astype(BF16), pool_scale[j],
                               w_out_even[j].astype(BF16), norm_post[l], mod, ncb, lc)
        else:
            glu, scg, qh, b_f, k_f, b_b, k_b, iv, sog = _inproj_odd(xc, norm_pre[l], mod, w_in_odd[j].astype(BF16),
                                                                    hgrn_lb, l, ncb, d_conv, d_hgrn)
            o_f, o_b = _hgrn(qh, b_f, k_f, b_b, k_b, iv, lc)
            xc = _outproj_odd(xc, glu, scg, conv_w[j], conv_b[j], conv_ln_g[j], conv_ln_b[j], o_f, o_b,
                              hgrn_norm[j], sog, w_out_odd[j].astype(BF16), norm_post[l], mod, ncb, lc)
    return xc[:, lc:]
```

```python
import functools
import math

import numpy as np
import jax
import jax.numpy as jnp
from jax import lax
from jax.experimental import pallas as pl
from jax.experimental.pallas import tpu as pltpu

F32 = jnp.float32
BF16 = jnp.bfloat16
EPS = 1e-6

GRID_W = 64
POOL_WINDOWS = (2, 4, 8, 16)
POOL_GROUP = 64
DIFF_HEAD = 64
DIFF_V = 2 * DIFF_HEAD
DIFF_SCALE = DIFF_HEAD ** -0.5
ROPE_BASE = 10000.0
ROPE_FREQS = DIFF_HEAD // 4
CONV_WIDTH = 31
CONV_HALO = 16
POOL_HALO = 8
HGRN_HEAD = 128
LANES = 128
SUBLANES = 8

ROW_TILE = 256
Q_TILE = 128
Q_SUBTILES = 2
SCORE_SLOTS = 4
ONES_ROWS = 16
LOG2E = 1.4426950408889634
KV_TILE = 512
HGRN_CHUNK = 64
HGRN_DIAG = 2
HGRN_INTERLEAVE = 16
VMEM_LIMIT = 48 * 1024 * 1024


def _cparams(sem):
    return pltpu.CompilerParams(dimension_semantics=sem, vmem_limit_bytes=VMEM_LIMIT)


def _silu(x):
    return x * jax.nn.sigmoid(x)


def _bdot(a, b):
    return jnp.dot(a.astype(BF16), b.astype(BF16), preferred_element_type=F32)


def _ada_kernel(c_ref, w_ref, b_ref, o_ref):
    c = c_ref[...]
    o_ref[0] = jnp.dot(_silu(c), w_ref[0], preferred_element_type=F32,
                       precision=lax.Precision.HIGHEST) + b_ref[0]


def _ada_call(cvec, ada_w, ada_b):
    depth, d, d3 = ada_w.shape
    nj = d3 // d
    return pl.pallas_call(
        _ada_kernel,
        grid=(depth, nj),
        in_specs=[pl.BlockSpec((SUBLANES, d), lambda l, j: (0, 0)),
                  pl.BlockSpec((1, d, d), lambda l, j: (l, 0, j)),
                  pl.BlockSpec((1, 1, d), lambda l, j: (l, 0, j))],
        out_specs=pl.BlockSpec((1, SUBLANES, d), lambda l, j: (l, 0, j)),
        out_shape=jax.ShapeDtypeStruct((depth, SUBLANES, d3), F32),
        name="ada_params",
        compiler_params=_cparams(("parallel", "parallel")),
    )(cvec, ada_w, ada_b.reshape(depth, 1, d3))


def _prenorm(x_ref, g_ref, mod_ref):
    x = x_ref[0]
    mod = mod_ref[0, 0]
    ms = jnp.mean(x * x, axis=-1, keepdims=True)
    h = x * lax.rsqrt(ms + EPS) * g_ref[...] * (1.0 + mod[1:2]) + mod[0:1]
    return h.astype(BF16)


def _pipelined(stages):
    nxt = stages[0][0]()
    for g, (_, consume) in enumerate(stages):
        cur = nxt
        if g + 1 < len(stages):
            nxt = stages[g + 1][0]()
        consume(cur)


def _postnorm_residual(x_ref, y, gpost_ref, mod_ref, o_ref):
    mod = mod_ref[0, 0]
    ms = jnp.mean(y * y, axis=-1, keepdims=True)
    yn = y * lax.rsqrt(ms + EPS) * gpost_ref[...]
    o_ref[0] = x_ref[0] + mod[2:3] * yn


def _row_specs(tm, ncb, first=0):
    def rows(width):
        return pl.BlockSpec((1, tm, width), lambda b, i: (b, i + first, 0))

    def full2(a, bdim):
        return pl.BlockSpec((a, bdim), lambda b, i: (0, 0))

    def mod(d):
        return pl.BlockSpec((1, 1, 3, d), lambda b, i: (b, jnp.where(i + first < ncb, 0, 1), 0, 0))

    return rows, full2, mod


def _inproj_even_kernel(x_ref, g_ref, mod_ref, w_ref, wt_ref, cos_ref, sin_ref, cost_ref, sint_ref,
                        u_ref, sga_ref, k_ref, sgb_ref, qt_ref, vt_ref, *, d_pool, d_diff):
    hb = _prenorm(x_ref, g_ref, mod_ref)
    wide = 2 * LANES

    def proj(c0, width):
        return lambda: jnp.dot(hb, w_ref[:, c0:c0 + width], preferred_element_type=F32)

    def proj_t(r0, height):
        return lambda: lax.dot_general(wt_ref[r0:r0 + height, :], hb, (((1,), (1,)), ((), ())),
                                       preferred_element_type=F32)

    def pool(r):
        u_ref[0] = r[:, :d_pool]
        sga_ref[0] = _silu(r[:, d_pool:])

    lane = lax.broadcasted_iota(jnp.int32, cos_ref.shape, 1)
    first_half = (lane & ROPE_FREQS) == 0

    def keys(c0):
        def put(r):
            for s in range(wide // LANES):
                t = r[:, s * LANES:(s + 1) * LANES]
                partner = jnp.where(first_half, pltpu.roll(t, LANES - ROPE_FREQS, 1), pltpu.roll(t, ROPE_FREQS, 1))
                k_ref[0, :, c0 + s * LANES:c0 + (s + 1) * LANES] = (t * cos_ref[...] + partner * sin_ref[...]).astype(BF16)
        return put

    def gate(c0):
        def put(r):
            sgb_ref[0, :, c0:c0 + wide] = _silu(r).astype(BF16)
        return put

    def queries(r0):
        f = ROPE_FREQS

        def put(r):
            for s in range(wide // LANES):
                t = r[s * LANES:(s + 1) * LANES]
                partner = jnp.concatenate([t[(blk ^ 1) * f:((blk ^ 1) + 1) * f] for blk in range(LANES // f)], axis=0)
                qt_ref[0, r0 + s * LANES:r0 + (s + 1) * LANES, :] = (
                    (t * cost_ref[...] + partner * sint_ref[...]) * (DIFF_SCALE * LOG2E)).astype(BF16)
        return put

    def values(r0):
        def put(r):
            vt_ref[0, r0:r0 + wide, :] = r.astype(BF16)
        return put

    k0 = 2 * d_pool
    stages = [(proj(0, 2 * d_pool), pool)]
    stages += [(proj(k0 + c, wide), keys(c)) for c in range(0, d_diff, wide)]
    stages += [(proj(k0 + d_diff + c, wide), gate(c)) for c in range(0, d_diff, wide)]
    stages += [(proj_t(r, wide), queries(r)) for r in range(0, d_diff, wide)]
    stages += [(proj_t(d_diff + r, wide), values(r)) for r in range(0, d_diff, wide)]
    _pipelined(stages)


def _inproj_even(xc, g, mod, w, wt, cos, sin, ncb, d_pool, d_diff):
    bsz, ltot, d = xc.shape
    tm = ROW_TILE
    rows, full2, modspec = _row_specs(tm, ncb)
    tab = pl.BlockSpec((tm, LANES), lambda b, i: (i, 0))
    tab_t = pl.BlockSpec((LANES, tm), lambda b, i: (0, i))
    cols_t = pl.BlockSpec((1, d_diff, tm), lambda b, i: (b, 0, i))
    shp = lambda width, dt: jax.ShapeDtypeStruct((bsz, ltot, width), dt)
    shp_t = jax.ShapeDtypeStruct((bsz, d_diff, ltot), BF16)
    return pl.pallas_call(
        functools.partial(_inproj_even_kernel, d_pool=d_pool, d_diff=d_diff),
        grid=(bsz, ltot // tm),
        in_specs=[rows(d), full2(1, d), modspec(d), full2(*w.shape), full2(*wt.shape), tab, tab, tab_t, tab_t],
        out_specs=[rows(d_pool), rows(d_pool), rows(d_diff), rows(d_diff), cols_t, cols_t],
        out_shape=[shp(d_pool, F32), shp(d_pool, F32), shp(d_diff, BF16), shp(d_diff, BF16), shp_t, shp_t],
        name="inproj_even",
        compiler_params=_cparams(("parallel", "parallel")),
    )(xc, g.reshape(1, d), mod, w, wt, cos, sin, cos.T, sin.T)


def _attn_kernel(qt_ref, k_ref, vt_ref, sgb_ref, lam_ref, sub_ref, o_ref, acc_sc, st_sc, mx_sc,
                 *, nsub, nq, n_ctx_q, ctx_chunks, lat_chunks, lam_init):
    i = pl.program_id(2)
    row = lax.broadcasted_iota(jnp.int32, (DIFF_V, nq), 0)
    ws = []
    for c in range(nsub):
        qt = qt_ref[0, :, c * nq:(c + 1) * nq]
        zero = jnp.zeros_like(qt)
        ws.append(jnp.concatenate([jnp.where(row < DIFF_HEAD, qt, zero), jnp.where(row >= DIFF_HEAD, qt, zero)],
                                  axis=1))

    ones = jnp.ones((ONES_ROWS, st_sc.shape[2]), BF16)

    def scores(chunk, slot):
        off, size = chunk
        kc = k_ref[0, off:off + size, :]
        for c in range(nsub):
            st = jnp.dot(kc, ws[c], preferred_element_type=F32)
            st_sc[slot, c, 0:size] = st
            mx_sc[slot, c] = jnp.max(st, axis=0, keepdims=True)

    def finish(pending):
        for c, (alpha, pv) in enumerate(pending):
            acc_sc[c] = pv if alpha is None else alpha * acc_sc[c] + pv

    def absorb(chunk, slot, ms, pending=None, ahead=None):
        off, size = chunk
        new = [mx_sc[slot, c] if ms is None else jnp.maximum(ms[c], mx_sc[slot, c]) for c in range(nsub)]
        alphas = [None if ms is None else jnp.exp2(ms[c] - new[c]) for c in range(nsub)]
        if ahead is not None:
            ahead()
        if pending is not None:
            finish(pending)
        vte = jnp.concatenate([vt_ref[0, :, off:off + size], ones[:, :size]], axis=0)
        pvs = [jnp.dot(vte, jnp.exp2(st_sc[slot, c, 0:size] - new[c]).astype(BF16), preferred_element_type=F32)
               for c in range(nsub)]
        return tuple(new), tuple(zip(alphas, pvs))

    def sweep(chunks):
        slots = SCORE_SLOTS
        for n in range(min(2, len(chunks))):
            scores(chunks[n], n)
        ms, pending = None, None
        for n, chunk in enumerate(chunks):
            ahead = (lambda c=chunks[n + 2], s=(n + 2) % slots: scores(c, s)) if n + 2 < len(chunks) else None
            ms, pending = absorb(chunk, n % slots, ms, pending, ahead)
        finish(pending)

    @pl.when(i < n_ctx_q)
    def _():
        sweep(ctx_chunks)

    @pl.when(i >= n_ctx_q)
    def _():
        sweep(ctx_chunks + lat_chunks)

    lp = lam_ref[...]
    lam = (jnp.exp(jnp.sum(lp[0:1] * lp[1:2], axis=1, keepdims=True))
           - jnp.exp(jnp.sum(lp[2:3] * lp[3:4], axis=1, keepdims=True)) + lam_init)
    accs = [acc_sc[c] for c in range(nsub)]
    o_alls = [acc[:DIFF_V] / acc[DIFF_V:DIFF_V + 1] for acc in accs]
    os = [(o_all[:, :nq] - lam * o_all[:, nq:]).T for o_all in o_alls]
    mss = [jnp.mean(o * o, axis=-1, keepdims=True) for o in os]
    gain = sub_ref[...] * (1.0 - lam_init)
    for c in range(nsub):
        on = os[c] * lax.rsqrt(mss[c] + EPS) * gain
        o_ref[0, c * nq:(c + 1) * nq, :] = (on * sgb_ref[0, c * nq:(c + 1) * nq, :].astype(F32)).astype(BF16)


def _attention(qt, k, vt, sgb, lam_p, subln, lc, lam_init):
    bsz, ltot, d_diff = k.shape
    heads = d_diff // DIFF_V
    nq, nsub, tk = Q_TILE, Q_SUBTILES, KV_TILE
    nqs = nq * nsub
    assert lc % nqs == 0 and ltot % nqs == 0 and lc % LANES == 0 and (ltot - lc) % tk == 0
    ctx_chunks = tuple((o, min(tk, lc - o)) for o in range(0, lc, tk))
    lat_chunks = tuple((o, tk) for o in range(lc, ltot, tk))
    rowspec = pl.BlockSpec((1, nqs, DIFF_V), lambda b, h, i: (b, i, h))
    kern = functools.partial(_attn_kernel, nsub=nsub, nq=nq, n_ctx_q=lc // nqs, ctx_chunks=ctx_chunks,
                             lat_chunks=lat_chunks, lam_init=lam_init)
    return pl.pallas_call(
        kern,
        grid=(bsz, heads, ltot // nqs),
        in_specs=[pl.BlockSpec((1, DIFF_V, nqs), lambda b, h, i: (b, h, i)),
                  pl.BlockSpec((1, ltot, DIFF_V), lambda b, h, i: (b, 0, h)),
                  pl.BlockSpec((1, DIFF_V, ltot), lambda b, h, i: (b, h, 0)),
                  rowspec,
                  pl.BlockSpec(lam_p.shape, lambda b, h, i: (0, 0)),
                  pl.BlockSpec((1, DIFF_V), lambda b, h, i: (0, 0))],
        out_specs=rowspec,
        out_shape=jax.ShapeDtypeStruct((bsz, ltot, d_diff), BF16),
        scratch_shapes=[pltpu.VMEM((nsub, DIFF_V + ONES_ROWS, 2 * nq), F32),
                        pltpu.VMEM((SCORE_SLOTS, nsub, tk, 2 * nq), F32),
                        pltpu.VMEM((SCORE_SLOTS, nsub, 1, 2 * nq), F32)],
        name="diff_attention",
        compiler_params=_cparams(("parallel", "parallel", "arbitrary")),
    )(qt, k, vt, sgb, lam_p, subln.reshape(1, DIFF_V))


def _seq_position(i, tm, ncb, lc, ll):
    nblk = ncb + ll // tm
    is_ctx = i < ncb
    t0 = jnp.where(is_ctx, i, i - ncb) * tm
    lseq = jnp.where(is_ctx, lc, ll)
    has_prev = jnp.logical_and(i != 0, i != ncb)
    has_next = jnp.logical_and(i != ncb - 1, i != nblk - 1)
    return t0, lseq, has_prev, has_next


def _outproj_even_kernel(x_ref, u_ref, up_ref, un_ref, sga_ref, yb_ref, wpool_ref, pscale_ref, wout_ref,
                         gpost_ref, mod_ref, o_ref, *, tm, ncb, lc, ll, d_pool):
    i = pl.program_id(1)
    t0, lseq, has_prev, has_next = _seq_position(i, tm, ncb, lc, ll)
    y_attn = jnp.dot(yb_ref[0], wout_ref[d_pool:, :], preferred_element_type=F32)
    u = u_ref[0]
    z = jnp.concatenate([jnp.where(has_prev, up_ref[0], 0.0), u, jnp.where(has_next, un_ref[0], 0.0)], axis=0)
    p2 = z[:-1] + z[1:]
    p4 = p2[:-2] + p2[2:]
    p8 = p4[:-4] + p4[4:]
    p16 = p8[:-8] + p8[8:]
    sums = (p2[7:7 + tm], p4[6:6 + tm], p8[4:4 + tm], p16[0:tm])
    lane = lax.broadcasted_iota(jnp.int32, u.shape, 1)
    grp = lane >> 6
    s = jnp.where(grp == 0, sums[0], jnp.where(grp == 1, sums[1], jnp.where(grp == 2, sums[2], sums[3])))
    half = jnp.left_shift(1, grp)
    t = lax.broadcasted_iota(jnp.int32, u.shape, 0) + t0
    cnt = jnp.minimum(t + half - 1, lseq - 1) + 1 - jnp.maximum(t - half, 0)
    dpool = s / cnt.astype(F32) - u
    ya = jnp.dot(dpool.astype(BF16), wpool_ref[...], preferred_element_type=F32) * pscale_ref[...] * sga_ref[0]
    y = jnp.dot(ya.astype(BF16), wout_ref[0:d_pool, :], preferred_element_type=F32) + y_attn
    _postnorm_residual(x_ref, y, gpost_ref, mod_ref, o_ref)


def _halo_specs(tm, halo, width, ltot, first=0):
    r = tm // halo
    prev = pl.BlockSpec((1, halo, width), lambda b, i: (b, jnp.maximum((i + first) * r - 1, 0), 0))
    nxt = pl.BlockSpec((1, halo, width),
                       lambda b, i: (b, jnp.minimum((i + first + 1) * r, ltot // halo - 1), 0))
    return prev, nxt


def _outproj_even(xc, u, sga, yb, wpool, pscale, wout, gpost, mod, ncb, lc):
    bsz, ltot, d = xc.shape
    tm = ROW_TILE
    d_pool = u.shape[-1]
    rows, full2, modspec = _row_specs(tm, ncb)
    prev, nxt = _halo_specs(tm, POOL_HALO, d_pool, ltot)
    kern = functools.partial(_outproj_even_kernel, tm=tm, ncb=ncb, lc=lc, ll=ltot - lc, d_pool=d_pool)
    return pl.pallas_call(
        kern,
        grid=(bsz, ltot // tm),
        in_specs=[rows(d), rows(d_pool), prev, nxt, rows(d_pool), rows(yb.shape[-1]),
                  full2(*wpool.shape), full2(1, d_pool), full2(*wout.shape), full2(1, d), modspec(d)],
        out_specs=rows(d),
        out_shape=jax.ShapeDtypeStruct(xc.shape, F32),
        name="outproj_even",
        compiler_params=_cparams(("parallel", "parallel")),
    )(xc, u, u, u, sga, yb, wpool, pscale.reshape(1, d_pool), wout, gpost.reshape(1, d), mod)


def _hgrn_lower_bounds(raw, layer):
    rows = [raw[:, j, :] for j in range(raw.shape[1])]
    mx = functools.reduce(jnp.maximum, rows)
    ex = [jnp.exp(r - mx) for r in rows]
    den = functools.reduce(lambda a, bb: a + bb, ex)
    if layer == 0:
        return jnp.zeros_like(mx)
    return functools.reduce(lambda a, bb: a + bb, [e / den for e in ex[1:layer + 1]])


def _chunk_tri(n, c, rev):
    ri = lax.broadcasted_iota(jnp.int32, (n, n), 0)
    ci = lax.broadcasted_iota(jnp.int32, (n, n), 1)
    shift = int(math.log2(c))
    keep = jnp.logical_and((ri >> shift) == (ci >> shift), (ci >= ri) if rev else (ci <= ri))
    return jnp.where(keep, 1.0, 0.0).astype(BF16)


def _cumsum_rows(x, tri):
    hi = x.astype(BF16)
    r1 = x - hi.astype(F32)
    mid = r1.astype(BF16)
    lo = (r1 - mid.astype(F32)).astype(BF16)
    n = x.shape[1]
    r = jnp.dot(tri, jnp.concatenate([hi, mid, lo], axis=1), preferred_element_type=F32)
    return r[:, :n] + r[:, n:2 * n] + r[:, 2 * n:]


def _inproj_odd_kernel(x_ref, g_ref, mod_ref, w_ref, lbraw_ref, glu_ref, scg_ref, qh_ref, bf_ref, kf_ref, bb_ref,
                       kb_ref, iv_ref, sog_ref, *, d_conv, d_hgrn, layer):
    hb = _prenorm(x_ref, g_ref, mod_ref)
    tm = hb.shape[0]
    lbs = _hgrn_lower_bounds(lbraw_ref[...], layer)

    def glu(r):
        glu_ref[0] = r[:, :d_conv] * jax.nn.sigmoid(r[:, d_conv:])

    def silu_to(ref):
        def put(r):
            ref[0] = _silu(r).astype(BF16)
        return put

    def gate(dirn, b_ref, k_ref):
        def put(r):
            lb = lbs[dirn:dirn + 1]
            f = lb + (1.0 - lb) * jax.nn.sigmoid(r)
            k_ref[0] = (1.0 - f).astype(BF16)
            b_ref[0] = _cumsum_rows(jnp.log2(f), _chunk_tri(tm, HGRN_CHUNK, bool(dirn)))
        return put

    def value(r):
        iv_ref[0] = r.astype(BF16)

    def proj(c0, width):
        return lambda: jnp.dot(hb, w_ref[:, c0:c0 + width], preferred_element_type=F32)

    c0 = 3 * d_conv
    _pipelined([(proj(0, 2 * d_conv), glu), (proj(2 * d_conv, d_conv), silu_to(scg_ref)),
                (proj(c0, d_hgrn), silu_to(qh_ref)), (proj(c0 + d_hgrn, d_hgrn), gate(0, bf_ref, kf_ref)),
                (proj(c0 + 2 * d_hgrn, d_hgrn), gate(1, bb_ref, kb_ref)),
                (proj(c0 + 3 * d_hgrn, d_hgrn), value), (proj(c0 + 4 * d_hgrn, d_hgrn), silu_to(sog_ref))])


def _inproj_odd(xc, g, mod, w, hgrn_lb, layer, ncb, d_conv, d_hgrn):
    bsz, ltot, d = xc.shape
    tm = ROW_TILE
    assert tm % HGRN_CHUNK == 0
    rows, full2, modspec = _row_specs(tm, ncb)
    shp = lambda width, dt: jax.ShapeDtypeStruct((bsz, ltot, width), dt)
    return pl.pallas_call(
        functools.partial(_inproj_odd_kernel, d_conv=d_conv, d_hgrn=d_hgrn, layer=layer),
        grid=(bsz, ltot // tm),
        in_specs=[rows(d), full2(1, d), modspec(d), full2(*w.shape),
                  pl.BlockSpec(hgrn_lb.shape, lambda b, i: (0, 0, 0))],
        out_specs=[rows(d_conv), rows(d_conv), rows(d_hgrn), rows(d_hgrn), rows(d_hgrn), rows(d_hgrn),
                   rows(d_hgrn), rows(d_hgrn), rows(d_hgrn)],
        out_shape=[shp(d_conv, F32), shp(d_conv, BF16), shp(d_hgrn, BF16), shp(d_hgrn, F32), shp(d_hgrn, BF16),
                   shp(d_hgrn, F32), shp(d_hgrn, BF16), shp(d_hgrn, BF16), shp(d_hgrn, BF16)],
        name="inproj_odd",
        compiler_params=_cparams(("parallel", "parallel")),
    )(xc, g.reshape(1, d), mod, w, hgrn_lb)


def _hgrn_masks(c, rev):
    ri = lax.broadcasted_iota(jnp.int32, (c, c), 0)
    ci = lax.broadcasted_iota(jnp.int32, (c, c), 1)
    masks = {}
    w = c // 2
    while w >= HGRN_DIAG:
        shift = int(math.log2(2 * w))
        same_pair = (ri >> shift) == (ci >> shift)
        r_odd, c_odd = (ri & w) != 0, (ci & w) != 0
        owns = jnp.logical_and(c_odd, jnp.logical_not(r_odd)) if rev else jnp.logical_and(r_odd, jnp.logical_not(c_odd))
        masks[w] = jnp.logical_and(same_pair, owns)
        w //= 2
    return masks


class _Chain:
    def __init__(self, q_ref, b_ref, k_ref, v_ref, o_ref, idx, st_ref, rev, masks):
        self.refs = (q_ref, b_ref, k_ref, v_ref, o_ref, idx, st_ref)
        self.rev, self.masks = rev, masks

    def gates(self):
        q_ref, b_ref, k_ref, v_ref, _, idx, _ = self.refs
        self.qf = q_ref[idx].astype(F32)
        self.vb = v_ref[idx]
        self.kk = k_ref[idx].astype(F32)
        self.b = b_ref[idx]

    def products(self):
        qf, kk, b, rev = self.qf, self.kk, self.b, self.rev
        c, n = qf.shape
        self.st = self.refs[6][...]
        self.inter = lax.dot_general((qf * jnp.exp2(b)).astype(BF16), self.st.astype(BF16),
                                     (((1,), (1,)), ((), ())), preferred_element_type=F32)
        self.levels = []
        b3 = b.reshape(c // SUBLANES, SUBLANES, n)
        sub = lax.broadcasted_iota(jnp.int32, (1, SUBLANES, 1), 1)
        w = c // 2
        while w >= HGRN_DIAG:
            if 2 * w >= SUBLANES:
                refs = []
                for p in range(c // (2 * w)):
                    r0 = p * 2 * w + (w if rev else w - 1)
                    refs.append(jnp.broadcast_to(b[r0:r0 + 1, :], (2 * w, n)))
                ref = refs[0] if len(refs) == 1 else jnp.concatenate(refs, axis=0)
            else:
                ref3 = None
                for p in range(SUBLANES // (2 * w)):
                    r0 = p * 2 * w + (w if rev else w - 1)
                    piece = jnp.broadcast_to(b3[:, r0:r0 + 1, :], b3.shape)
                    ref3 = piece if ref3 is None else jnp.where(sub >= p * 2 * w, piece, ref3)
                ref = ref3.reshape(c, n)
            qs = jnp.exp2(jnp.minimum(b - ref, 0.0)) * qf
            ks = jnp.exp2(jnp.minimum(ref - b, 0.0)) * kk
            self.levels.append((w, lax.dot_general(qs.astype(BF16), ks.astype(BF16), (((1,), (1,)), ((), ())),
                                                   preferred_element_type=F32)))
            w //= 2
        self.bl = b[0:1, :] if rev else b[c - 1:c, :]
        kd = (kk * jnp.exp2(self.bl - b)).astype(BF16)
        self.st_add = lax.dot_general(self.vb, kd, (((0,), (0,)), ((), ())), preferred_element_type=F32)

    def diagonal(self):
        c, n = self.qf.shape
        blocks = lambda x: x.reshape(c // SUBLANES, SUBLANES, n)
        b3, k3, v3, q3 = blocks(self.b), blocks(self.kk), blocks(self.vb.astype(F32)), blocks(self.qf)
        sub = lax.broadcasted_iota(jnp.int32, (1, SUBLANES, 1), 1) & (HGRN_DIAG - 1)
        diag = jnp.sum(q3 * k3, axis=2, keepdims=True) * v3
        for d in range(1, HGRN_DIAG):
            sh = SUBLANES - d if self.rev else d
            valid = (sub + d <= HGRN_DIAG - 1) if self.rev else (sub >= d)
            e = jnp.exp2(jnp.where(valid, b3 - pltpu.roll(b3, sh, 1), 0.0))
            a = jnp.sum(q3 * e * pltpu.roll(k3, sh, 1), axis=2, keepdims=True)
            diag = diag + jnp.where(valid, a, 0.0) * pltpu.roll(v3, sh, 1)
        self.diag = diag.reshape(c, n)

    def intra(self):
        att = functools.reduce(lambda x, y: x + y, [jnp.where(self.masks[w], a, 0.0) for w, a in self.levels])
        self.intra_out = jnp.dot(att.astype(BF16), self.vb, preferred_element_type=F32)
        self.refs[6][...] = self.st * jnp.exp2(self.bl) + self.st_add

    def finish(self):
        self.refs[4][self.refs[5]] = self.inter + self.intra_out + self.diag


def _hgrn_kernel(qf_ref, bf_ref, kf_ref, vf_ref, qb_ref, bb_ref, kb_ref, vb_ref, of_ref, ob_ref, st_sc,
                 *, bsz, heads):
    s = pl.program_id(0)

    @pl.when(s == 0)
    def _():
        st_sc[...] = jnp.zeros(st_sc.shape, F32)

    c = qf_ref.shape[1]
    fwd, bwd = _hgrn_masks(c, False), _hgrn_masks(c, True)
    chains = []
    for bi in range(bsz):
        for h in range(heads):
            sl = slice(h * HGRN_HEAD, (h + 1) * HGRN_HEAD)
            idx = (bi, slice(None), sl)
            chains.append(_Chain(qf_ref, bf_ref, kf_ref, vf_ref, of_ref, idx, st_sc.at[bi, 0, h], False, fwd))
            chains.append(_Chain(qb_ref, bb_ref, kb_ref, vb_ref, ob_ref, idx, st_sc.at[bi, 1, h], True, bwd))
    for g0 in range(0, len(chains), HGRN_INTERLEAVE):
        group = chains[g0:g0 + HGRN_INTERLEAVE]
        for stage in (_Chain.gates, _Chain.products, _Chain.intra, _Chain.diagonal, _Chain.finish):
            for ch in group:
                stage(ch)


def _hgrn(qh, b_f, k_f, b_b, k_b, iv, lc):
    bsz, ltot, d_hgrn = qh.shape
    heads = d_hgrn // HGRN_HEAD
    c = HGRN_CHUNK
    assert lc % c == 0 and ltot % c == 0
    ncc, ntot = lc // c, ltot // c

    def bwd(s):
        return jnp.where(s < ncc, ncc - 1 - s, ntot - 1 + ncc - s)

    fspec = pl.BlockSpec((bsz, c, d_hgrn), lambda s: (0, s, 0))
    bspec = pl.BlockSpec((bsz, c, d_hgrn), lambda s: (0, bwd(s), 0))
    kern = functools.partial(_hgrn_kernel, bsz=bsz, heads=heads)
    return pl.pallas_call(
        kern,
        grid=(ntot,),
        in_specs=[fspec, fspec, fspec, fspec, bspec, bspec, bspec, bspec],
        out_specs=[fspec, bspec],
        out_shape=[jax.ShapeDtypeStruct((bsz, ltot, d_hgrn), F32)] * 2,
        scratch_shapes=[pltpu.VMEM((bsz, 2, heads, HGRN_HEAD, HGRN_HEAD), F32)],
        name="hgrn_scan",
        compiler_params=_cparams(("arbitrary",)),
    )(qh, b_f, k_f, iv, qh, b_b, k_b, iv)


def _outproj_odd_kernel(x_ref, glu_ref, gp_ref, gn_ref, scg_ref, cw_ref, cb_ref, lng_ref, lnb_ref,
                        of_ref, ob_ref, hn_ref, sog_ref, wout_ref, gpost_ref, mod_ref, o_ref, z_sc, zs_sc,
                        *, tm, ncb, lc, ll, d_conv, first):
    i = pl.program_id(1) + first
    _, _, has_prev, has_next = _seq_position(i, tm, ncb, lc, ll)
    h = CONV_HALO
    z_sc[0:h, :] = jnp.where(has_prev, gp_ref[0], 0.0)
    z_sc[h:h + tm, :] = glu_ref[0]
    z_sc[h + tm:, :] = jnp.where(has_next, gn_ref[0], 0.0)
    cw = cw_ref[...]
    base = h - CONV_WIDTH // 2
    span = zs_sc.shape[1]
    for r in range(SUBLANES):
        zs_sc[r] = z_sc[r:r + span, :]
    acc = jnp.zeros((tm, d_conv), F32)
    for j in range(CONV_WIDTH):
        r = (base + j) % SUBLANES
        a0 = base + j - r
        acc = acc + cw[j:j + 1, :] * zs_sc[r, a0:a0 + tm, :]
    zc = acc + cb_ref[...]
    mu = jnp.mean(zc, axis=-1, keepdims=True)
    zc = zc - mu
    var = jnp.mean(zc * zc, axis=-1, keepdims=True)
    zn = zc * lax.rsqrt(var + EPS) * lng_ref[...] + lnb_ref[...]
    yc = _silu(zn) * scg_ref[0].astype(F32)
    o = of_ref[0] + ob_ref[0]
    hn = hn_ref[...]
    parts = []
    for hd in range(o.shape[1] // HGRN_HEAD):
        sl = slice(hd * HGRN_HEAD, (hd + 1) * HGRN_HEAD)
        oh = o[:, sl]
        ms = jnp.mean(oh * oh, axis=-1, keepdims=True)
        parts.append(oh * lax.rsqrt(ms + EPS) * hn[:, sl])
    yd = jnp.concatenate(parts, axis=1) * sog_ref[0].astype(F32)
    y = (jnp.dot(yc.astype(BF16), wout_ref[0:d_conv, :], preferred_element_type=F32)
         + jnp.dot(yd.astype(BF16), wout_ref[d_conv:, :], preferred_element_type=F32))
    _postnorm_residual(x_ref, y, gpost_ref, mod_ref, o_ref)


def _outproj_odd(xc, glu, scg, cw, cb, lng, lnb, o_f, o_b, hnorm, sog, wout, gpost, mod, ncb, lc, latent_only):
    bsz, ltot, d = xc.shape
    tm = ROW_TILE
    d_conv = glu.shape[-1]
    d_hgrn = o_f.shape[-1]
    first = ncb if latent_only else 0
    rows, full2, modspec = _row_specs(tm, ncb, first)
    prev, nxt = _halo_specs(tm, CONV_HALO, d_conv, ltot, first)
    conv_span = (CONV_HALO + CONV_WIDTH // 2) // SUBLANES * SUBLANES
    assert SUBLANES - 1 + tm + conv_span <= tm + 2 * CONV_HALO
    kern = functools.partial(_outproj_odd_kernel, tm=tm, ncb=ncb, lc=lc, ll=ltot - lc, d_conv=d_conv, first=first)
    return pl.pallas_call(
        kern,
        grid=(bsz, ltot // tm - first),
        in_specs=[rows(d), rows(d_conv), prev, nxt, rows(d_conv), full2(*cw.shape), full2(1, d_conv),
                  full2(1, d_conv), full2(1, d_conv), rows(d_hgrn), rows(d_hgrn), full2(1, d_hgrn),
                  rows(d_hgrn), full2(*wout.shape), full2(1, d), modspec(d)],
        out_specs=pl.BlockSpec((1, tm, d), lambda b, i: (b, i, 0)),
        out_shape=jax.ShapeDtypeStruct((bsz, ltot - first * tm, d), F32),
        scratch_shapes=[pltpu.VMEM((tm + 2 * CONV_HALO, d_conv), F32),
                        pltpu.VMEM((SUBLANES, tm + conv_span, d_conv), F32)],
        name="outproj_odd",
        compiler_params=_cparams(("parallel", "parallel")),
    )(xc, glu, glu, glu, scg, cw, cb.reshape(1, d_conv), lng.reshape(1, d_conv), lnb.reshape(1, d_conv),
      o_f, o_b, hnorm.reshape(1, d_hgrn), sog, wout, gpost.reshape(1, d), mod)


def _rope_tables(lc, ll):
    t = jnp.arange(ll)
    inv = ROPE_BASE ** (-jnp.arange(ROPE_FREQS, dtype=F32) / ROPE_FREQS)
    ang = jnp.stack([t // GRID_W, t % GRID_W], axis=-1).astype(F32)[:, :, None] * inv
    cos = jnp.cos(ang)
    sin = jnp.sin(ang)
    cos64 = jnp.stack([cos, cos], axis=2).reshape(ll, DIFF_HEAD)
    sin64 = jnp.stack([-sin, sin], axis=2).reshape(ll, DIFF_HEAD)
    cos_t = jnp.concatenate([jnp.ones((lc, DIFF_HEAD), F32), cos64], axis=0)
    sin_t = jnp.concatenate([jnp.zeros((lc, DIFF_HEAD), F32), sin64], axis=0)
    return jnp.tile(cos_t, (1, LANES // DIFF_HEAD)), jnp.tile(sin_t, (1, LANES // DIFF_HEAD))


def _block_diag(w):
    g, a, b = w.shape
    out = jnp.zeros((g * a, g * b), w.dtype)
    for j in range(g):
        out = out.at[j * a:(j + 1) * a, j * b:(j + 1) * b].set(w[j])
    return out


def kernel(x, c, ctx, c_ctx, ada_w, ada_b, norm_pre, norm_post, w_in_even, w_out_even, pool_w, pool_scale,
           diff_lambda, diff_subln, w_in_odd, w_out_odd, conv_w, conv_b, conv_ln_g, conv_ln_b, hgrn_norm,
           hgrn_lb):
    bsz, ll, d = x.shape
    lc = ctx.shape[1]
    depth = ada_w.shape[0]
    d_pool = pool_scale.shape[-1]
    d_diff = w_out_even.shape[1] - d_pool
    d_conv = conv_b.shape[-1]
    d_hgrn = hgrn_norm.shape[-1]
    assert lc % ROW_TILE == 0 and ll % ROW_TILE == 0 and bsz + 1 <= SUBLANES
    ncb = lc // ROW_TILE

    cvec = jnp.zeros((SUBLANES, d), F32).at[:bsz].set(c).at[bsz].set(c_ctx)
    ada = _ada_call(cvec, ada_w, ada_b).reshape(depth, SUBLANES, 3, d)
    xc = jnp.concatenate([ctx, x], axis=1)
    cos, sin = _rope_tables(lc, ll)

    for l in range(depth):
        mod = jnp.stack([jnp.broadcast_to(ada[l, bsz], (bsz, 3, d)), ada[l, :bsz]], axis=1)
        j = l // 2
        if l % 2 == 0:
            lam_init = 0.8 - 0.6 * math.exp(-0.3 * l)
            wb = w_in_even[j].astype(BF16)
            q0, k0, v0, g0 = 2 * d_pool, 2 * d_pool + d_diff, 2 * d_pool + 2 * d_diff, 2 * d_pool + 3 * d_diff
            w_rows = jnp.concatenate([wb[:, :q0], wb[:, k0:v0], wb[:, g0:]], axis=1)
            w_cols_t = jnp.concatenate([wb[:, q0:k0], wb[:, v0:g0]], axis=1).T
            u, sga, k, sgb, qt, vt = _inproj_even(xc, norm_pre[l], mod, w_rows, w_cols_t, cos, sin, ncb,
                                                  d_pool, d_diff)
            yb = _attention(qt, k, vt, sgb, diff_lambda[j], diff_subln[j], lc, lam_init)
            xc = _outproj_even(xc, u, sga, yb, _block_diag(pool_w[j]).astype(BF16), pool_scale[j],
                               w_out_even[j].astype(BF16), norm_post[l], mod, ncb, lc)
        else:
            glu, scg, qh, b_f, k_f, b_b, k_b, iv, sog = _inproj_odd(xc, norm_pre[l], mod, w_in_odd[j].astype(BF16),
                                                                    hgrn_lb, l, ncb, d_conv, d_hgrn)
            o_f, o_b = _hgrn(qh, b_f, k_f, b_b, k_b, iv, lc)
            xc = _outproj_odd(xc, glu, scg, conv_w[j], conv_b[j], conv_ln_g[j], conv_ln_b[j], o_f, o_b,
                              hgrn_norm[j], sog, w_out_odd[j].astype(BF16), norm_post[l], mod, ncb, lc,
                              latent_only=l == depth - 1)
    return xc if depth % 2 == 0 else xc[:, lc:]
```

```python
import functools
import math

import numpy as np
import jax
import jax.numpy as jnp
from jax import lax
from jax.experimental import pallas as pl
from jax.experimental.pallas import tpu as pltpu

F32 = jnp.float32
BF16 = jnp.bfloat16
EPS = 1e-6

GRID_W = 64
POOL_WINDOWS = (2, 4, 8, 16)
POOL_GROUP = 64
DIFF_HEAD = 64
DIFF_V = 2 * DIFF_HEAD
DIFF_SCALE = DIFF_HEAD ** -0.5
ROPE_BASE = 10000.0
ROPE_FREQS = DIFF_HEAD // 4
CONV_WIDTH = 31
CONV_HALO = 16
POOL_HALO = 8
HGRN_HEAD = 128
LANES = 128
SUBLANES = 8

ROW_TILE = 256
Q_TILE = 128
Q_SUBTILES = 2
ATTN_HEADS_PER_STEP = 1
SCORE_SLOTS = 3
ONES_ROWS = 16
LOG2E = 1.4426950408889634
KV_TILE = 512
HGRN_CHUNK = 64
HGRN_DIAG = 2
HGRN_INTERLEAVE = 16
VMEM_LIMIT = 48 * 1024 * 1024


def _cparams(sem):
    return pltpu.CompilerParams(dimension_semantics=sem, vmem_limit_bytes=VMEM_LIMIT)


def _silu(x):
    return x * jax.nn.sigmoid(x)


def _bdot(a, b):
    return jnp.dot(a.astype(BF16), b.astype(BF16), preferred_element_type=F32)


def _ada_kernel(c_ref, w_ref, b_ref, o_ref):
    c = c_ref[...]
    o_ref[0] = jnp.dot(_silu(c), w_ref[0], preferred_element_type=F32,
                       precision=lax.Precision.HIGHEST) + b_ref[0]


def _ada_call(cvec, ada_w, ada_b):
    depth, d, d3 = ada_w.shape
    nj = d3 // d
    return pl.pallas_call(
        _ada_kernel,
        grid=(depth, nj),
        in_specs=[pl.BlockSpec((SUBLANES, d), lambda l, j: (0, 0)),
                  pl.BlockSpec((1, d, d), lambda l, j: (l, 0, j)),
                  pl.BlockSpec((1, 1, d), lambda l, j: (l, 0, j))],
        out_specs=pl.BlockSpec((1, SUBLANES, d), lambda l, j: (l, 0, j)),
        out_shape=jax.ShapeDtypeStruct((depth, SUBLANES, d3), F32),
        name="ada_params",
        compiler_params=_cparams(("parallel", "parallel")),
    )(cvec, ada_w, ada_b.reshape(depth, 1, d3))


def _prenorm(x_ref, g_ref, mod_ref):
    x = x_ref[0]
    mod = mod_ref[0, 0]
    ms = jnp.mean(x * x, axis=-1, keepdims=True)
    h = x * lax.rsqrt(ms + EPS) * g_ref[...] * (1.0 + mod[1:2]) + mod[0:1]
    return h.astype(BF16)


def _pipelined(stages):
    nxt = stages[0][0]()
    for g, (_, consume) in enumerate(stages):
        cur = nxt
        if g + 1 < len(stages):
            nxt = stages[g + 1][0]()
        consume(cur)


def _postnorm_residual(x_ref, y, gpost_ref, mod_ref, o_ref):
    mod = mod_ref[0, 0]
    ms = jnp.mean(y * y, axis=-1, keepdims=True)
    yn = y * lax.rsqrt(ms + EPS) * gpost_ref[...]
    o_ref[0] = x_ref[0] + mod[2:3] * yn


def _row_specs(tm, ncb, first=0):
    def rows(width):
        return pl.BlockSpec((1, tm, width), lambda b, i: (b, i + first, 0))

    def full2(a, bdim):
        return pl.BlockSpec((a, bdim), lambda b, i: (0, 0))

    def mod(d):
        return pl.BlockSpec((1, 1, 3, d), lambda b, i: (b, jnp.where(i + first < ncb, 0, 1), 0, 0))

    return rows, full2, mod


def _inproj_even_kernel(x_ref, g_ref, mod_ref, w_ref, wt_ref, cos_ref, sin_ref, cost_ref, sint_ref,
                        u_ref, sga_ref, k_ref, qt_ref, vt_ref, sgbt_ref, *, d_pool, d_diff):
    hb = _prenorm(x_ref, g_ref, mod_ref)
    wide = 2 * LANES

    def proj(c0, width):
        return lambda: jnp.dot(hb, w_ref[:, c0:c0 + width], preferred_element_type=F32)

    def proj_t(r0, height):
        return lambda: lax.dot_general(wt_ref[r0:r0 + height, :], hb, (((1,), (1,)), ((), ())),
                                       preferred_element_type=F32)

    def pool(r):
        u_ref[0] = r[:, :d_pool]
        sga_ref[0] = _silu(r[:, d_pool:])

    lane = lax.broadcasted_iota(jnp.int32, cos_ref.shape, 1)
    first_half = (lane & ROPE_FREQS) == 0

    def keys(c0):
        def put(r):
            for s in range(wide // LANES):
                t = r[:, s * LANES:(s + 1) * LANES]
                partner = jnp.where(first_half, pltpu.roll(t, LANES - ROPE_FREQS, 1), pltpu.roll(t, ROPE_FREQS, 1))
                k_ref[0, :, c0 + s * LANES:c0 + (s + 1) * LANES] = (t * cos_ref[...] + partner * sin_ref[...]).astype(BF16)
        return put

    def gate(r0):
        def put(r):
            sgbt_ref[0, r0:r0 + wide, :] = _silu(r).astype(BF16)
        return put

    def queries(r0):
        f = ROPE_FREQS

        def put(r):
            for s in range(wide // LANES):
                t = r[s * LANES:(s + 1) * LANES]
                partner = jnp.concatenate([t[(blk ^ 1) * f:((blk ^ 1) + 1) * f] for blk in range(LANES // f)], axis=0)
                qt_ref[0, r0 + s * LANES:r0 + (s + 1) * LANES, :] = (
                    (t * cost_ref[...] + partner * sint_ref[...]) * (DIFF_SCALE * LOG2E)).astype(BF16)
        return put

    def values(r0):
        def put(r):
            vt_ref[0, r0:r0 + wide, :] = r.astype(BF16)
        return put

    k0 = 2 * d_pool
    stages = [(proj(0, 2 * d_pool), pool)]
    stages += [(proj(k0 + c, wide), keys(c)) for c in range(0, d_diff, wide)]
    stages += [(proj_t(r, wide), queries(r)) for r in range(0, d_diff, wide)]
    stages += [(proj_t(d_diff + r, wide), values(r)) for r in range(0, d_diff, wide)]
    stages += [(proj_t(2 * d_diff + r, wide), gate(r)) for r in range(0, d_diff, wide)]
    _pipelined(stages)


def _inproj_even(xc, g, mod, w, wt, cos, sin, ncb, d_pool, d_diff):
    bsz, ltot, d = xc.shape
    tm = ROW_TILE
    rows, full2, modspec = _row_specs(tm, ncb)
    tab = pl.BlockSpec((tm, LANES), lambda b, i: (i, 0))
    tab_t = pl.BlockSpec((LANES, tm), lambda b, i: (0, i))
    cols_t = pl.BlockSpec((1, d_diff, tm), lambda b, i: (b, 0, i))
    shp = lambda width, dt: jax.ShapeDtypeStruct((bsz, ltot, width), dt)
    shp_t = jax.ShapeDtypeStruct((bsz, d_diff, ltot), BF16)
    return pl.pallas_call(
        functools.partial(_inproj_even_kernel, d_pool=d_pool, d_diff=d_diff),
        grid=(bsz, ltot // tm),
        in_specs=[rows(d), full2(1, d), modspec(d), full2(*w.shape), full2(*wt.shape), tab, tab, tab_t, tab_t],
        out_specs=[rows(d_pool), rows(d_pool), rows(d_diff), cols_t, cols_t, cols_t],
        out_shape=[shp(d_pool, F32), shp(d_pool, F32), shp(d_diff, BF16), shp_t, shp_t, shp_t],
        name="inproj_even",
        compiler_params=_cparams(("parallel", "parallel")),
    )(xc, g.reshape(1, d), mod, w, wt, cos, sin, cos.T, sin.T)


def _attn_kernel(qt_ref, k_ref, vt_ref, sgbt_ref, lam_ref, sub_ref, o_ref, acc_sc, st_sc, mx_sc,
                 *, nhead, nsub, nq, n_ctx_q, ctx_chunks, lat_chunks, lam_init):
    i = pl.program_id(2)
    units = [(e, s) for e in range(nhead) for s in range(nsub)]
    hd = lambda e: slice(e * DIFF_V, (e + 1) * DIFF_V)
    qs = lambda s: slice(s * nq, (s + 1) * nq)
    row = lax.broadcasted_iota(jnp.int32, (DIFF_V, nq), 0)
    ws = []
    for e, s in units:
        qt = qt_ref[0, hd(e), qs(s)]
        zero = jnp.zeros_like(qt)
        ws.append(jnp.concatenate([jnp.where(row < DIFF_HEAD, qt, zero), jnp.where(row >= DIFF_HEAD, qt, zero)],
                                  axis=1))

    ones = jnp.ones((ONES_ROWS, st_sc.shape[2]), BF16)

    def scores(chunk, slot):
        off, size = chunk
        kcs = [k_ref[0, off:off + size, hd(e)] for e in range(nhead)]
        for c, (e, _) in enumerate(units):
            st = jnp.dot(kcs[e], ws[c], preferred_element_type=F32)
            st_sc[slot, c, 0:size] = st
            mx_sc[slot, c] = jnp.max(st, axis=0, keepdims=True)

    def finish(pending):
        for c, (alpha, pv) in enumerate(pending):
            acc_sc[c] = pv if alpha is None else alpha * acc_sc[c] + pv

    def absorb(chunk, slot, ms, pending=None, ahead=None):
        off, size = chunk
        nun = len(units)
        new = [mx_sc[slot, c] if ms is None else jnp.maximum(ms[c], mx_sc[slot, c]) for c in range(nun)]
        alphas = [None if ms is None else jnp.exp2(ms[c] - new[c]) for c in range(nun)]
        if ahead is not None:
            ahead()
        if pending is not None:
            finish(pending)
        vtes = [jnp.concatenate([vt_ref[0, hd(e), off:off + size], ones[:, :size]], axis=0)
                for e in range(nhead)]
        pvs = [jnp.dot(vtes[e], jnp.exp2(st_sc[slot, c, 0:size] - new[c]).astype(BF16),
                       preferred_element_type=F32) for c, (e, _) in enumerate(units)]
        return tuple(new), tuple(zip(alphas, pvs))

    def sweep(chunks):
        slots = SCORE_SLOTS
        for n in range(min(2, len(chunks))):
            scores(chunks[n], n)
        ms, pending = None, None
        for n, chunk in enumerate(chunks):
            ahead = (lambda c=chunks[n + 2], s=(n + 2) % slots: scores(c, s)) if n + 2 < len(chunks) else None
            ms, pending = absorb(chunk, n % slots, ms, pending, ahead)
        finish(pending)

    @pl.when(i < n_ctx_q)
    def _():
        sweep(ctx_chunks)

    @pl.when(i >= n_ctx_q)
    def _():
        sweep(ctx_chunks + lat_chunks)

    lp = lam_ref[...]
    lam = (jnp.exp(jnp.sum(lp[0:1] * lp[1:2], axis=1, keepdims=True))
           - jnp.exp(jnp.sum(lp[2:3] * lp[3:4], axis=1, keepdims=True)) + lam_init)
    accs = [acc_sc[c] for c in range(len(units))]
    o_alls = [acc[:DIFF_V] / acc[DIFF_V:DIFF_V + 1] for acc in accs]
    os = [o_all[:, :nq] - lam * o_all[:, nq:] for o_all in o_alls]
    mss = [jnp.mean(o * o, axis=0, keepdims=True) for o in os]
    gain = sub_ref[...] * (1.0 - lam_init)
    for c, (e, s) in enumerate(units):
        on = os[c] * lax.rsqrt(mss[c] + EPS) * gain
        o_ref[0, hd(e), qs(s)] = (on * sgbt_ref[0, hd(e), qs(s)].astype(F32)).astype(BF16)


def _attention(qt, k, vt, sgbt, lam_p, subln, lc, lam_init):
    bsz, ltot, d_diff = k.shape
    heads = d_diff // DIFF_V
    nq, nsub, tk, nhead = Q_TILE, Q_SUBTILES, KV_TILE, ATTN_HEADS_PER_STEP
    nqs = nq * nsub
    nun = nhead * nsub
    width = nhead * DIFF_V
    assert lc % nqs == 0 and ltot % nqs == 0 and lc % LANES == 0 and (ltot - lc) % tk == 0 and heads % nhead == 0
    ctx_chunks = tuple((o, min(tk, lc - o)) for o in range(0, lc, tk))
    lat_chunks = tuple((o, tk) for o in range(lc, ltot, tk))
    colspec = pl.BlockSpec((1, width, nqs), lambda b, h, i: (b, h, i))
    kern = functools.partial(_attn_kernel, nhead=nhead, nsub=nsub, nq=nq, n_ctx_q=lc // nqs, ctx_chunks=ctx_chunks,
                             lat_chunks=lat_chunks, lam_init=lam_init)
    return pl.pallas_call(
        kern,
        grid=(bsz, heads // nhead, ltot // nqs),
        in_specs=[colspec,
                  pl.BlockSpec((1, ltot, width), lambda b, h, i: (b, 0, h)),
                  pl.BlockSpec((1, width, ltot), lambda b, h, i: (b, h, 0)),
                  colspec,
                  pl.BlockSpec(lam_p.shape, lambda b, h, i: (0, 0)),
                  pl.BlockSpec((DIFF_V, 1), lambda b, h, i: (0, 0))],
        out_specs=colspec,
        out_shape=jax.ShapeDtypeStruct((bsz, d_diff, ltot), BF16),
        scratch_shapes=[pltpu.VMEM((nun, DIFF_V + ONES_ROWS, 2 * nq), F32),
                        pltpu.VMEM((SCORE_SLOTS, nun, tk, 2 * nq), F32),
                        pltpu.VMEM((SCORE_SLOTS, nun, 1, 2 * nq), F32)],
        name="diff_attention",
        compiler_params=_cparams(("parallel", "parallel", "arbitrary")),
    )(qt, k, vt, sgbt, lam_p, subln.reshape(DIFF_V, 1))


def _seq_position(i, tm, ncb, lc, ll):
    nblk = ncb + ll // tm
    is_ctx = i < ncb
    t0 = jnp.where(is_ctx, i, i - ncb) * tm
    lseq = jnp.where(is_ctx, lc, ll)
    has_prev = jnp.logical_and(i != 0, i != ncb)
    has_next = jnp.logical_and(i != ncb - 1, i != nblk - 1)
    return t0, lseq, has_prev, has_next


def _outproj_even_kernel(x_ref, u_ref, up_ref, un_ref, sga_ref, ybt_ref, wpool_ref, pscale_ref, wout_ref,
                         gpost_ref, mod_ref, o_ref, *, tm, ncb, lc, ll, d_pool):
    i = pl.program_id(1)
    t0, lseq, has_prev, has_next = _seq_position(i, tm, ncb, lc, ll)
    y_attn = lax.dot_general(ybt_ref[0], wout_ref[d_pool:, :], (((0,), (0,)), ((), ())),
                             preferred_element_type=F32)
    u = u_ref[0]
    z = jnp.concatenate([jnp.where(has_prev, up_ref[0], 0.0), u, jnp.where(has_next, un_ref[0], 0.0)], axis=0)
    p2 = z[:-1] + z[1:]
    p4 = p2[:-2] + p2[2:]
    p8 = p4[:-4] + p4[4:]
    p16 = p8[:-8] + p8[8:]
    sums = (p2[7:7 + tm], p4[6:6 + tm], p8[4:4 + tm], p16[0:tm])
    lane = lax.broadcasted_iota(jnp.int32, u.shape, 1)
    grp = lane >> 6
    s = jnp.where(grp == 0, sums[0], jnp.where(grp == 1, sums[1], jnp.where(grp == 2, sums[2], sums[3])))
    half = jnp.left_shift(1, grp)
    t = lax.broadcasted_iota(jnp.int32, u.shape, 0) + t0
    cnt = jnp.minimum(t + half - 1, lseq - 1) + 1 - jnp.maximum(t - half, 0)
    dpool = s / cnt.astype(F32) - u
    ya = jnp.dot(dpool.astype(BF16), wpool_ref[...], preferred_element_type=F32) * pscale_ref[...] * sga_ref[0]
    y = jnp.dot(ya.astype(BF16), wout_ref[0:d_pool, :], preferred_element_type=F32) + y_attn
    _postnorm_residual(x_ref, y, gpost_ref, mod_ref, o_ref)


def _halo_specs(tm, halo, width, ltot, first=0):
    r = tm // halo
    prev = pl.BlockSpec((1, halo, width), lambda b, i: (b, jnp.maximum((i + first) * r - 1, 0), 0))
    nxt = pl.BlockSpec((1, halo, width),
                       lambda b, i: (b, jnp.minimum((i + first + 1) * r, ltot // halo - 1), 0))
    return prev, nxt


def _outproj_even(xc, u, sga, ybt, wpool, pscale, wout, gpost, mod, ncb, lc):
    bsz, ltot, d = xc.shape
    tm = ROW_TILE
    d_pool = u.shape[-1]
    rows, full2, modspec = _row_specs(tm, ncb)
    prev, nxt = _halo_specs(tm, POOL_HALO, d_pool, ltot)
    kern = functools.partial(_outproj_even_kernel, tm=tm, ncb=ncb, lc=lc, ll=ltot - lc, d_pool=d_pool)
    return pl.pallas_call(
        kern,
        grid=(bsz, ltot // tm),
        in_specs=[rows(d), rows(d_pool), prev, nxt, rows(d_pool),
                  pl.BlockSpec((1, ybt.shape[1], tm), lambda b, i: (b, 0, i)),
                  full2(*wpool.shape), full2(1, d_pool), full2(*wout.shape), full2(1, d), modspec(d)],
        out_specs=rows(d),
        out_shape=jax.ShapeDtypeStruct(xc.shape, F32),
        name="outproj_even",
        compiler_params=_cparams(("parallel", "parallel")),
    )(xc, u, u, u, sga, ybt, wpool, pscale.reshape(1, d_pool), wout, gpost.reshape(1, d), mod)


def _hgrn_lower_bounds(raw, layer):
    rows = [raw[:, j, :] for j in range(raw.shape[1])]
    mx = functools.reduce(jnp.maximum, rows)
    ex = [jnp.exp(r - mx) for r in rows]
    den = functools.reduce(lambda a, bb: a + bb, ex)
    if layer == 0:
        return jnp.zeros_like(mx)
    return functools.reduce(lambda a, bb: a + bb, [e / den for e in ex[1:layer + 1]])


def _chunk_tri(n, c, rev):
    ri = lax.broadcasted_iota(jnp.int32, (n, n), 0)
    ci = lax.broadcasted_iota(jnp.int32, (n, n), 1)
    shift = int(math.log2(c))
    keep = jnp.logical_and((ri >> shift) == (ci >> shift), (ci >= ri) if rev else (ci <= ri))
    return jnp.where(keep, 1.0, 0.0).astype(BF16)


def _cumsum_rows(x, tri):
    hi = x.astype(BF16)
    r1 = x - hi.astype(F32)
    mid = r1.astype(BF16)
    lo = (r1 - mid.astype(F32)).astype(BF16)
    n = x.shape[1]
    r = jnp.dot(tri, jnp.concatenate([hi, mid, lo], axis=1), preferred_element_type=F32)
    return r[:, :n] + r[:, n:2 * n] + r[:, 2 * n:]


def _inproj_odd_kernel(x_ref, g_ref, mod_ref, w_ref, lbraw_ref, glu_ref, scg_ref, qh_ref, bf_ref, kf_ref, bb_ref,
                       kb_ref, iv_ref, sog_ref, *, d_conv, d_hgrn, layer):
    hb = _prenorm(x_ref, g_ref, mod_ref)
    tm = hb.shape[0]
    lbs = _hgrn_lower_bounds(lbraw_ref[...], layer)

    def glu(r):
        glu_ref[0] = r[:, :d_conv] * jax.nn.sigmoid(r[:, d_conv:])

    def silu_to(ref):
        def put(r):
            ref[0] = _silu(r).astype(BF16)
        return put

    def gate(dirn, b_ref, k_ref):
        def put(r):
            lb = lbs[dirn:dirn + 1]
            f = lb + (1.0 - lb) * jax.nn.sigmoid(r)
            k_ref[0] = (1.0 - f).astype(BF16)
            b_ref[0] = _cumsum_rows(jnp.log2(f), _chunk_tri(tm, HGRN_CHUNK, bool(dirn)))
        return put

    def value(r):
        iv_ref[0] = r.astype(BF16)

    def proj(c0, width):
        return lambda: jnp.dot(hb, w_ref[:, c0:c0 + width], preferred_element_type=F32)

    c0 = 3 * d_conv
    _pipelined([(proj(0, 2 * d_conv), glu), (proj(2 * d_conv, d_conv), silu_to(scg_ref)),
                (proj(c0, d_hgrn), silu_to(qh_ref)), (proj(c0 + d_hgrn, d_hgrn), gate(0, bf_ref, kf_ref)),
                (proj(c0 + 2 * d_hgrn, d_hgrn), gate(1, bb_ref, kb_ref)),
                (proj(c0 + 3 * d_hgrn, d_hgrn), value), (proj(c0 + 4 * d_hgrn, d_hgrn), silu_to(sog_ref))])


def _inproj_odd(xc, g, mod, w, hgrn_lb, layer, ncb, d_conv, d_hgrn):
    bsz, ltot, d = xc.shape
    tm = ROW_TILE
    assert tm % HGRN_CHUNK == 0
    rows, full2, modspec = _row_specs(tm, ncb)
    shp = lambda width, dt: jax.ShapeDtypeStruct((bsz, ltot, width), dt)
    return pl.pallas_call(
        functools.partial(_inproj_odd_kernel, d_conv=d_conv, d_hgrn=d_hgrn, layer=layer),
        grid=(bsz, ltot // tm),
        in_specs=[rows(d), full2(1, d), modspec(d), full2(*w.shape),
                  pl.BlockSpec(hgrn_lb.shape, lambda b, i: (0, 0, 0))],
        out_specs=[rows(d_conv), rows(d_conv), rows(d_hgrn), rows(d_hgrn), rows(d_hgrn), rows(d_hgrn),
                   rows(d_hgrn), rows(d_hgrn), rows(d_hgrn)],
        out_shape=[shp(d_conv, F32), shp(d_conv, BF16), shp(d_hgrn, BF16), shp(d_hgrn, F32), shp(d_hgrn, BF16),
                   shp(d_hgrn, F32), shp(d_hgrn, BF16), shp(d_hgrn, BF16), shp(d_hgrn, BF16)],
        name="inproj_odd",
        compiler_params=_cparams(("parallel", "parallel")),
    )(xc, g.reshape(1, d), mod, w, hgrn_lb)


def _hgrn_masks(c, rev):
    ri = lax.broadcasted_iota(jnp.int32, (c, c), 0)
    ci = lax.broadcasted_iota(jnp.int32, (c, c), 1)
    masks = {}
    w = c // 2
    while w >= HGRN_DIAG:
        shift = int(math.log2(2 * w))
        same_pair = (ri >> shift) == (ci >> shift)
        r_odd, c_odd = (ri & w) != 0, (ci & w) != 0
        owns = jnp.logical_and(c_odd, jnp.logical_not(r_odd)) if rev else jnp.logical_and(r_odd, jnp.logical_not(c_odd))
        masks[w] = jnp.logical_and(same_pair, owns)
        w //= 2
    return masks


class _Chain:
    def __init__(self, q_ref, b_ref, k_ref, v_ref, o_ref, idx, st_ref, rev, masks):
        self.refs = (q_ref, b_ref, k_ref, v_ref, o_ref, idx, st_ref)
        self.rev, self.masks = rev, masks

    def gates(self):
        q_ref, b_ref, k_ref, v_ref, _, idx, _ = self.refs
        self.qf = q_ref[idx].astype(F32)
        self.vb = v_ref[idx]
        self.kk = k_ref[idx].astype(F32)
        self.b = b_ref[idx]

    def products(self):
        qf, kk, b, rev = self.qf, self.kk, self.b, self.rev
        c, n = qf.shape
        self.st = self.refs[6][...]
        self.inter = lax.dot_general((qf * jnp.exp2(b)).astype(BF16), self.st.astype(BF16),
                                     (((1,), (1,)), ((), ())), preferred_element_type=F32)
        self.levels = []
        b3 = b.reshape(c // SUBLANES, SUBLANES, n)
        sub = lax.broadcasted_iota(jnp.int32, (1, SUBLANES, 1), 1)
        w = c // 2
        while w >= HGRN_DIAG:
            if 2 * w >= SUBLANES:
                refs = []
                for p in range(c // (2 * w)):
                    r0 = p * 2 * w + (w if rev else w - 1)
                    refs.append(jnp.broadcast_to(b[r0:r0 + 1, :], (2 * w, n)))
                ref = refs[0] if len(refs) == 1 else jnp.concatenate(refs, axis=0)
            else:
                ref3 = None
                for p in range(SUBLANES // (2 * w)):
                    r0 = p * 2 * w + (w if rev else w - 1)
                    piece = jnp.broadcast_to(b3[:, r0:r0 + 1, :], b3.shape)
                    ref3 = piece if ref3 is None else jnp.where(sub >= p * 2 * w, piece, ref3)
                ref = ref3.reshape(c, n)
            qs = jnp.exp2(jnp.minimum(b - ref, 0.0)) * qf
            ks = jnp.exp2(jnp.minimum(ref - b, 0.0)) * kk
            self.levels.append((w, lax.dot_general(qs.astype(BF16), ks.astype(BF16), (((1,), (1,)), ((), ())),
                                                   preferred_element_type=F32)))
            w //= 2
        self.bl = b[0:1, :] if rev else b[c - 1:c, :]
        kd = (kk * jnp.exp2(self.bl - b)).astype(BF16)
        self.st_add = lax.dot_general(self.vb, kd, (((0,), (0,)), ((), ())), preferred_element_type=F32)

    def diagonal(self):
        c, n = self.qf.shape
        blocks = lambda x: x.reshape(c // SUBLANES, SUBLANES, n)
        b3, k3, v3, q3 = blocks(self.b), blocks(self.kk), blocks(self.vb.astype(F32)), blocks(self.qf)
        sub = lax.broadcasted_iota(jnp.int32, (1, SUBLANES, 1), 1) & (HGRN_DIAG - 1)
        diag = jnp.sum(q3 * k3, axis=2, keepdims=True) * v3
        for d in range(1, HGRN_DIAG):
            sh = SUBLANES - d if self.rev else d
            valid = (sub + d <= HGRN_DIAG - 1) if self.rev else (sub >= d)
            e = jnp.exp2(jnp.where(valid, b3 - pltpu.roll(b3, sh, 1), 0.0))
            a = jnp.sum(q3 * e * pltpu.roll(k3, sh, 1), axis=2, keepdims=True)
            diag = diag + jnp.where(valid, a, 0.0) * pltpu.roll(v3, sh, 1)
        self.diag = diag.reshape(c, n)

    def intra(self):
        att = functools.reduce(lambda x, y: x + y, [jnp.where(self.masks[w], a, 0.0) for w, a in self.levels])
        self.intra_out = jnp.dot(att.astype(BF16), self.vb, preferred_element_type=F32)
        self.refs[6][...] = self.st * jnp.exp2(self.bl) + self.st_add

    def finish(self):
        self.refs[4][self.refs[5]] = self.inter + self.intra_out + self.diag


def _hgrn_kernel(qf_ref, bf_ref, kf_ref, vf_ref, qb_ref, bb_ref, kb_ref, vb_ref, of_ref, ob_ref, st_sc,
                 *, bsz, heads):
    s = pl.program_id(0)

    @pl.when(s == 0)
    def _():
        st_sc[...] = jnp.zeros(st_sc.shape, F32)

    c = qf_ref.shape[1]
    fwd, bwd = _hgrn_masks(c, False), _hgrn_masks(c, True)
    chains = []
    for bi in range(bsz):
        for h in range(heads):
            sl = slice(h * HGRN_HEAD, (h + 1) * HGRN_HEAD)
            idx = (bi, slice(None), sl)
            chains.append(_Chain(qf_ref, bf_ref, kf_ref, vf_ref, of_ref, idx, st_sc.at[bi, 0, h], False, fwd))
            chains.append(_Chain(qb_ref, bb_ref, kb_ref, vb_ref, ob_ref, idx, st_sc.at[bi, 1, h], True, bwd))
    for g0 in range(0, len(chains), HGRN_INTERLEAVE):
        group = chains[g0:g0 + HGRN_INTERLEAVE]
        for stage in (_Chain.gates, _Chain.products, _Chain.intra, _Chain.diagonal, _Chain.finish):
            for ch in group:
                stage(ch)


def _hgrn(qh, b_f, k_f, b_b, k_b, iv, lc):
    bsz, ltot, d_hgrn = qh.shape
    heads = d_hgrn // HGRN_HEAD
    c = HGRN_CHUNK
    assert lc % c == 0 and ltot % c == 0
    ncc, ntot = lc // c, ltot // c

    def bwd(s):
        return jnp.where(s < ncc, ncc - 1 - s, ntot - 1 + ncc - s)

    fspec = pl.BlockSpec((bsz, c, d_hgrn), lambda s: (0, s, 0))
    bspec = pl.BlockSpec((bsz, c, d_hgrn), lambda s: (0, bwd(s), 0))
    kern = functools.partial(_hgrn_kernel, bsz=bsz, heads=heads)
    return pl.pallas_call(
        kern,
        grid=(ntot,),
        in_specs=[fspec, fspec, fspec, fspec, bspec, bspec, bspec, bspec],
        out_specs=[fspec, bspec],
        out_shape=[jax.ShapeDtypeStruct((bsz, ltot, d_hgrn), F32)] * 2,
        scratch_shapes=[pltpu.VMEM((bsz, 2, heads, HGRN_HEAD, HGRN_HEAD), F32)],
        name="hgrn_scan",
        compiler_params=_cparams(("arbitrary",)),
    )(qh, b_f, k_f, iv, qh, b_b, k_b, iv)


def _outproj_odd_kernel(x_ref, glu_ref, gp_ref, gn_ref, scg_ref, cw_ref, cb_ref, lng_ref, lnb_ref,
                        of_ref, ob_ref, hn_ref, sog_ref, wout_ref, gpost_ref, mod_ref, o_ref, z_sc, zs_sc,
                        *, tm, ncb, lc, ll, d_conv, first):
    i = pl.program_id(1) + first
    _, _, has_prev, has_next = _seq_position(i, tm, ncb, lc, ll)
    h = CONV_HALO
    z_sc[0:h, :] = jnp.where(has_prev, gp_ref[0], 0.0)
    z_sc[h:h + tm, :] = glu_ref[0]
    z_sc[h + tm:, :] = jnp.where(has_next, gn_ref[0], 0.0)
    cw = cw_ref[...]
    base = h - CONV_WIDTH // 2
    span = zs_sc.shape[1]
    for r in range(SUBLANES):
        zs_sc[r] = z_sc[r:r + span, :]
    acc = jnp.zeros((tm, d_conv), F32)
    for j in range(CONV_WIDTH):
        r = (base + j) % SUBLANES
        a0 = base + j - r
        acc = acc + cw[j:j + 1, :] * zs_sc[r, a0:a0 + tm, :]
    zc = acc + cb_ref[...]
    mu = jnp.mean(zc, axis=-1, keepdims=True)
    zc = zc - mu
    var = jnp.mean(zc * zc, axis=-1, keepdims=True)
    zn = zc * lax.rsqrt(var + EPS) * lng_ref[...] + lnb_ref[...]
    yc = _silu(zn) * scg_ref[0].astype(F32)
    o = of_ref[0] + ob_ref[0]
    hn = hn_ref[...]
    parts = []
    for hd in range(o.shape[1] // HGRN_HEAD):
        sl = slice(hd * HGRN_HEAD, (hd + 1) * HGRN_HEAD)
        oh = o[:, sl]
        ms = jnp.mean(oh * oh, axis=-1, keepdims=True)
        parts.append(oh * lax.rsqrt(ms + EPS) * hn[:, sl])
    yd = jnp.concatenate(parts, axis=1) * sog_ref[0].astype(F32)
    y = (jnp.dot(yc.astype(BF16), wout_ref[0:d_conv, :], preferred_element_type=F32)
         + jnp.dot(yd.astype(BF16), wout_ref[d_conv:, :], preferred_element_type=F32))
    _postnorm_residual(x_ref, y, gpost_ref, mod_ref, o_ref)


def _outproj_odd(xc, glu, scg, cw, cb, lng, lnb, o_f, o_b, hnorm, sog, wout, gpost, mod, ncb, lc, latent_only):
    bsz, ltot, d = xc.shape
    tm = ROW_TILE
    d_conv = glu.shape[-1]
    d_hgrn = o_f.shape[-1]
    first = ncb if latent_only else 0
    rows, full2, modspec = _row_specs(tm, ncb, first)
    prev, nxt = _halo_specs(tm, CONV_HALO, d_conv, ltot, first)
    conv_span = (CONV_HALO + CONV_WIDTH // 2) // SUBLANES * SUBLANES
    assert SUBLANES - 1 + tm + conv_span <= tm + 2 * CONV_HALO
    kern = functools.partial(_outproj_odd_kernel, tm=tm, ncb=ncb, lc=lc, ll=ltot - lc, d_conv=d_conv, first=first)
    return pl.pallas_call(
        kern,
        grid=(bsz, ltot // tm - first),
        in_specs=[rows(d), rows(d_conv), prev, nxt, rows(d_conv), full2(*cw.shape), full2(1, d_conv),
                  full2(1, d_conv), full2(1, d_conv), rows(d_hgrn), rows(d_hgrn), full2(1, d_hgrn),
                  rows(d_hgrn), full2(*wout.shape), full2(1, d), modspec(d)],
        out_specs=pl.BlockSpec((1, tm, d), lambda b, i: (b, i, 0)),
        out_shape=jax.ShapeDtypeStruct((bsz, ltot - first * tm, d), F32),
        scratch_shapes=[pltpu.VMEM((tm + 2 * CONV_HALO, d_conv), F32),
                        pltpu.VMEM((SUBLANES, tm + conv_span, d_conv), F32)],
        name="outproj_odd",
        compiler_params=_cparams(("parallel", "parallel")),
    )(xc, glu, glu, glu, scg, cw, cb.reshape(1, d_conv), lng.reshape(1, d_conv), lnb.reshape(1, d_conv),
      o_f, o_b, hnorm.reshape(1, d_hgrn), sog, wout, gpost.reshape(1, d), mod)


def _rope_tables(lc, ll):
    t = jnp.arange(ll)
    inv = ROPE_BASE ** (-jnp.arange(ROPE_FREQS, dtype=F32) / ROPE_FREQS)
    ang = jnp.stack([t // GRID_W, t % GRID_W], axis=-1).astype(F32)[:, :, None] * inv
    cos = jnp.cos(ang)
    sin = jnp.sin(ang)
    cos64 = jnp.stack([cos, cos], axis=2).reshape(ll, DIFF_HEAD)
    sin64 = jnp.stack([-sin, sin], axis=2).reshape(ll, DIFF_HEAD)
    cos_t = jnp.concatenate([jnp.ones((lc, DIFF_HEAD), F32), cos64], axis=0)
    sin_t = jnp.concatenate([jnp.zeros((lc, DIFF_HEAD), F32), sin64], axis=0)
    return jnp.tile(cos_t, (1, LANES // DIFF_HEAD)), jnp.tile(sin_t, (1, LANES // DIFF_HEAD))


def _block_diag(w):
    g, a, b = w.shape
    out = jnp.zeros((g * a, g * b), w.dtype)
    for j in range(g):
        out = out.at[j * a:(j + 1) * a, j * b:(j + 1) * b].set(w[j])
    return out


def kernel(x, c, ctx, c_ctx, ada_w, ada_b, norm_pre, norm_post, w_in_even, w_out_even, pool_w, pool_scale,
           diff_lambda, diff_subln, w_in_odd, w_out_odd, conv_w, conv_b, conv_ln_g, conv_ln_b, hgrn_norm,
           hgrn_lb):
    bsz, ll, d = x.shape
    lc = ctx.shape[1]
    depth = ada_w.shape[0]
    d_pool = pool_scale.shape[-1]
    d_diff = w_out_even.shape[1] - d_pool
    d_conv = conv_b.shape[-1]
    d_hgrn = hgrn_norm.shape[-1]
    assert lc % ROW_TILE == 0 and ll % ROW_TILE == 0 and bsz + 1 <= SUBLANES
    ncb = lc // ROW_TILE

    cvec = jnp.zeros((SUBLANES, d), F32).at[:bsz].set(c).at[bsz].set(c_ctx)
    ada = _ada_call(cvec, ada_w, ada_b).reshape(depth, SUBLANES, 3, d)
    xc = jnp.concatenate([ctx, x], axis=1)
    cos, sin = _rope_tables(lc, ll)

    for l in range(depth):
        mod = jnp.stack([jnp.broadcast_to(ada[l, bsz], (bsz, 3, d)), ada[l, :bsz]], axis=1)
        j = l // 2
        if l % 2 == 0:
            lam_init = 0.8 - 0.6 * math.exp(-0.3 * l)
            wb = w_in_even[j].astype(BF16)
            q0, k0, v0, g0 = 2 * d_pool, 2 * d_pool + d_diff, 2 * d_pool + 2 * d_diff, 2 * d_pool + 3 * d_diff
            w_rows = jnp.concatenate([wb[:, :q0], wb[:, k0:v0]], axis=1)
            w_cols_t = jnp.concatenate([wb[:, q0:k0], wb[:, v0:g0], wb[:, g0:]], axis=1).T
            u, sga, k, qt, vt, sgbt = _inproj_even(xc, norm_pre[l], mod, w_rows, w_cols_t, cos, sin, ncb,
                                                   d_pool, d_diff)
            ybt = _attention(qt, k, vt, sgbt, diff_lambda[j], diff_subln[j], lc, lam_init)
            xc = _outproj_even(xc, u, sga, ybt, _block_diag(pool_w[j]).astype(BF16), pool_scale[j],
                               w_out_even[j].astype(BF16), norm_post[l], mod, ncb, lc)
        else:
            glu, scg, qh, b_f, k_f, b_b, k_b, iv, sog = _inproj_odd(xc, norm_pre[l], mod, w_in_odd[j].astype(BF16),
                                                                    hgrn_lb, l, ncb, d_conv, d_hgrn)
            o_f, o_b = _hgrn(qh, b_f, k_f, b_b, k_b, iv, lc)
            xc = _outproj_odd(xc, glu, scg, conv_w[j], conv_b[j], conv_ln_g[j], conv_ln_b[j], o_f, o_b,
                              hgrn_norm[j], sog, w_out_odd[j].astype(BF16), norm_post[l], mod, ncb, lc,
                              latent_only=l == depth - 1)
    return xc if depth % 2 == 0 else xc[:, lc:]
```

```python
import functools
import math

import jax
import jax.numpy as jnp
from jax import lax
from jax.experimental import pallas as pl
from jax.experimental.pallas import tpu as pltpu

F32 = jnp.float32
BF16 = jnp.bfloat16
EPS = 1e-6

GRID_W = 64
POOL_WINDOWS = (2, 4, 8, 16)
POOL_HALO = max(POOL_WINDOWS) // 2
DIFF_HEAD = 64
DIFF_V = 2 * DIFF_HEAD
DIFF_SCALE = DIFF_HEAD ** -0.5
ROPE_BASE = 10000.0
ROPE_FREQS = DIFF_HEAD // 4
CONV_WIDTH = 31
HGRN_HEAD = 128
LOG2E = 1.4426950408889634

LANES = 128
SUBLANES = 8
VMEM_LIMIT = 48 * 1024 * 1024

ROW_TILE = 256
CONV_HALO = 16
Q_TILE = 128
Q_SUBTILES = 2
ATTN_HEADS_PER_STEP = 1
KV_TILE = 512
SCORE_SLOTS = 3
ONES_ROWS = 16
HGRN_CHUNK = 64
HGRN_DIAG = 2
HGRN_INTERLEAVE = 16


def _cparams(sem):
    return pltpu.CompilerParams(dimension_semantics=sem, vmem_limit_bytes=VMEM_LIMIT)


def _silu(x):
    return x * jax.nn.sigmoid(x)


def _ada_kernel(c_ref, w_ref, b_ref, o_ref):
    c = c_ref[...]
    o_ref[0] = jnp.dot(_silu(c), w_ref[0], preferred_element_type=F32,
                       precision=lax.Precision.HIGHEST) + b_ref[0]


def _ada_call(cvec, ada_w, ada_b):
    depth, d, d3 = ada_w.shape
    nj = d3 // d
    return pl.pallas_call(
        _ada_kernel,
        grid=(depth, nj),
        in_specs=[pl.BlockSpec((SUBLANES, d), lambda l, j: (0, 0)),
                  pl.BlockSpec((1, d, d), lambda l, j: (l, 0, j)),
                  pl.BlockSpec((1, 1, d), lambda l, j: (l, 0, j))],
        out_specs=pl.BlockSpec((1, SUBLANES, d), lambda l, j: (l, 0, j)),
        out_shape=jax.ShapeDtypeStruct((depth, SUBLANES, d3), F32),
        name="ada_params",
        compiler_params=_cparams(("parallel", "parallel")),
    )(cvec, ada_w, ada_b.reshape(depth, 1, d3))


def _prenorm(x_ref, g_ref, mod_ref):
    x = x_ref[0]
    mod = mod_ref[0, 0]
    ms = jnp.mean(x * x, axis=-1, keepdims=True)
    h = x * lax.rsqrt(ms + EPS) * g_ref[...] * (1.0 + mod[1:2]) + mod[0:1]
    return h.astype(BF16)


def _pipelined(stages):
    nxt = stages[0][0]()
    for g, (_, consume) in enumerate(stages):
        cur = nxt
        if g + 1 < len(stages):
            nxt = stages[g + 1][0]()
        consume(cur)


def _postnorm_residual(x_ref, y, gpost_ref, mod_ref, o_ref):
    mod = mod_ref[0, 0]
    ms = jnp.mean(y * y, axis=-1, keepdims=True)
    yn = y * lax.rsqrt(ms + EPS) * gpost_ref[...]
    o_ref[0] = x_ref[0] + mod[2:3] * yn


def _row_specs(tm, ncb, first=0):
    def rows(width):
        return pl.BlockSpec((1, tm, width), lambda b, i: (b, i + first, 0))

    def full2(a, bdim):
        return pl.BlockSpec((a, bdim), lambda b, i: (0, 0))

    def mod(d):
        return pl.BlockSpec((1, 1, 3, d), lambda b, i: (b, jnp.where(i + first < ncb, 0, 1), 0, 0))

    return rows, full2, mod


def _inproj_even_kernel(x_ref, g_ref, mod_ref, w_ref, wt_ref, cos_ref, sin_ref, cost_ref, sint_ref,
                        u_ref, sga_ref, k_ref, qt_ref, vt_ref, sgbt_ref, *, d_pool, d_diff):
    hb = _prenorm(x_ref, g_ref, mod_ref)
    wide = 2 * LANES

    def proj(c0, width):
        return lambda: jnp.dot(hb, w_ref[:, c0:c0 + width], preferred_element_type=F32)

    def proj_t(r0, height):
        return lambda: lax.dot_general(wt_ref[r0:r0 + height, :], hb, (((1,), (1,)), ((), ())),
                                       preferred_element_type=F32)

    def pool(r):
        u_ref[0] = r[:, :d_pool]
        sga_ref[0] = _silu(r[:, d_pool:])

    lane = lax.broadcasted_iota(jnp.int32, cos_ref.shape, 1)
    first_half = (lane & ROPE_FREQS) == 0

    def keys(c0):
        def put(r):
            for s in range(wide // LANES):
                t = r[:, s * LANES:(s + 1) * LANES]
                partner = jnp.where(first_half, pltpu.roll(t, LANES - ROPE_FREQS, 1), pltpu.roll(t, ROPE_FREQS, 1))
                k_ref[0, :, c0 + s * LANES:c0 + (s + 1) * LANES] = (t * cos_ref[...] + partner * sin_ref[...]).astype(BF16)
        return put

    def gate(r0):
        def put(r):
            sgbt_ref[0, r0:r0 + wide, :] = _silu(r).astype(BF16)
        return put

    def queries(r0):
        f = ROPE_FREQS

        def put(r):
            for s in range(wide // LANES):
                t = r[s * LANES:(s + 1) * LANES]
                partner = jnp.concatenate([t[(blk ^ 1) * f:((blk ^ 1) + 1) * f] for blk in range(LANES // f)], axis=0)
                qt_ref[0, r0 + s * LANES:r0 + (s + 1) * LANES, :] = (
                    (t * cost_ref[...] + partner * sint_ref[...]) * (DIFF_SCALE * LOG2E)).astype(BF16)
        return put

    def values(r0):
        def put(r):
            vt_ref[0, r0:r0 + wide, :] = r.astype(BF16)
        return put

    k0 = 2 * d_pool
    stages = [(proj(0, 2 * d_pool), pool)]
    stages += [(proj(k0 + c, wide), keys(c)) for c in range(0, d_diff, wide)]
    stages += [(proj_t(r, wide), queries(r)) for r in range(0, d_diff, wide)]
    stages += [(proj_t(d_diff + r, wide), values(r)) for r in range(0, d_diff, wide)]
    stages += [(proj_t(2 * d_diff + r, wide), gate(r)) for r in range(0, d_diff, wide)]
    _pipelined(stages)


def _inproj_even(xc, g, mod, w, wt, cos, sin, ncb, d_pool, d_diff):
    bsz, ltot, d = xc.shape
    tm = ROW_TILE
    rows, full2, modspec = _row_specs(tm, ncb)
    tab = pl.BlockSpec((tm, LANES), lambda b, i: (i, 0))
    tab_t = pl.BlockSpec((LANES, tm), lambda b, i: (0, i))
    cols_t = pl.BlockSpec((1, d_diff, tm), lambda b, i: (b, 0, i))
    shp = lambda width, dt: jax.ShapeDtypeStruct((bsz, ltot, width), dt)
    shp_t = jax.ShapeDtypeStruct((bsz, d_diff, ltot), BF16)
    return pl.pallas_call(
        functools.partial(_inproj_even_kernel, d_pool=d_pool, d_diff=d_diff),
        grid=(bsz, ltot // tm),
        in_specs=[rows(d), full2(1, d), modspec(d), full2(*w.shape), full2(*wt.shape), tab, tab, tab_t, tab_t],
        out_specs=[rows(d_pool), rows(d_pool), rows(d_diff), cols_t, cols_t, cols_t],
        out_shape=[shp(d_pool, F32), shp(d_pool, F32), shp(d_diff, BF16), shp_t, shp_t, shp_t],
        name="inproj_even",
        compiler_params=_cparams(("parallel", "parallel")),
    )(xc, g.reshape(1, d), mod, w, wt, cos, sin, cos.T, sin.T)


def _attn_kernel(qt_ref, k_ref, vt_ref, sgbt_ref, lam_ref, sub_ref, o_ref, acc_sc, st_sc, mx_sc,
                 *, nhead, nsub, nq, n_ctx_q, ctx_chunks, lat_chunks, lam_init):
    i = pl.program_id(2)
    units = [(e, s) for e in range(nhead) for s in range(nsub)]
    hd = lambda e: slice(e * DIFF_V, (e + 1) * DIFF_V)
    qs = lambda s: slice(s * nq, (s + 1) * nq)
    row = lax.broadcasted_iota(jnp.int32, (DIFF_V, nq), 0)
    ws = []
    for e, s in units:
        qt = qt_ref[0, hd(e), qs(s)]
        zero = jnp.zeros_like(qt)
        ws.append(jnp.concatenate([jnp.where(row < DIFF_HEAD, qt, zero), jnp.where(row >= DIFF_HEAD, qt, zero)],
                                  axis=1))

    ones = jnp.ones((ONES_ROWS, st_sc.shape[2]), BF16)

    def scores(chunk, slot):
        off, size = chunk
        kcs = [k_ref[0, off:off + size, hd(e)] for e in range(nhead)]
        for c, (e, _) in enumerate(units):
            st = jnp.dot(kcs[e], ws[c], preferred_element_type=F32)
            st_sc[slot, c, 0:size] = st
            mx_sc[slot, c] = jnp.max(st, axis=0, keepdims=True)

    def finish(pending):
        for c, (alpha, pv) in enumerate(pending):
            acc_sc[c] = pv if alpha is None else alpha * acc_sc[c] + pv

    def absorb(chunk, slot, ms, pending=None, ahead=None):
        off, size = chunk
        nun = len(units)
        new = [mx_sc[slot, c] if ms is None else jnp.maximum(ms[c], mx_sc[slot, c]) for c in range(nun)]
        alphas = [None if ms is None else jnp.exp2(ms[c] - new[c]) for c in range(nun)]
        if ahead is not None:
            ahead()
        if pending is not None:
            finish(pending)
        vtes = [jnp.concatenate([vt_ref[0, hd(e), off:off + size], ones[:, :size]], axis=0)
                for e in range(nhead)]
        pvs = [jnp.dot(vtes[e], jnp.exp2(st_sc[slot, c, 0:size] - new[c]).astype(BF16),
                       preferred_element_type=F32) for c, (e, _) in enumerate(units)]
        return tuple(new), tuple(zip(alphas, pvs))

    def sweep(chunks):
        slots = SCORE_SLOTS
        for n in range(min(2, len(chunks))):
            scores(chunks[n], n)
        ms, pending = None, None
        for n, chunk in enumerate(chunks):
            ahead = (lambda c=chunks[n + 2], s=(n + 2) % slots: scores(c, s)) if n + 2 < len(chunks) else None
            ms, pending = absorb(chunk, n % slots, ms, pending, ahead)
        finish(pending)

    @pl.when(i < n_ctx_q)
    def _():
        sweep(ctx_chunks)

    @pl.when(i >= n_ctx_q)
    def _():
        sweep(ctx_chunks + lat_chunks)

    lp = lam_ref[...]
    lam = (jnp.exp(jnp.sum(lp[0:1] * lp[1:2], axis=1, keepdims=True))
           - jnp.exp(jnp.sum(lp[2:3] * lp[3:4], axis=1, keepdims=True)) + lam_init)
    accs = [acc_sc[c] for c in range(len(units))]
    o_alls = [acc[:DIFF_V] / acc[DIFF_V:DIFF_V + 1] for acc in accs]
    os = [o_all[:, :nq] - lam * o_all[:, nq:] for o_all in o_alls]
    mss = [jnp.mean(o * o, axis=0, keepdims=True) for o in os]
    gain = sub_ref[...] * (1.0 - lam_init)
    for c, (e, s) in enumerate(units):
        on = os[c] * lax.rsqrt(mss[c] + EPS) * gain
        o_ref[0, hd(e), qs(s)] = (on * sgbt_ref[0, hd(e), qs(s)].astype(F32)).astype(BF16)


def _attention(qt, k, vt, sgbt, lam_p, subln, lc, lam_init):
    bsz, ltot, d_diff = k.shape
    heads = d_diff // DIFF_V
    nq, nsub, tk, nhead = Q_TILE, Q_SUBTILES, KV_TILE, ATTN_HEADS_PER_STEP
    nqs = nq * nsub
    nun = nhead * nsub
    width = nhead * DIFF_V
    assert lc % nqs == 0 and ltot % nqs == 0 and lc % LANES == 0 and (ltot - lc) % tk == 0 and heads % nhead == 0
    ctx_chunks = tuple((o, min(tk, lc - o)) for o in range(0, lc, tk))
    lat_chunks = tuple((o, tk) for o in range(lc, ltot, tk))
    colspec = pl.BlockSpec((1, width, nqs), lambda b, h, i: (b, h, i))
    kern = functools.partial(_attn_kernel, nhead=nhead, nsub=nsub, nq=nq, n_ctx_q=lc // nqs, ctx_chunks=ctx_chunks,
                             lat_chunks=lat_chunks, lam_init=lam_init)
    return pl.pallas_call(
        kern,
        grid=(bsz, heads // nhead, ltot // nqs),
        in_specs=[colspec,
                  pl.BlockSpec((1, ltot, width), lambda b, h, i: (b, 0, h)),
                  pl.BlockSpec((1, width, ltot), lambda b, h, i: (b, h, 0)),
                  colspec,
                  pl.BlockSpec(lam_p.shape, lambda b, h, i: (0, 0)),
                  pl.BlockSpec((DIFF_V, 1), lambda b, h, i: (0, 0))],
        out_specs=colspec,
        out_shape=jax.ShapeDtypeStruct((bsz, d_diff, ltot), BF16),
        scratch_shapes=[pltpu.VMEM((nun, DIFF_V + ONES_ROWS, 2 * nq), F32),
                        pltpu.VMEM((SCORE_SLOTS, nun, tk, 2 * nq), F32),
                        pltpu.VMEM((SCORE_SLOTS, nun, 1, 2 * nq), F32)],
        name="diff_attention",
        compiler_params=_cparams(("parallel", "parallel", "arbitrary")),
    )(qt, k, vt, sgbt, lam_p, subln.reshape(DIFF_V, 1))


def _seq_position(i, tm, ncb, lc, ll):
    nblk = ncb + ll // tm
    is_ctx = i < ncb
    t0 = jnp.where(is_ctx, i, i - ncb) * tm
    lseq = jnp.where(is_ctx, lc, ll)
    has_prev = jnp.logical_and(i != 0, i != ncb)
    has_next = jnp.logical_and(i != ncb - 1, i != nblk - 1)
    return t0, lseq, has_prev, has_next


def _outproj_even_kernel(x_ref, u_ref, up_ref, un_ref, sga_ref, ybt_ref, wpool_ref, pscale_ref, wout_ref,
                         gpost_ref, mod_ref, o_ref, *, tm, ncb, lc, ll, d_pool):
    i = pl.program_id(1)
    t0, lseq, has_prev, has_next = _seq_position(i, tm, ncb, lc, ll)
    y_attn = lax.dot_general(ybt_ref[0], wout_ref[d_pool:, :], (((0,), (0,)), ((), ())),
                             preferred_element_type=F32)
    u = u_ref[0]
    z = jnp.concatenate([jnp.where(has_prev, up_ref[0], 0.0), u, jnp.where(has_next, un_ref[0], 0.0)], axis=0)
    group_width = d_pool // len(POOL_WINDOWS)
    assert group_width & (group_width - 1) == 0
    grp = lax.broadcasted_iota(jnp.int32, u.shape, 1) >> int(math.log2(group_width))
    p, w, s, half = z, 1, None, None
    for g, win in enumerate(POOL_WINDOWS):
        while w < win:
            p = p[:-w] + p[w:]
            w *= 2
        assert w == win, "pooling windows must be powers of two in increasing order"
        start = POOL_HALO - win // 2
        s = p[start:start + tm] if s is None else jnp.where(grp == g, p[start:start + tm], s)
        half = jnp.full(u.shape, win // 2, jnp.int32) if half is None else jnp.where(grp == g, win // 2, half)
    t = lax.broadcasted_iota(jnp.int32, u.shape, 0) + t0
    cnt = jnp.minimum(t + half - 1, lseq - 1) + 1 - jnp.maximum(t - half, 0)
    dpool = s / cnt.astype(F32) - u
    ya = jnp.dot(dpool.astype(BF16), wpool_ref[...], preferred_element_type=F32) * pscale_ref[...] * sga_ref[0]
    y = jnp.dot(ya.astype(BF16), wout_ref[0:d_pool, :], preferred_element_type=F32) + y_attn
    _postnorm_residual(x_ref, y, gpost_ref, mod_ref, o_ref)


def _halo_specs(tm, halo, width, ltot, first=0):
    r = tm // halo
    prev = pl.BlockSpec((1, halo, width), lambda b, i: (b, jnp.maximum((i + first) * r - 1, 0), 0))
    nxt = pl.BlockSpec((1, halo, width),
                       lambda b, i: (b, jnp.minimum((i + first + 1) * r, ltot // halo - 1), 0))
    return prev, nxt


def _outproj_even(xc, u, sga, ybt, wpool, pscale, wout, gpost, mod, ncb, lc):
    bsz, ltot, d = xc.shape
    tm = ROW_TILE
    d_pool = u.shape[-1]
    rows, full2, modspec = _row_specs(tm, ncb)
    prev, nxt = _halo_specs(tm, POOL_HALO, d_pool, ltot)
    kern = functools.partial(_outproj_even_kernel, tm=tm, ncb=ncb, lc=lc, ll=ltot - lc, d_pool=d_pool)
    return pl.pallas_call(
        kern,
        grid=(bsz, ltot // tm),
        in_specs=[rows(d), rows(d_pool), prev, nxt, rows(d_pool),
                  pl.BlockSpec((1, ybt.shape[1], tm), lambda b, i: (b, 0, i)),
                  full2(*wpool.shape), full2(1, d_pool), full2(*wout.shape), full2(1, d), modspec(d)],
        out_specs=rows(d),
        out_shape=jax.ShapeDtypeStruct(xc.shape, F32),
        name="outproj_even",
        compiler_params=_cparams(("parallel", "parallel")),
    )(xc, u, u, u, sga, ybt, wpool, pscale.reshape(1, d_pool), wout, gpost.reshape(1, d), mod)


def _hgrn_lower_bounds(raw, layer):
    rows = [raw[:, j, :] for j in range(raw.shape[1])]
    mx = functools.reduce(jnp.maximum, rows)
    ex = [jnp.exp(r - mx) for r in rows]
    den = functools.reduce(lambda a, bb: a + bb, ex)
    if layer == 0:
        return jnp.zeros_like(mx)
    return functools.reduce(lambda a, bb: a + bb, [e / den for e in ex[1:layer + 1]])


def _chunk_tri(n, c, rev):
    ri = lax.broadcasted_iota(jnp.int32, (n, n), 0)
    ci = lax.broadcasted_iota(jnp.int32, (n, n), 1)
    shift = int(math.log2(c))
    keep = jnp.logical_and((ri >> shift) == (ci >> shift), (ci >= ri) if rev else (ci <= ri))
    return jnp.where(keep, 1.0, 0.0).astype(BF16)


def _cumsum_rows(x, tri):
    hi = x.astype(BF16)
    r1 = x - hi.astype(F32)
    mid = r1.astype(BF16)
    lo = (r1 - mid.astype(F32)).astype(BF16)
    n = x.shape[1]
    r = jnp.dot(tri, jnp.concatenate([hi, mid, lo], axis=1), preferred_element_type=F32)
    return r[:, :n] + r[:, n:2 * n] + r[:, 2 * n:]


def _inproj_odd_kernel(x_ref, g_ref, mod_ref, w_ref, lbraw_ref, glu_ref, scg_ref, qh_ref, bf_ref, kf_ref, bb_ref,
                       kb_ref, iv_ref, sog_ref, *, d_conv, d_hgrn, layer):
    hb = _prenorm(x_ref, g_ref, mod_ref)
    tm = hb.shape[0]
    lbs = _hgrn_lower_bounds(lbraw_ref[...], layer)

    def glu(r):
        glu_ref[0] = r[:, :d_conv] * jax.nn.sigmoid(r[:, d_conv:])

    def silu_to(ref):
        def put(r):
            ref[0] = _silu(r).astype(BF16)
        return put

    def gate(dirn, b_ref, k_ref):
        def put(r):
            lb = lbs[dirn:dirn + 1]
            f = lb + (1.0 - lb) * jax.nn.sigmoid(r)
            k_ref[0] = (1.0 - f).astype(BF16)
            b_ref[0] = _cumsum_rows(jnp.log2(f), _chunk_tri(tm, HGRN_CHUNK, bool(dirn)))
        return put

    def value(r):
        iv_ref[0] = r.astype(BF16)

    def proj(c0, width):
        return lambda: jnp.dot(hb, w_ref[:, c0:c0 + width], preferred_element_type=F32)

    c0 = 3 * d_conv
    _pipelined([(proj(0, 2 * d_conv), glu), (proj(2 * d_conv, d_conv), silu_to(scg_ref)),
                (proj(c0, d_hgrn), silu_to(qh_ref)), (proj(c0 + d_hgrn, d_hgrn), gate(0, bf_ref, kf_ref)),
                (proj(c0 + 2 * d_hgrn, d_hgrn), gate(1, bb_ref, kb_ref)),
                (proj(c0 + 3 * d_hgrn, d_hgrn), value), (proj(c0 + 4 * d_hgrn, d_hgrn), silu_to(sog_ref))])


def _inproj_odd(xc, g, mod, w, hgrn_lb, layer, ncb, d_conv, d_hgrn):
    bsz, ltot, d = xc.shape
    tm = ROW_TILE
    assert tm % HGRN_CHUNK == 0
    rows, full2, modspec = _row_specs(tm, ncb)
    shp = lambda width, dt: jax.ShapeDtypeStruct((bsz, ltot, width), dt)
    return pl.pallas_call(
        functools.partial(_inproj_odd_kernel, d_conv=d_conv, d_hgrn=d_hgrn, layer=layer),
        grid=(bsz, ltot // tm),
        in_specs=[rows(d), full2(1, d), modspec(d), full2(*w.shape),
                  pl.BlockSpec(hgrn_lb.shape, lambda b, i: (0, 0, 0))],
        out_specs=[rows(d_conv), rows(d_conv), rows(d_hgrn), rows(d_hgrn), rows(d_hgrn), rows(d_hgrn),
                   rows(d_hgrn), rows(d_hgrn), rows(d_hgrn)],
        out_shape=[shp(d_conv, F32), shp(d_conv, BF16), shp(d_hgrn, BF16), shp(d_hgrn, F32), shp(d_hgrn, BF16),
                   shp(d_hgrn, F32), shp(d_hgrn, BF16), shp(d_hgrn, BF16), shp(d_hgrn, BF16)],
        name="inproj_odd",
        compiler_params=_cparams(("parallel", "parallel")),
    )(xc, g.reshape(1, d), mod, w, hgrn_lb)


def _hgrn_masks(c, rev):
    ri = lax.broadcasted_iota(jnp.int32, (c, c), 0)
    ci = lax.broadcasted_iota(jnp.int32, (c, c), 1)
    masks = {}
    w = c // 2
    while w >= HGRN_DIAG:
        shift = int(math.log2(2 * w))
        same_pair = (ri >> shift) == (ci >> shift)
        r_odd, c_odd = (ri & w) != 0, (ci & w) != 0
        owns = jnp.logical_and(c_odd, jnp.logical_not(r_odd)) if rev else jnp.logical_and(r_odd, jnp.logical_not(c_odd))
        masks[w] = jnp.logical_and(same_pair, owns)
        w //= 2
    return masks


class _Chain:
    def __init__(self, q_ref, b_ref, k_ref, v_ref, o_ref, idx, st_ref, rev, masks):
        self.refs = (q_ref, b_ref, k_ref, v_ref, o_ref, idx, st_ref)
        self.rev, self.masks = rev, masks

    def load(self):
        q_ref, b_ref, k_ref, v_ref, _, idx, _ = self.refs
        self.qf = q_ref[idx].astype(F32)
        self.vb = v_ref[idx]
        self.kk = k_ref[idx].astype(F32)
        self.b = b_ref[idx]

    def products(self):
        qf, kk, b, rev = self.qf, self.kk, self.b, self.rev
        c, n = qf.shape
        self.st = self.refs[6][...]
        self.inter = lax.dot_general((qf * jnp.exp2(b)).astype(BF16), self.st.astype(BF16),
                                     (((1,), (1,)), ((), ())), preferred_element_type=F32)
        self.levels = []
        b3 = b.reshape(c // SUBLANES, SUBLANES, n)
        sub = lax.broadcasted_iota(jnp.int32, (1, SUBLANES, 1), 1)
        w = c // 2
        while w >= HGRN_DIAG:
            if 2 * w >= SUBLANES:
                refs = []
                for p in range(c // (2 * w)):
                    r0 = p * 2 * w + (w if rev else w - 1)
                    refs.append(jnp.broadcast_to(b[r0:r0 + 1, :], (2 * w, n)))
                ref = refs[0] if len(refs) == 1 else jnp.concatenate(refs, axis=0)
            else:
                ref3 = None
                for p in range(SUBLANES // (2 * w)):
                    r0 = p * 2 * w + (w if rev else w - 1)
                    piece = jnp.broadcast_to(b3[:, r0:r0 + 1, :], b3.shape)
                    ref3 = piece if ref3 is None else jnp.where(sub >= p * 2 * w, piece, ref3)
                ref = ref3.reshape(c, n)
            qs = jnp.exp2(jnp.minimum(b - ref, 0.0)) * qf
            ks = jnp.exp2(jnp.minimum(ref - b, 0.0)) * kk
            self.levels.append((w, lax.dot_general(qs.astype(BF16), ks.astype(BF16), (((1,), (1,)), ((), ())),
                                                   preferred_element_type=F32)))
            w //= 2
        self.bl = b[0:1, :] if rev else b[c - 1:c, :]
        kd = (kk * jnp.exp2(self.bl - b)).astype(BF16)
        self.st_add = lax.dot_general(self.vb, kd, (((0,), (0,)), ((), ())), preferred_element_type=F32)

    def diagonal(self):
        c, n = self.qf.shape
        blocks = lambda x: x.reshape(c // SUBLANES, SUBLANES, n)
        b3, k3, v3, q3 = blocks(self.b), blocks(self.kk), blocks(self.vb.astype(F32)), blocks(self.qf)
        sub = lax.broadcasted_iota(jnp.int32, (1, SUBLANES, 1), 1) & (HGRN_DIAG - 1)
        diag = jnp.sum(q3 * k3, axis=2, keepdims=True) * v3
        for d in range(1, HGRN_DIAG):
            sh = SUBLANES - d if self.rev else d
            valid = (sub + d <= HGRN_DIAG - 1) if self.rev else (sub >= d)
            e = jnp.exp2(jnp.where(valid, b3 - pltpu.roll(b3, sh, 1), 0.0))
            a = jnp.sum(q3 * e * pltpu.roll(k3, sh, 1), axis=2, keepdims=True)
            diag = diag + jnp.where(valid, a, 0.0) * pltpu.roll(v3, sh, 1)
        self.diag = diag.reshape(c, n)

    def intra(self):
        att = functools.reduce(lambda x, y: x + y, [jnp.where(self.masks[w], a, 0.0) for w, a in self.levels])
        self.intra_out = jnp.dot(att.astype(BF16), self.vb, preferred_element_type=F32)
        self.refs[6][...] = self.st * jnp.exp2(self.bl) + self.st_add

    def finish(self):
        self.refs[4][self.refs[5]] = self.inter + self.intra_out + self.diag


def _hgrn_kernel(qf_ref, bf_ref, kf_ref, vf_ref, qb_ref, bb_ref, kb_ref, vb_ref, of_ref, ob_ref, st_sc,
                 *, bsz, heads):
    s = pl.program_id(0)

    @pl.when(s == 0)
    def _():
        st_sc[...] = jnp.zeros(st_sc.shape, F32)

    c = qf_ref.shape[1]
    fwd, bwd = _hgrn_masks(c, False), _hgrn_masks(c, True)
    chains = []
    for bi in range(bsz):
        for h in range(heads):
            sl = slice(h * HGRN_HEAD, (h + 1) * HGRN_HEAD)
            idx = (bi, slice(None), sl)
            chains.append(_Chain(qf_ref, bf_ref, kf_ref, vf_ref, of_ref, idx, st_sc.at[bi, 0, h], False, fwd))
            chains.append(_Chain(qb_ref, bb_ref, kb_ref, vb_ref, ob_ref, idx, st_sc.at[bi, 1, h], True, bwd))
    for g0 in range(0, len(chains), HGRN_INTERLEAVE):
        group = chains[g0:g0 + HGRN_INTERLEAVE]
        for stage in (_Chain.load, _Chain.products, _Chain.intra, _Chain.diagonal, _Chain.finish):
            for ch in group:
                stage(ch)


def _hgrn(qh, b_f, k_f, b_b, k_b, iv, lc):
    bsz, ltot, d_hgrn = qh.shape
    heads = d_hgrn // HGRN_HEAD
    c = HGRN_CHUNK
    assert lc % c == 0 and ltot % c == 0
    ncc, ntot = lc // c, ltot // c

    def bwd(s):
        return jnp.where(s < ncc, ncc - 1 - s, ntot - 1 + ncc - s)

    fspec = pl.BlockSpec((bsz, c, d_hgrn), lambda s: (0, s, 0))
    bspec = pl.BlockSpec((bsz, c, d_hgrn), lambda s: (0, bwd(s), 0))
    kern = functools.partial(_hgrn_kernel, bsz=bsz, heads=heads)
    return pl.pallas_call(
        kern,
        grid=(ntot,),
        in_specs=[fspec, fspec, fspec, fspec, bspec, bspec, bspec, bspec],
        out_specs=[fspec, bspec],
        out_shape=[jax.ShapeDtypeStruct((bsz, ltot, d_hgrn), F32)] * 2,
        scratch_shapes=[pltpu.VMEM((bsz, 2, heads, HGRN_HEAD, HGRN_HEAD), F32)],
        name="hgrn_scan",
        compiler_params=_cparams(("arbitrary",)),
    )(qh, b_f, k_f, iv, qh, b_b, k_b, iv)


def _outproj_odd_kernel(x_ref, glu_ref, gp_ref, gn_ref, scg_ref, cw_ref, cb_ref, lng_ref, lnb_ref,
                        of_ref, ob_ref, hn_ref, sog_ref, wout_ref, gpost_ref, mod_ref, o_ref, z_sc, zs_sc,
                        *, tm, ncb, lc, ll, d_conv, first):
    i = pl.program_id(1) + first
    _, _, has_prev, has_next = _seq_position(i, tm, ncb, lc, ll)
    h = CONV_HALO
    z_sc[0:h, :] = jnp.where(has_prev, gp_ref[0], 0.0)
    z_sc[h:h + tm, :] = glu_ref[0]
    z_sc[h + tm:, :] = jnp.where(has_next, gn_ref[0], 0.0)
    cw = cw_ref[...]
    base = h - CONV_WIDTH // 2
    span = zs_sc.shape[1]
    for r in range(SUBLANES):
        zs_sc[r] = z_sc[r:r + span, :]
    acc = jnp.zeros((tm, d_conv), F32)
    for j in range(CONV_WIDTH):
        r = (base + j) % SUBLANES
        a0 = base + j - r
        acc = acc + cw[j:j + 1, :] * zs_sc[r, a0:a0 + tm, :]
    zc = acc + cb_ref[...]
    mu = jnp.mean(zc, axis=-1, keepdims=True)
    zc = zc - mu
    var = jnp.mean(zc * zc, axis=-1, keepdims=True)
    zn = zc * lax.rsqrt(var + EPS) * lng_ref[...] + lnb_ref[...]
    yc = _silu(zn) * scg_ref[0].astype(F32)
    o = of_ref[0] + ob_ref[0]
    hn = hn_ref[...]
    parts = []
    for hd in range(o.shape[1] // HGRN_HEAD):
        sl = slice(hd * HGRN_HEAD, (hd + 1) * HGRN_HEAD)
        oh = o[:, sl]
        ms = jnp.mean(oh * oh, axis=-1, keepdims=True)
        parts.append(oh * lax.rsqrt(ms + EPS) * hn[:, sl])
    yd = jnp.concatenate(parts, axis=1) * sog_ref[0].astype(F32)
    y = (jnp.dot(yc.astype(BF16), wout_ref[0:d_conv, :], preferred_element_type=F32)
         + jnp.dot(yd.astype(BF16), wout_ref[d_conv:, :], preferred_element_type=F32))
    _postnorm_residual(x_ref, y, gpost_ref, mod_ref, o_ref)


def _outproj_odd(xc, glu, scg, cw, cb, lng, lnb, o_f, o_b, hnorm, sog, wout, gpost, mod, ncb, lc, latent_only):
    bsz, ltot, d = xc.shape
    tm = ROW_TILE
    d_conv = glu.shape[-1]
    d_hgrn = o_f.shape[-1]
    first = ncb if latent_only else 0
    rows, full2, modspec = _row_specs(tm, ncb, first)
    prev, nxt = _halo_specs(tm, CONV_HALO, d_conv, ltot, first)
    conv_span = (CONV_HALO + CONV_WIDTH // 2) // SUBLANES * SUBLANES
    assert SUBLANES - 1 + tm + conv_span <= tm + 2 * CONV_HALO
    kern = functools.partial(_outproj_odd_kernel, tm=tm, ncb=ncb, lc=lc, ll=ltot - lc, d_conv=d_conv, first=first)
    return pl.pallas_call(
        kern,
        grid=(bsz, ltot // tm - first),
        in_specs=[rows(d), rows(d_conv), prev, nxt, rows(d_conv), full2(*cw.shape), full2(1, d_conv),
                  full2(1, d_conv), full2(1, d_conv), rows(d_hgrn), rows(d_hgrn), full2(1, d_hgrn),
                  rows(d_hgrn), full2(*wout.shape), full2(1, d), modspec(d)],
        out_specs=pl.BlockSpec((1, tm, d), lambda b, i: (b, i, 0)),
        out_shape=jax.ShapeDtypeStruct((bsz, ltot - first * tm, d), F32),
        scratch_shapes=[pltpu.VMEM((tm + 2 * CONV_HALO, d_conv), F32),
                        pltpu.VMEM((SUBLANES, tm + conv_span, d_conv), F32)],
        name="outproj_odd",
        compiler_params=_cparams(("parallel", "parallel")),
    )(xc, glu, glu, glu, scg, cw, cb.reshape(1, d_conv), lng.reshape(1, d_conv), lnb.reshape(1, d_conv),
      o_f, o_b, hnorm.reshape(1, d_hgrn), sog, wout, gpost.reshape(1, d), mod)


def _rope_tables(lc, ll):
    t = jnp.arange(ll)
    inv = ROPE_BASE ** (-jnp.arange(ROPE_FREQS, dtype=F32) / ROPE_FREQS)
    ang = jnp.stack([t // GRID_W, t % GRID_W], axis=-1).astype(F32)[:, :, None] * inv
    cos = jnp.cos(ang)
    sin = jnp.sin(ang)
    cos64 = jnp.stack([cos, cos], axis=2).reshape(ll, DIFF_HEAD)
    sin64 = jnp.stack([-sin, sin], axis=2).reshape(ll, DIFF_HEAD)
    cos_t = jnp.concatenate([jnp.ones((lc, DIFF_HEAD), F32), cos64], axis=0)
    sin_t = jnp.concatenate([jnp.zeros((lc, DIFF_HEAD), F32), sin64], axis=0)
    return jnp.tile(cos_t, (1, LANES // DIFF_HEAD)), jnp.tile(sin_t, (1, LANES // DIFF_HEAD))


def _block_diag(w):
    g, a, b = w.shape
    out = jnp.zeros((g * a, g * b), w.dtype)
    for j in range(g):
        out = out.at[j * a:(j + 1) * a, j * b:(j + 1) * b].set(w[j])
    return out


def kernel(x, c, ctx, c_ctx, ada_w, ada_b, norm_pre, norm_post, w_in_even, w_out_even, pool_w, pool_scale,
           diff_lambda, diff_subln, w_in_odd, w_out_odd, conv_w, conv_b, conv_ln_g, conv_ln_b, hgrn_norm,
           hgrn_lb):
    bsz, ll, d = x.shape
    lc = ctx.shape[1]
    depth = ada_w.shape[0]
    d_pool = pool_scale.shape[-1]
    d_diff = w_out_even.shape[1] - d_pool
    d_conv = conv_b.shape[-1]
    d_hgrn = hgrn_norm.shape[-1]
    assert lc % ROW_TILE == 0 and ll % ROW_TILE == 0 and bsz + 1 <= SUBLANES
    ncb = lc // ROW_TILE

    cvec = jnp.zeros((SUBLANES, d), F32).at[:bsz].set(c).at[bsz].set(c_ctx)
    ada = _ada_call(cvec, ada_w, ada_b).reshape(depth, SUBLANES, 3, d)
    xc = jnp.concatenate([ctx, x], axis=1)
    cos, sin = _rope_tables(lc, ll)

    for l in range(depth):
        mod = jnp.stack([jnp.broadcast_to(ada[l, bsz], (bsz, 3, d)), ada[l, :bsz]], axis=1)
        j = l // 2
        if l % 2 == 0:
            lam_init = 0.8 - 0.6 * math.exp(-0.3 * l)
            wb = w_in_even[j].astype(BF16)
            q0, k0, v0, g0 = 2 * d_pool, 2 * d_pool + d_diff, 2 * d_pool + 2 * d_diff, 2 * d_pool + 3 * d_diff
            w_rows = jnp.concatenate([wb[:, :q0], wb[:, k0:v0]], axis=1)
            w_cols_t = jnp.concatenate([wb[:, q0:k0], wb[:, v0:g0], wb[:, g0:]], axis=1).T
            u, sga, k, qt, vt, sgbt = _inproj_even(xc, norm_pre[l], mod, w_rows, w_cols_t, cos, sin, ncb,
                                                   d_pool, d_diff)
            ybt = _attention(qt, k, vt, sgbt, diff_lambda[j], diff_subln[j], lc, lam_init)
            xc = _outproj_even(xc, u, sga, ybt, _block_diag(pool_w[j]).astype(BF16), pool_scale[j],
                               w_out_even[j].astype(BF16), norm_post[l], mod, ncb, lc)
        else:
            glu, scg, qh, b_f, k_f, b_b, k_b, iv, sog = _inproj_odd(xc, norm_pre[l], mod, w_in_odd[j].astype(BF16),
                                                                    hgrn_lb, l, ncb, d_conv, d_hgrn)
            o_f, o_b = _hgrn(qh, b_f, k_f, b_b, k_b, iv, lc)
            xc = _outproj_odd(xc, glu, scg, conv_w[j], conv_b[j], conv_ln_g[j], conv_ln_b[j], o_f, o_b,
                              hgrn_norm[j], sog, w_out_odd[j].astype(BF16), norm_post[l], mod, ncb, lc,
                              latent_only=l == depth - 1)
    return xc if depth % 2 == 0 else xc[:, lc:]
```

```python
import functools
import math

import jax
import jax.numpy as jnp
from jax import lax
from jax.experimental import pallas as pl
from jax.experimental.pallas import tpu as pltpu

F32 = jnp.float32
BF16 = jnp.bfloat16
EPS = 1e-6

GRID_W = 64
POOL_WINDOWS = (2, 4, 8, 16)
POOL_HALO = max(POOL_WINDOWS) // 2
DIFF_HEAD = 64
DIFF_V = 2 * DIFF_HEAD
DIFF_SCALE = DIFF_HEAD ** -0.5
ROPE_BASE = 10000.0
ROPE_FREQS = DIFF_HEAD // 4
CONV_WIDTH = 31
HGRN_HEAD = 128
LOG2E = 1.4426950408889634

LANES = 128
SUBLANES = 8
VMEM_LIMIT = 48 * 1024 * 1024

ROW_TILE = 256
CONV_HALO = 16
Q_TILE = 128
Q_SUBTILES = 2
ATTN_HEADS_PER_STEP = 2
KV_TILE = 512
SCORE_SLOTS = 3
ONES_ROWS = 16
HGRN_CHUNK = 64
HGRN_DIAG = 2
HGRN_INTERLEAVE = 16


def _cparams(sem):
    return pltpu.CompilerParams(dimension_semantics=sem, vmem_limit_bytes=VMEM_LIMIT)


def _silu(x):
    return x * jax.nn.sigmoid(x)


def _ada_kernel(c_ref, w_ref, b_ref, o_ref):
    c = c_ref[...]
    o_ref[0] = jnp.dot(_silu(c), w_ref[0], preferred_element_type=F32,
                       precision=lax.Precision.HIGHEST) + b_ref[0]


def _ada_call(cvec, ada_w, ada_b):
    depth, d, d3 = ada_w.shape
    nj = d3 // d
    return pl.pallas_call(
        _ada_kernel,
        grid=(depth, nj),
        in_specs=[pl.BlockSpec((SUBLANES, d), lambda l, j: (0, 0)),
                  pl.BlockSpec((1, d, d), lambda l, j: (l, 0, j)),
                  pl.BlockSpec((1, 1, d), lambda l, j: (l, 0, j))],
        out_specs=pl.BlockSpec((1, SUBLANES, d), lambda l, j: (l, 0, j)),
        out_shape=jax.ShapeDtypeStruct((depth, SUBLANES, d3), F32),
        name="ada_params",
        compiler_params=_cparams(("parallel", "parallel")),
    )(cvec, ada_w, ada_b.reshape(depth, 1, d3))


def _prenorm(x_ref, g_ref, mod_ref):
    x = x_ref[0]
    mod = mod_ref[0, 0]
    ms = jnp.mean(x * x, axis=-1, keepdims=True)
    h = x * lax.rsqrt(ms + EPS) * g_ref[...] * (1.0 + mod[1:2]) + mod[0:1]
    return h.astype(BF16)


def _pipelined(stages):
    nxt = stages[0][0]()
    for g, (_, consume) in enumerate(stages):
        cur = nxt
        if g + 1 < len(stages):
            nxt = stages[g + 1][0]()
        consume(cur)


def _postnorm_residual(x_ref, y, gpost_ref, mod_ref, o_ref):
    mod = mod_ref[0, 0]
    ms = jnp.mean(y * y, axis=-1, keepdims=True)
    yn = y * lax.rsqrt(ms + EPS) * gpost_ref[...]
    o_ref[0] = x_ref[0] + mod[2:3] * yn


def _row_specs(tm, ncb, first=0):
    def rows(width):
        return pl.BlockSpec((1, tm, width), lambda b, i: (b, i + first, 0))

    def full2(a, bdim):
        return pl.BlockSpec((a, bdim), lambda b, i: (0, 0))

    def mod(d):
        return pl.BlockSpec((1, 1, 3, d), lambda b, i: (b, jnp.where(i + first < ncb, 0, 1), 0, 0))

    return rows, full2, mod


def _inproj_even_kernel(x_ref, g_ref, mod_ref, w_ref, wt_ref, cos_ref, sin_ref, cost_ref, sint_ref,
                        u_ref, sga_ref, k_ref, qt_ref, vt_ref, sgbt_ref, *, d_pool, d_diff):
    hb = _prenorm(x_ref, g_ref, mod_ref)
    wide = 2 * LANES

    def proj(c0, width):
        return lambda: jnp.dot(hb, w_ref[:, c0:c0 + width], preferred_element_type=F32)

    def proj_t(r0, height):
        return lambda: lax.dot_general(wt_ref[r0:r0 + height, :], hb, (((1,), (1,)), ((), ())),
                                       preferred_element_type=F32)

    def pool(r):
        u_ref[0] = r[:, :d_pool]
        sga_ref[0] = _silu(r[:, d_pool:])

    lane = lax.broadcasted_iota(jnp.int32, cos_ref.shape, 1)
    first_half = (lane & ROPE_FREQS) == 0

    def keys(c0):
        def put(r):
            for s in range(wide // LANES):
                t = r[:, s * LANES:(s + 1) * LANES]
                partner = jnp.where(first_half, pltpu.roll(t, LANES - ROPE_FREQS, 1), pltpu.roll(t, ROPE_FREQS, 1))
                k_ref[0, :, c0 + s * LANES:c0 + (s + 1) * LANES] = (t * cos_ref[...] + partner * sin_ref[...]).astype(BF16)
        return put

    def gate(r0):
        def put(r):
            sgbt_ref[0, r0:r0 + wide, :] = _silu(r).astype(BF16)
        return put

    def queries(r0):
        f = ROPE_FREQS

        def put(r):
            for s in range(wide // LANES):
                t = r[s * LANES:(s + 1) * LANES]
                partner = jnp.concatenate([t[(blk ^ 1) * f:((blk ^ 1) + 1) * f] for blk in range(LANES // f)], axis=0)
                qt_ref[0, r0 + s * LANES:r0 + (s + 1) * LANES, :] = (
                    (t * cost_ref[...] + partner * sint_ref[...]) * (DIFF_SCALE * LOG2E)).astype(BF16)
        return put

    def values(r0):
        def put(r):
            vt_ref[0, r0:r0 + wide, :] = r.astype(BF16)
        return put

    k0 = 2 * d_pool
    stages = [(proj(0, 2 * d_pool), pool)]
    stages += [(proj(k0 + c, wide), keys(c)) for c in range(0, d_diff, wide)]
    stages += [(proj_t(r, wide), queries(r)) for r in range(0, d_diff, wide)]
    stages += [(proj_t(d_diff + r, wide), values(r)) for r in range(0, d_diff, wide)]
    stages += [(proj_t(2 * d_diff + r, wide), gate(r)) for r in range(0, d_diff, wide)]
    _pipelined(stages)


def _inproj_even(xc, g, mod, w, wt, cos, sin, ncb, d_pool, d_diff):
    bsz, ltot, d = xc.shape
    tm = ROW_TILE
    rows, full2, modspec = _row_specs(tm, ncb)
    tab = pl.BlockSpec((tm, LANES), lambda b, i: (i, 0))
    tab_t = pl.BlockSpec((LANES, tm), lambda b, i: (0, i))
    cols_t = pl.BlockSpec((1, d_diff, tm), lambda b, i: (b, 0, i))
    shp = lambda width, dt: jax.ShapeDtypeStruct((bsz, ltot, width), dt)
    shp_t = jax.ShapeDtypeStruct((bsz, d_diff, ltot), BF16)
    return pl.pallas_call(
        functools.partial(_inproj_even_kernel, d_pool=d_pool, d_diff=d_diff),
        grid=(bsz, ltot // tm),
        in_specs=[rows(d), full2(1, d), modspec(d), full2(*w.shape), full2(*wt.shape), tab, tab, tab_t, tab_t],
        out_specs=[rows(d_pool), rows(d_pool), rows(d_diff), cols_t, cols_t, cols_t],
        out_shape=[shp(d_pool, F32), shp(d_pool, F32), shp(d_diff, BF16), shp_t, shp_t, shp_t],
        name="inproj_even",
        compiler_params=_cparams(("parallel", "parallel")),
    )(xc, g.reshape(1, d), mod, w, wt, cos, sin, cos.T, sin.T)


def _attn_kernel(qt_ref, k_ref, vt_ref, sgbt_ref, lam_ref, sub_ref, o_ref, acc_sc, st_sc, mx_sc,
                 *, nhead, nsub, nq, n_ctx_q, ctx_chunks, lat_chunks, lam_init):
    i = pl.program_id(2)
    units = [(e, s) for e in range(nhead) for s in range(nsub)]
    hd = lambda e: slice(e * DIFF_V, (e + 1) * DIFF_V)
    qs = lambda s: slice(s * nq, (s + 1) * nq)
    row = lax.broadcasted_iota(jnp.int32, (DIFF_V, nq), 0)
    ws = []
    for e, s in units:
        qt = qt_ref[0, hd(e), qs(s)]
        zero = jnp.zeros_like(qt)
        ws.append(jnp.concatenate([jnp.where(row < DIFF_HEAD, qt, zero), jnp.where(row >= DIFF_HEAD, qt, zero)],
                                  axis=1))

    ones = jnp.ones((ONES_ROWS, st_sc.shape[2]), BF16)

    def scores(chunk, slot):
        off, size = chunk
        kcs = [k_ref[0, off:off + size, hd(e)] for e in range(nhead)]
        for c, (e, _) in enumerate(units):
            st = jnp.dot(kcs[e], ws[c], preferred_element_type=F32)
            st_sc[slot, c, 0:size] = st
            mx_sc[slot, c] = jnp.max(st, axis=0, keepdims=True)

    def finish(pending):
        for c, (alpha, pv) in enumerate(pending):
            acc_sc[c] = pv if alpha is None else alpha * acc_sc[c] + pv

    def absorb(chunk, slot, ms, pending=None, ahead=None):
        off, size = chunk
        nun = len(units)
        new = [mx_sc[slot, c] if ms is None else jnp.maximum(ms[c], mx_sc[slot, c]) for c in range(nun)]
        alphas = [None if ms is None else jnp.exp2(ms[c] - new[c]) for c in range(nun)]
        if ahead is not None:
            ahead()
        if pending is not None:
            finish(pending)
        vtes = [jnp.concatenate([vt_ref[0, hd(e), off:off + size], ones[:, :size]], axis=0)
                for e in range(nhead)]
        pvs = [jnp.dot(vtes[e], jnp.exp2(st_sc[slot, c, 0:size] - new[c]).astype(BF16),
                       preferred_element_type=F32) for c, (e, _) in enumerate(units)]
        return tuple(new), tuple(zip(alphas, pvs))

    def sweep(chunks):
        slots = SCORE_SLOTS
        for n in range(min(2, len(chunks))):
            scores(chunks[n], n)
        ms, pending = None, None
        for n, chunk in enumerate(chunks):
            ahead = (lambda c=chunks[n + 2], s=(n + 2) % slots: scores(c, s)) if n + 2 < len(chunks) else None
            ms, pending = absorb(chunk, n % slots, ms, pending, ahead)
        finish(pending)

    @pl.when(i < n_ctx_q)
    def _():
        sweep(ctx_chunks)

    @pl.when(i >= n_ctx_q)
    def _():
        sweep(ctx_chunks + lat_chunks)

    lp = lam_ref[...]
    lam = (jnp.exp(jnp.sum(lp[0:1] * lp[1:2], axis=1, keepdims=True))
           - jnp.exp(jnp.sum(lp[2:3] * lp[3:4], axis=1, keepdims=True)) + lam_init)
    accs = [acc_sc[c] for c in range(len(units))]
    o_alls = [acc[:DIFF_V] / acc[DIFF_V:DIFF_V + 1] for acc in accs]
    os = [o_all[:, :nq] - lam * o_all[:, nq:] for o_all in o_alls]
    mss = [jnp.mean(o * o, axis=0, keepdims=True) for o in os]
    gain = sub_ref[...] * (1.0 - lam_init)
    for c, (e, s) in enumerate(units):
        on = os[c] * lax.rsqrt(mss[c] + EPS) * gain
        o_ref[0, hd(e), qs(s)] = (on * sgbt_ref[0, hd(e), qs(s)].astype(F32)).astype(BF16)


def _attention(qt, k, vt, sgbt, lam_p, subln, lc, lam_init):
    bsz, ltot, d_diff = k.shape
    heads = d_diff // DIFF_V
    nq, nsub, tk, nhead = Q_TILE, Q_SUBTILES, KV_TILE, ATTN_HEADS_PER_STEP
    nqs = nq * nsub
    nun = nhead * nsub
    width = nhead * DIFF_V
    assert lc % nqs == 0 and ltot % nqs == 0 and lc % LANES == 0 and (ltot - lc) % tk == 0 and heads % nhead == 0
    ctx_chunks = tuple((o, min(tk, lc - o)) for o in range(0, lc, tk))
    lat_chunks = tuple((o, tk) for o in range(lc, ltot, tk))
    colspec = pl.BlockSpec((1, width, nqs), lambda b, h, i: (b, h, i))
    kern = functools.partial(_attn_kernel, nhead=nhead, nsub=nsub, nq=nq, n_ctx_q=lc // nqs, ctx_chunks=ctx_chunks,
                             lat_chunks=lat_chunks, lam_init=lam_init)
    return pl.pallas_call(
        kern,
        grid=(bsz, heads // nhead, ltot // nqs),
        in_specs=[colspec,
                  pl.BlockSpec((1, ltot, width), lambda b, h, i: (b, 0, h)),
                  pl.BlockSpec((1, width, ltot), lambda b, h, i: (b, h, 0)),
                  colspec,
                  pl.BlockSpec(lam_p.shape, lambda b, h, i: (0, 0)),
                  pl.BlockSpec((DIFF_V, 1), lambda b, h, i: (0, 0))],
        out_specs=colspec,
        out_shape=jax.ShapeDtypeStruct((bsz, d_diff, ltot), BF16),
        scratch_shapes=[pltpu.VMEM((nun, DIFF_V + ONES_ROWS, 2 * nq), F32),
                        pltpu.VMEM((SCORE_SLOTS, nun, tk, 2 * nq), F32),
                        pltpu.VMEM((SCORE_SLOTS, nun, 1, 2 * nq), F32)],
        name="diff_attention",
        compiler_params=_cparams(("parallel", "parallel", "arbitrary")),
    )(qt, k, vt, sgbt, lam_p, subln.reshape(DIFF_V, 1))


def _seq_position(i, tm, ncb, lc, ll):
    nblk = ncb + ll // tm
    is_ctx = i < ncb
    t0 = jnp.where(is_ctx, i, i - ncb) * tm
    lseq = jnp.where(is_ctx, lc, ll)
    has_prev = jnp.logical_and(i != 0, i != ncb)
    has_next = jnp.logical_and(i != ncb - 1, i != nblk - 1)
    return t0, lseq, has_prev, has_next


def _outproj_even_kernel(x_ref, u_ref, up_ref, un_ref, sga_ref, ybt_ref, wpool_ref, pscale_ref, wout_ref,
                         gpost_ref, mod_ref, o_ref, *, tm, ncb, lc, ll, d_pool):
    i = pl.program_id(1)
    t0, lseq, has_prev, has_next = _seq_position(i, tm, ncb, lc, ll)
    y_attn = lax.dot_general(ybt_ref[0], wout_ref[d_pool:, :], (((0,), (0,)), ((), ())),
                             preferred_element_type=F32)
    u = u_ref[0]
    z = jnp.concatenate([jnp.where(has_prev, up_ref[0], 0.0), u, jnp.where(has_next, un_ref[0], 0.0)], axis=0)
    group_width = d_pool // len(POOL_WINDOWS)
    assert group_width & (group_width - 1) == 0
    grp = lax.broadcasted_iota(jnp.int32, u.shape, 1) >> int(math.log2(group_width))
    p, w, s, half = z, 1, None, None
    for g, win in enumerate(POOL_WINDOWS):
        while w < win:
            p = p[:-w] + p[w:]
            w *= 2
        assert w == win, "pooling windows must be powers of two in increasing order"
        start = POOL_HALO - win // 2
        s = p[start:start + tm] if s is None else jnp.where(grp == g, p[start:start + tm], s)
        half = jnp.full(u.shape, win // 2, jnp.int32) if half is None else jnp.where(grp == g, win // 2, half)
    t = lax.broadcasted_iota(jnp.int32, u.shape, 0) + t0
    cnt = jnp.minimum(t + half - 1, lseq - 1) + 1 - jnp.maximum(t - half, 0)
    dpool = s / cnt.astype(F32) - u
    ya = jnp.dot(dpool.astype(BF16), wpool_ref[...], preferred_element_type=F32) * pscale_ref[...] * sga_ref[0]
    y = jnp.dot(ya.astype(BF16), wout_ref[0:d_pool, :], preferred_element_type=F32) + y_attn
    _postnorm_residual(x_ref, y, gpost_ref, mod_ref, o_ref)


def _halo_specs(tm, halo, width, ltot, first=0):
    r = tm // halo
    prev = pl.BlockSpec((1, halo, width), lambda b, i: (b, jnp.maximum((i + first) * r - 1, 0), 0))
    nxt = pl.BlockSpec((1, halo, width),
                       lambda b, i: (b, jnp.minimum((i + first + 1) * r, ltot // halo - 1), 0))
    return prev, nxt


def _outproj_even(xc, u, sga, ybt, wpool, pscale, wout, gpost, mod, ncb, lc):
    bsz, ltot, d = xc.shape
    tm = ROW_TILE
    d_pool = u.shape[-1]
    rows, full2, modspec = _row_specs(tm, ncb)
    prev, nxt = _halo_specs(tm, POOL_HALO, d_pool, ltot)
    kern = functools.partial(_outproj_even_kernel, tm=tm, ncb=ncb, lc=lc, ll=ltot - lc, d_pool=d_pool)
    return pl.pallas_call(
        kern,
        grid=(bsz, ltot // tm),
        in_specs=[rows(d), rows(d_pool), prev, nxt, rows(d_pool),
                  pl.BlockSpec((1, ybt.shape[1], tm), lambda b, i: (b, 0, i)),
                  full2(*wpool.shape), full2(1, d_pool), full2(*wout.shape), full2(1, d), modspec(d)],
        out_specs=rows(d),
        out_shape=jax.ShapeDtypeStruct(xc.shape, F32),
        name="outproj_even",
        compiler_params=_cparams(("parallel", "parallel")),
    )(xc, u, u, u, sga, ybt, wpool, pscale.reshape(1, d_pool), wout, gpost.reshape(1, d), mod)


def _hgrn_lower_bounds(raw, layer):
    rows = [raw[:, j, :] for j in range(raw.shape[1])]
    mx = functools.reduce(jnp.maximum, rows)
    ex = [jnp.exp(r - mx) for r in rows]
    den = functools.reduce(lambda a, bb: a + bb, ex)
    if layer == 0:
        return jnp.zeros_like(mx)
    return functools.reduce(lambda a, bb: a + bb, [e / den for e in ex[1:layer + 1]])


def _chunk_tri(n, c, rev):
    ri = lax.broadcasted_iota(jnp.int32, (n, n), 0)
    ci = lax.broadcasted_iota(jnp.int32, (n, n), 1)
    shift = int(math.log2(c))
    keep = jnp.logical_and((ri >> shift) == (ci >> shift), (ci >= ri) if rev else (ci <= ri))
    return jnp.where(keep, 1.0, 0.0).astype(BF16)


def _cumsum_rows(x, tri):
    hi = x.astype(BF16)
    r1 = x - hi.astype(F32)
    mid = r1.astype(BF16)
    lo = (r1 - mid.astype(F32)).astype(BF16)
    n = x.shape[1]
    r = jnp.dot(tri, jnp.concatenate([hi, mid, lo], axis=1), preferred_element_type=F32)
    return r[:, :n] + r[:, n:2 * n] + r[:, 2 * n:]


def _inproj_odd_kernel(x_ref, g_ref, mod_ref, w_ref, lbraw_ref, glu_ref, scg_ref, qh_ref, bf_ref, kf_ref, bb_ref,
                       kb_ref, iv_ref, sog_ref, *, d_conv, d_hgrn, layer):
    hb = _prenorm(x_ref, g_ref, mod_ref)
    tm = hb.shape[0]
    lbs = _hgrn_lower_bounds(lbraw_ref[...], layer)

    def glu(r):
        glu_ref[0] = r[:, :d_conv] * jax.nn.sigmoid(r[:, d_conv:])

    def silu_to(ref):
        def put(r):
            ref[0] = _silu(r).astype(BF16)
        return put

    def gate(dirn, b_ref, k_ref):
        def put(r):
            lb = lbs[dirn:dirn + 1]
            f = lb + (1.0 - lb) * jax.nn.sigmoid(r)
            k_ref[0] = (1.0 - f).astype(BF16)
            b_ref[0] = _cumsum_rows(jnp.log2(f), _chunk_tri(tm, HGRN_CHUNK, bool(dirn)))
        return put

    def value(r):
        iv_ref[0] = r.astype(BF16)

    def proj(c0, width):
        return lambda: jnp.dot(hb, w_ref[:, c0:c0 + width], preferred_element_type=F32)

    c0 = 3 * d_conv
    _pipelined([(proj(0, 2 * d_conv), glu), (proj(2 * d_conv, d_conv), silu_to(scg_ref)),
                (proj(c0, d_hgrn), silu_to(qh_ref)), (proj(c0 + d_hgrn, d_hgrn), gate(0, bf_ref, kf_ref)),
                (proj(c0 + 2 * d_hgrn, d_hgrn), gate(1, bb_ref, kb_ref)),
                (proj(c0 + 3 * d_hgrn, d_hgrn), value), (proj(c0 + 4 * d_hgrn, d_hgrn), silu_to(sog_ref))])


def _inproj_odd(xc, g, mod, w, hgrn_lb, layer, ncb, d_conv, d_hgrn):
    bsz, ltot, d = xc.shape
    tm = ROW_TILE
    assert tm % HGRN_CHUNK == 0
    rows, full2, modspec = _row_specs(tm, ncb)
    shp = lambda width, dt: jax.ShapeDtypeStruct((bsz, ltot, width), dt)
    return pl.pallas_call(
        functools.partial(_inproj_odd_kernel, d_conv=d_conv, d_hgrn=d_hgrn, layer=layer),
        grid=(bsz, ltot // tm),
        in_specs=[rows(d), full2(1, d), modspec(d), full2(*w.shape),
                  pl.BlockSpec(hgrn_lb.shape, lambda b, i: (0, 0, 0))],
        out_specs=[rows(d_conv), rows(d_conv), rows(d_hgrn), rows(d_hgrn), rows(d_hgrn), rows(d_hgrn),
                   rows(d_hgrn), rows(d_hgrn), rows(d_hgrn)],
        out_shape=[shp(d_conv, F32), shp(d_conv, BF16), shp(d_hgrn, BF16), shp(d_hgrn, F32), shp(d_hgrn, BF16),
                   shp(d_hgrn, F32), shp(d_hgrn, BF16), shp(d_hgrn, BF16), shp(d_hgrn, BF16)],
        name="inproj_odd",
        compiler_params=_cparams(("parallel", "parallel")),
    )(xc, g.reshape(1, d), mod, w, hgrn_lb)


def _hgrn_masks(c, rev):
    ri = lax.broadcasted_iota(jnp.int32, (c, c), 0)
    ci = lax.broadcasted_iota(jnp.int32, (c, c), 1)
    masks = {}
    w = c // 2
    while w >= HGRN_DIAG:
        shift = int(math.log2(2 * w))
        same_pair = (ri >> shift) == (ci >> shift)
        r_odd, c_odd = (ri & w) != 0, (ci & w) != 0
        owns = jnp.logical_and(c_odd, jnp.logical_not(r_odd)) if rev else jnp.logical_and(r_odd, jnp.logical_not(c_odd))
        masks[w] = jnp.logical_and(same_pair, owns)
        w //= 2
    return masks


class _Chain:
    def __init__(self, q_ref, b_ref, k_ref, v_ref, o_ref, idx, st_ref, rev, masks):
        self.refs = (q_ref, b_ref, k_ref, v_ref, o_ref, idx, st_ref)
        self.rev, self.masks = rev, masks

    def load(self):
        q_ref, b_ref, k_ref, v_ref, _, idx, _ = self.refs
        self.qf = q_ref[idx].astype(F32)
        self.vb = v_ref[idx]
        self.kk = k_ref[idx].astype(F32)
        self.b = b_ref[idx]

    def products(self):
        qf, kk, b, rev = self.qf, self.kk, self.b, self.rev
        c, n = qf.shape
        self.st = self.refs[6][...]
        self.inter = lax.dot_general((qf * jnp.exp2(b)).astype(BF16), self.st.astype(BF16),
                                     (((1,), (1,)), ((), ())), preferred_element_type=F32)
        self.levels = []
        b3 = b.reshape(c // SUBLANES, SUBLANES, n)
        sub = lax.broadcasted_iota(jnp.int32, (1, SUBLANES, 1), 1)
        w = c // 2
        while w >= HGRN_DIAG:
            if 2 * w >= SUBLANES:
                refs = []
                for p in range(c // (2 * w)):
                    r0 = p * 2 * w + (w if rev else w - 1)
                    refs.append(jnp.broadcast_to(b[r0:r0 + 1, :], (2 * w, n)))
                ref = refs[0] if len(refs) == 1 else jnp.concatenate(refs, axis=0)
            else:
                ref3 = None
                for p in range(SUBLANES // (2 * w)):
                    r0 = p * 2 * w + (w if rev else w - 1)
                    piece = jnp.broadcast_to(b3[:, r0:r0 + 1, :], b3.shape)
                    ref3 = piece if ref3 is None else jnp.where(sub >= p * 2 * w, piece, ref3)
                ref = ref3.reshape(c, n)
            qs = jnp.exp2(jnp.minimum(b - ref, 0.0)) * qf
            ks = jnp.exp2(jnp.minimum(ref - b, 0.0)) * kk
            self.levels.append((w, lax.dot_general(qs.astype(BF16), ks.astype(BF16), (((1,), (1,)), ((), ())),
                                                   preferred_element_type=F32)))
            w //= 2
        self.bl = b[0:1, :] if rev else b[c - 1:c, :]
        kd = (kk * jnp.exp2(self.bl - b)).astype(BF16)
        self.st_add = lax.dot_general(self.vb, kd, (((0,), (0,)), ((), ())), preferred_element_type=F32)

    def diagonal(self):
        c, n = self.qf.shape
        blocks = lambda x: x.reshape(c // SUBLANES, SUBLANES, n)
        b3, k3, v3, q3 = blocks(self.b), blocks(self.kk), blocks(self.vb.astype(F32)), blocks(self.qf)
        sub = lax.broadcasted_iota(jnp.int32, (1, SUBLANES, 1), 1) & (HGRN_DIAG - 1)
        diag = jnp.sum(q3 * k3, axis=2, keepdims=True) * v3
        for d in range(1, HGRN_DIAG):
            sh = SUBLANES - d if self.rev else d
            valid = (sub + d <= HGRN_DIAG - 1) if self.rev else (sub >= d)
            e = jnp.exp2(jnp.where(valid, b3 - pltpu.roll(b3, sh, 1), 0.0))
            a = jnp.sum(q3 * e * pltpu.roll(k3, sh, 1), axis=2, keepdims=True)
            diag = diag + jnp.where(valid, a, 0.0) * pltpu.roll(v3, sh, 1)
        self.diag = diag.reshape(c, n)

    def intra(self):
        att = functools.reduce(lambda x, y: x + y, [jnp.where(self.masks[w], a, 0.0) for w, a in self.levels])
        self.intra_out = jnp.dot(att.astype(BF16), self.vb, preferred_element_type=F32)
        self.refs[6][...] = self.st * jnp.exp2(self.bl) + self.st_add

    def finish(self):
        self.refs[4][self.refs[5]] = self.inter + self.intra_out + self.diag


def _hgrn_kernel(qf_ref, bf_ref, kf_ref, vf_ref, qb_ref, bb_ref, kb_ref, vb_ref, of_ref, ob_ref, st_sc,
                 *, bsz, heads):
    s = pl.program_id(0)

    @pl.when(s == 0)
    def _():
        st_sc[...] = jnp.zeros(st_sc.shape, F32)

    c = qf_ref.shape[1]
    fwd, bwd = _hgrn_masks(c, False), _hgrn_masks(c, True)
    chains = []
    for bi in range(bsz):
        for h in range(heads):
            sl = slice(h * HGRN_HEAD, (h + 1) * HGRN_HEAD)
            idx = (bi, slice(None), sl)
            chains.append(_Chain(qf_ref, bf_ref, kf_ref, vf_ref, of_ref, idx, st_sc.at[bi, 0, h], False, fwd))
            chains.append(_Chain(qb_ref, bb_ref, kb_ref, vb_ref, ob_ref, idx, st_sc.at[bi, 1, h], True, bwd))
    for g0 in range(0, len(chains), HGRN_INTERLEAVE):
        group = chains[g0:g0 + HGRN_INTERLEAVE]
        for stage in (_Chain.load, _Chain.products, _Chain.intra, _Chain.diagonal, _Chain.finish):
            for ch in group:
                stage(ch)


def _hgrn(qh, b_f, k_f, b_b, k_b, iv, lc):
    bsz, ltot, d_hgrn = qh.shape
    heads = d_hgrn // HGRN_HEAD
    c = HGRN_CHUNK
    assert lc % c == 0 and ltot % c == 0
    ncc, ntot = lc // c, ltot // c

    def bwd(s):
        return jnp.where(s < ncc, ncc - 1 - s, ntot - 1 + ncc - s)

    fspec = pl.BlockSpec((bsz, c, d_hgrn), lambda s: (0, s, 0))
    bspec = pl.BlockSpec((bsz, c, d_hgrn), lambda s: (0, bwd(s), 0))
    kern = functools.partial(_hgrn_kernel, bsz=bsz, heads=heads)
    return pl.pallas_call(
        kern,
        grid=(ntot,),
        in_specs=[fspec, fspec, fspec, fspec, bspec, bspec, bspec, bspec],
        out_specs=[fspec, bspec],
        out_shape=[jax.ShapeDtypeStruct((bsz, ltot, d_hgrn), F32)] * 2,
        scratch_shapes=[pltpu.VMEM((bsz, 2, heads, HGRN_HEAD, HGRN_HEAD), F32)],
        name="hgrn_scan",
        compiler_params=_cparams(("arbitrary",)),
    )(qh, b_f, k_f, iv, qh, b_b, k_b, iv)


def _outproj_odd_kernel(x_ref, glu_ref, gp_ref, gn_ref, scg_ref, cw_ref, cb_ref, lng_ref, lnb_ref,
                        of_ref, ob_ref, hn_ref, sog_ref, wout_ref, gpost_ref, mod_ref, o_ref, z_sc, zs_sc,
                        *, tm, ncb, lc, ll, d_conv, first):
    i = pl.program_id(1) + first
    _, _, has_prev, has_next = _seq_position(i, tm, ncb, lc, ll)
    h = CONV_HALO
    z_sc[0:h, :] = jnp.where(has_prev, gp_ref[0], 0.0)
    z_sc[h:h + tm, :] = glu_ref[0]
    z_sc[h + tm:, :] = jnp.where(has_next, gn_ref[0], 0.0)
    cw = cw_ref[...]
    base = h - CONV_WIDTH // 2
    span = zs_sc.shape[1]
    for r in range(SUBLANES):
        zs_sc[r] = z_sc[r:r + span, :]
    acc = jnp.zeros((tm, d_conv), F32)
    for j in range(CONV_WIDTH):
        r = (base + j) % SUBLANES
        a0 = base + j - r
        acc = acc + cw[j:j + 1, :] * zs_sc[r, a0:a0 + tm, :]
    zc = acc + cb_ref[...]
    mu = jnp.mean(zc, axis=-1, keepdims=True)
    zc = zc - mu
    var = jnp.mean(zc * zc, axis=-1, keepdims=True)
    zn = zc * lax.rsqrt(var + EPS) * lng_ref[...] + lnb_ref[...]
    yc = _silu(zn) * scg_ref[0].astype(F32)
    o = of_ref[0] + ob_ref[0]
    hn = hn_ref[...]
    parts = []
    for hd in range(o.shape[1] // HGRN_HEAD):
        sl = slice(hd * HGRN_HEAD, (hd + 1) * HGRN_HEAD)
        oh = o[:, sl]
        ms = jnp.mean(oh * oh, axis=-1, keepdims=True)
        parts.append(oh * lax.rsqrt(ms + EPS) * hn[:, sl])
    yd = jnp.concatenate(parts, axis=1) * sog_ref[0].astype(F32)
    y = (jnp.dot(yc.astype(BF16), wout_ref[0:d_conv, :], preferred_element_type=F32)
         + jnp.dot(yd.astype(BF16), wout_ref[d_conv:, :], preferred_element_type=F32))
    _postnorm_residual(x_ref, y, gpost_ref, mod_ref, o_ref)


def _outproj_odd(xc, glu, scg, cw, cb, lng, lnb, o_f, o_b, hnorm, sog, wout, gpost, mod, ncb, lc, latent_only):
    bsz, ltot, d = xc.shape
    tm = ROW_TILE
    d_conv = glu.shape[-1]
    d_hgrn = o_f.shape[-1]
    first = ncb if latent_only else 0
    rows, full2, modspec = _row_specs(tm, ncb, first)
    prev, nxt = _halo_specs(tm, CONV_HALO, d_conv, ltot, first)
    conv_span = (CONV_HALO + CONV_WIDTH // 2) // SUBLANES * SUBLANES
    assert SUBLANES - 1 + tm + conv_span <= tm + 2 * CONV_HALO
    kern = functools.partial(_outproj_odd_kernel, tm=tm, ncb=ncb, lc=lc, ll=ltot - lc, d_conv=d_conv, first=first)
    return pl.pallas_call(
        kern,
        grid=(bsz, ltot // tm - first),
        in_specs=[rows(d), rows(d_conv), prev, nxt, rows(d_conv), full2(*cw.shape), full2(1, d_conv),
                  full2(1, d_conv), full2(1, d_conv), rows(d_hgrn), rows(d_hgrn), full2(1, d_hgrn),
                  rows(d_hgrn), full2(*wout.shape), full2(1, d), modspec(d)],
        out_specs=pl.BlockSpec((1, tm, d), lambda b, i: (b, i, 0)),
        out_shape=jax.ShapeDtypeStruct((bsz, ltot - first * tm, d), F32),
        scratch_shapes=[pltpu.VMEM((tm + 2 * CONV_HALO, d_conv), F32),
                        pltpu.VMEM((SUBLANES, tm + conv_span, d_conv), F32)],
        name="outproj_odd",
        compiler_params=_cparams(("parallel", "parallel")),
    )(xc, glu, glu, glu, scg, cw, cb.reshape(1, d_conv), lng.reshape(1, d_conv), lnb.reshape(1, d_conv),
      o_f, o_b, hnorm.reshape(1, d_hgrn), sog, wout, gpost.reshape(1, d), mod)


def _rope_tables(lc, ll):
    t = jnp.arange(ll)
    inv = ROPE_BASE ** (-jnp.arange(ROPE_FREQS, dtype=F32) / ROPE_FREQS)
    ang = jnp.stack([t // GRID_W, t % GRID_W], axis=-1).astype(F32)[:, :, None] * inv
    cos = jnp.cos(ang)
    sin = jnp.sin(ang)
    cos64 = jnp.stack([cos, cos], axis=2).reshape(ll, DIFF_HEAD)
    sin64 = jnp.stack([-sin, sin], axis=2).reshape(ll, DIFF_HEAD)
    cos_t = jnp.concatenate([jnp.ones((lc, DIFF_HEAD), F32), cos64], axis=0)
    sin_t = jnp.concatenate([jnp.zeros((lc, DIFF_HEAD), F32), sin64], axis=0)
    return jnp.tile(cos_t, (1, LANES // DIFF_HEAD)), jnp.tile(sin_t, (1, LANES // DIFF_HEAD))


def _block_diag(w):
    g, a, b = w.shape
    out = jnp.zeros((g * a, g * b), w.dtype)
    for j in range(g):
        out = out.at[j * a:(j + 1) * a, j * b:(j + 1) * b].set(w[j])
    return out


def kernel(x, c, ctx, c_ctx, ada_w, ada_b, norm_pre, norm_post, w_in_even, w_out_even, pool_w, pool_scale,
           diff_lambda, diff_subln, w_in_odd, w_out_odd, conv_w, conv_b, conv_ln_g, conv_ln_b, hgrn_norm,
           hgrn_lb):
    bsz, ll, d = x.shape
    lc = ctx.shape[1]
    depth = ada_w.shape[0]
    d_pool = pool_scale.shape[-1]
    d_diff = w_out_even.shape[1] - d_pool
    d_conv = conv_b.shape[-1]
    d_hgrn = hgrn_norm.shape[-1]
    assert lc % ROW_TILE == 0 and ll % ROW_TILE == 0 and bsz + 1 <= SUBLANES
    ncb = lc // ROW_TILE

    cvec = jnp.zeros((SUBLANES, d), F32).at[:bsz].set(c).at[bsz].set(c_ctx)
    ada = _ada_call(cvec, ada_w, ada_b).reshape(depth, SUBLANES, 3, d)
    xc = jnp.concatenate([ctx, x], axis=1)
    cos, sin = _rope_tables(lc, ll)

    for l in range(depth):
        mod = jnp.stack([jnp.broadcast_to(ada[l, bsz], (bsz, 3, d)), ada[l, :bsz]], axis=1)
        j = l // 2
        if l % 2 == 0:
            lam_init = 0.8 - 0.6 * math.exp(-0.3 * l)
            wb = w_in_even[j].astype(BF16)
            q0, k0, v0, g0 = 2 * d_pool, 2 * d_pool + d_diff, 2 * d_pool + 2 * d_diff, 2 * d_pool + 3 * d_diff
            w_rows = jnp.concatenate([wb[:, :q0], wb[:, k0:v0]], axis=1)
            w_cols_t = jnp.concatenate([wb[:, q0:k0], wb[:, v0:g0], wb[:, g0:]], axis=1).T
            u, sga, k, qt, vt, sgbt = _inproj_even(xc, norm_pre[l], mod, w_rows, w_cols_t, cos, sin, ncb,
                                                   d_pool, d_diff)
            ybt = _attention(qt, k, vt, sgbt, diff_lambda[j], diff_subln[j], lc, lam_init)
            xc = _outproj_even(xc, u, sga, ybt, _block_diag(pool_w[j]).astype(BF16), pool_scale[j],
                               w_out_even[j].astype(BF16), norm_post[l], mod, ncb, lc)
        else:
            glu, scg, qh, b_f, k_f, b_b, k_b, iv, sog = _inproj_odd(xc, norm_pre[l], mod, w_in_odd[j].astype(BF16),
                                                                    hgrn_lb, l, ncb, d_conv, d_hgrn)
            o_f, o_b = _hgrn(qh, b_f, k_f, b_b, k_b, iv, lc)
            xc = _outproj_odd(xc, glu, scg, conv_w[j], conv_b[j], conv_ln_g[j], conv_ln_b[j], o_f, o_b,
                              hgrn_norm[j], sog, w_out_odd[j].astype(BF16), norm_post[l], mod, ncb, lc,
                              latent_only=l == depth - 1)
    return xc if depth % 2 == 0 else xc[:, lc:]
```

```python
import functools
import math

import jax
import jax.numpy as jnp
from jax import lax
from jax.experimental import pallas as pl
from jax.experimental.pallas import tpu as pltpu

F32 = jnp.float32
BF16 = jnp.bfloat16
EPS = 1e-6

GRID_W = 64
POOL_WINDOWS = (2, 4, 8, 16)
POOL_HALO = max(POOL_WINDOWS) // 2
DIFF_HEAD = 64
DIFF_V = 2 * DIFF_HEAD
DIFF_SCALE = DIFF_HEAD ** -0.5
ROPE_BASE = 10000.0
ROPE_FREQS = DIFF_HEAD // 4
CONV_WIDTH = 31
HGRN_HEAD = 128
LOG2E = 1.4426950408889634

LANES = 128
SUBLANES = 8
VMEM_LIMIT = 48 * 1024 * 1024

ROW_TILE = 256
CONV_HALO = 16
Q_TILE = 128
Q_SUBTILES = 2
ATTN_HEADS_PER_STEP = 2
KV_TILE = 512
SCORE_SLOTS = 3
ONES_ROWS = 16
HGRN_CHUNK = 64
HGRN_DIAG = 2
HGRN_INTERLEAVE = 16


def _cparams(sem):
    return pltpu.CompilerParams(dimension_semantics=sem, vmem_limit_bytes=VMEM_LIMIT)


def _silu(x):
    return x * jax.nn.sigmoid(x)


def _ada_kernel(c_ref, w_ref, b_ref, o_ref):
    c = c_ref[...]
    o_ref[0] = jnp.dot(_silu(c), w_ref[0], preferred_element_type=F32,
                       precision=lax.Precision.HIGHEST) + b_ref[0]


def _ada_call(cvec, ada_w, ada_b):
    depth, d, d3 = ada_w.shape
    nj = d3 // d
    return pl.pallas_call(
        _ada_kernel,
        grid=(depth, nj),
        in_specs=[pl.BlockSpec((SUBLANES, d), lambda l, j: (0, 0)),
                  pl.BlockSpec((1, d, d), lambda l, j: (l, 0, j)),
                  pl.BlockSpec((1, 1, d), lambda l, j: (l, 0, j))],
        out_specs=pl.BlockSpec((1, SUBLANES, d), lambda l, j: (l, 0, j)),
        out_shape=jax.ShapeDtypeStruct((depth, SUBLANES, d3), F32),
        name="ada_params",
        compiler_params=_cparams(("parallel", "parallel")),
    )(cvec, ada_w, ada_b.reshape(depth, 1, d3))


def _prenorm(x_ref, g_ref, mod_ref):
    x = x_ref[0]
    mod = mod_ref[0, 0]
    ms = jnp.mean(x * x, axis=-1, keepdims=True)
    h = x * lax.rsqrt(ms + EPS) * g_ref[...] * (1.0 + mod[1:2]) + mod[0:1]
    return h.astype(BF16)


def _prenorm_all(x_ref, g_ref, mod_ref):
    g = g_ref[...]
    parts = []
    for b in range(x_ref.shape[0]):
        x = x_ref[b]
        mod = mod_ref[b, 0]
        ms = jnp.mean(x * x, axis=-1, keepdims=True)
        parts.append((x * lax.rsqrt(ms + EPS) * g * (1.0 + mod[1:2]) + mod[0:1]).astype(BF16))
    return jnp.concatenate(parts, axis=0)


def _batch_row_specs(bsz, tm, ncb):
    def rows(width):
        return pl.BlockSpec((bsz, tm, width), lambda i: (0, i, 0))

    def cols(height):
        return pl.BlockSpec((bsz, height, tm), lambda i: (0, 0, i))

    def full(*shape):
        return pl.BlockSpec(shape, lambda i: (0,) * len(shape))

    def mod(d):
        return pl.BlockSpec((bsz, 1, 3, d), lambda i: (0, jnp.where(i < ncb, 0, 1), 0, 0))

    return rows, cols, full, mod


def _pipelined(stages):
    nxt = stages[0][0]()
    for g, (_, consume) in enumerate(stages):
        cur = nxt
        if g + 1 < len(stages):
            nxt = stages[g + 1][0]()
        consume(cur)


def _postnorm_residual(x_ref, y, gpost_ref, mod_ref, o_ref):
    mod = mod_ref[0, 0]
    ms = jnp.mean(y * y, axis=-1, keepdims=True)
    yn = y * lax.rsqrt(ms + EPS) * gpost_ref[...]
    o_ref[0] = x_ref[0] + mod[2:3] * yn


def _row_specs(tm, ncb, first=0):
    def rows(width):
        return pl.BlockSpec((1, tm, width), lambda b, i: (b, i + first, 0))

    def full2(a, bdim):
        return pl.BlockSpec((a, bdim), lambda b, i: (0, 0))

    def mod(d):
        return pl.BlockSpec((1, 1, 3, d), lambda b, i: (b, jnp.where(i + first < ncb, 0, 1), 0, 0))

    return rows, full2, mod


def _inproj_even_kernel(x_ref, g_ref, mod_ref, w_ref, wt_ref, cos_ref, sin_ref, cost_ref, sint_ref,
                        u_ref, sga_ref, k_ref, qt_ref, vt_ref, sgbt_ref, *, d_pool, d_diff):
    bsz, tm, _ = x_ref.shape
    hb = _prenorm_all(x_ref, g_ref, mod_ref)
    wide = 2 * LANES
    brows = lambda b: slice(b * tm, (b + 1) * tm)

    def proj(c0, width):
        return lambda: jnp.dot(hb, w_ref[:, c0:c0 + width], preferred_element_type=F32)

    def proj_t(r0, height):
        return lambda: lax.dot_general(wt_ref[r0:r0 + height, :], hb, (((1,), (1,)), ((), ())),
                                       preferred_element_type=F32)

    def pool(r):
        for b in range(bsz):
            u_ref[b] = r[brows(b), :d_pool]
            sga_ref[b] = _silu(r[brows(b), d_pool:])

    lane = lax.broadcasted_iota(jnp.int32, cos_ref.shape, 1)
    first_half = (lane & ROPE_FREQS) == 0

    def keys(c0):
        def put(r):
            for b in range(bsz):
                for s in range(wide // LANES):
                    t = r[brows(b), s * LANES:(s + 1) * LANES]
                    partner = jnp.where(first_half, pltpu.roll(t, LANES - ROPE_FREQS, 1),
                                        pltpu.roll(t, ROPE_FREQS, 1))
                    k_ref[b, :, c0 + s * LANES:c0 + (s + 1) * LANES] = (
                        t * cos_ref[...] + partner * sin_ref[...]).astype(BF16)
        return put

    def gate(r0):
        def put(r):
            for b in range(bsz):
                sgbt_ref[b, r0:r0 + wide, :] = _silu(r[:, brows(b)]).astype(BF16)
        return put

    def queries(r0):
        f = ROPE_FREQS

        def put(r):
            for b in range(bsz):
                for s in range(wide // LANES):
                    t = r[s * LANES:(s + 1) * LANES, brows(b)]
                    partner = jnp.concatenate([t[(blk ^ 1) * f:((blk ^ 1) + 1) * f] for blk in range(LANES // f)],
                                              axis=0)
                    qt_ref[b, r0 + s * LANES:r0 + (s + 1) * LANES, :] = (
                        (t * cost_ref[...] + partner * sint_ref[...]) * (DIFF_SCALE * LOG2E)).astype(BF16)
        return put

    def values(r0):
        def put(r):
            for b in range(bsz):
                vt_ref[b, r0:r0 + wide, :] = r[:, brows(b)].astype(BF16)
        return put

    k0 = 2 * d_pool
    stages = [(proj(0, 2 * d_pool), pool)]
    stages += [(proj(k0 + c, wide), keys(c)) for c in range(0, d_diff, wide)]
    stages += [(proj_t(r, wide), queries(r)) for r in range(0, d_diff, wide)]
    stages += [(proj_t(d_diff + r, wide), values(r)) for r in range(0, d_diff, wide)]
    stages += [(proj_t(2 * d_diff + r, wide), gate(r)) for r in range(0, d_diff, wide)]
    _pipelined(stages)


def _inproj_even(xc, g, mod, w, wt, cos, sin, ncb, d_pool, d_diff):
    bsz, ltot, d = xc.shape
    tm = ROW_TILE
    rows, cols, full, modspec = _batch_row_specs(bsz, tm, ncb)
    tab = pl.BlockSpec((tm, LANES), lambda i: (i, 0))
    tab_t = pl.BlockSpec((LANES, tm), lambda i: (0, i))
    shp = lambda width, dt: jax.ShapeDtypeStruct((bsz, ltot, width), dt)
    shp_t = jax.ShapeDtypeStruct((bsz, d_diff, ltot), BF16)
    return pl.pallas_call(
        functools.partial(_inproj_even_kernel, d_pool=d_pool, d_diff=d_diff),
        grid=(ltot // tm,),
        in_specs=[rows(d), full(1, d), modspec(d), full(*w.shape), full(*wt.shape), tab, tab, tab_t, tab_t],
        out_specs=[rows(d_pool), rows(d_pool), rows(d_diff), cols(d_diff), cols(d_diff), cols(d_diff)],
        out_shape=[shp(d_pool, F32), shp(d_pool, F32), shp(d_diff, BF16), shp_t, shp_t, shp_t],
        name="inproj_even",
        compiler_params=_cparams(("parallel",)),
    )(xc, g.reshape(1, d), mod, w, wt, cos, sin, cos.T, sin.T)


def _attn_kernel(qt_ref, k_ref, vt_ref, sgbt_ref, lam_ref, sub_ref, o_ref, acc_sc, st_sc, mx_sc,
                 *, nhead, nsub, nq, n_ctx_q, ctx_chunks, lat_chunks, lam_init):
    i = pl.program_id(2)
    units = [(e, s) for e in range(nhead) for s in range(nsub)]
    hd = lambda e: slice(e * DIFF_V, (e + 1) * DIFF_V)
    qs = lambda s: slice(s * nq, (s + 1) * nq)
    row = lax.broadcasted_iota(jnp.int32, (DIFF_V, nq), 0)
    ws = []
    for e, s in units:
        qt = qt_ref[0, hd(e), qs(s)]
        zero = jnp.zeros_like(qt)
        ws.append(jnp.concatenate([jnp.where(row < DIFF_HEAD, qt, zero), jnp.where(row >= DIFF_HEAD, qt, zero)],
                                  axis=1))

    ones = jnp.ones((ONES_ROWS, st_sc.shape[2]), BF16)

    def scores(chunk, slot):
        off, size = chunk
        kcs = [k_ref[0, off:off + size, hd(e)] for e in range(nhead)]
        for c, (e, _) in enumerate(units):
            st = jnp.dot(kcs[e], ws[c], preferred_element_type=F32)
            st_sc[slot, c, 0:size] = st
            mx_sc[slot, c] = jnp.max(st, axis=0, keepdims=True)

    def finish(pending):
        for c, (alpha, pv) in enumerate(pending):
            acc_sc[c] = pv if alpha is None else alpha * acc_sc[c] + pv

    def absorb(chunk, slot, ms, pending=None, ahead=None):
        off, size = chunk
        nun = len(units)
        new = [mx_sc[slot, c] if ms is None else jnp.maximum(ms[c], mx_sc[slot, c]) for c in range(nun)]
        alphas = [None if ms is None else jnp.exp2(ms[c] - new[c]) for c in range(nun)]
        if ahead is not None:
            ahead()
        if pending is not None:
            finish(pending)
        vtes = [jnp.concatenate([vt_ref[0, hd(e), off:off + size], ones[:, :size]], axis=0)
                for e in range(nhead)]
        pvs = [jnp.dot(vtes[e], jnp.exp2(st_sc[slot, c, 0:size] - new[c]).astype(BF16),
                       preferred_element_type=F32) for c, (e, _) in enumerate(units)]
        return tuple(new), tuple(zip(alphas, pvs))

    def sweep(chunks):
        slots = SCORE_SLOTS
        for n in range(min(2, len(chunks))):
            scores(chunks[n], n)
        ms, pending = None, None
        for n, chunk in enumerate(chunks):
            ahead = (lambda c=chunks[n + 2], s=(n + 2) % slots: scores(c, s)) if n + 2 < len(chunks) else None
            ms, pending = absorb(chunk, n % slots, ms, pending, ahead)
        finish(pending)

    @pl.when(i < n_ctx_q)
    def _():
        sweep(ctx_chunks)

    @pl.when(i >= n_ctx_q)
    def _():
        sweep(ctx_chunks + lat_chunks)

    lp = lam_ref[...]
    lam = (jnp.exp(jnp.sum(lp[0:1] * lp[1:2], axis=1, keepdims=True))
           - jnp.exp(jnp.sum(lp[2:3] * lp[3:4], axis=1, keepdims=True)) + lam_init)
    accs = [acc_sc[c] for c in range(len(units))]
    o_alls = [acc[:DIFF_V] / acc[DIFF_V:DIFF_V + 1] for acc in accs]
    os = [o_all[:, :nq] - lam * o_all[:, nq:] for o_all in o_alls]
    mss = [jnp.mean(o * o, axis=0, keepdims=True) for o in os]
    gain = sub_ref[...] * (1.0 - lam_init)
    for c, (e, s) in enumerate(units):
        on = os[c] * lax.rsqrt(mss[c] + EPS) * gain
        o_ref[0, hd(e), qs(s)] = (on * sgbt_ref[0, hd(e), qs(s)].astype(F32)).astype(BF16)


def _attention(qt, k, vt, sgbt, lam_p, subln, lc, lam_init):
    bsz, ltot, d_diff = k.shape
    heads = d_diff // DIFF_V
    nq, nsub, tk, nhead = Q_TILE, Q_SUBTILES, KV_TILE, ATTN_HEADS_PER_STEP
    nqs = nq * nsub
    nun = nhead * nsub
    width = nhead * DIFF_V
    assert lc % nqs == 0 and ltot % nqs == 0 and lc % LANES == 0 and (ltot - lc) % tk == 0 and heads % nhead == 0
    ctx_chunks = tuple((o, min(tk, lc - o)) for o in range(0, lc, tk))
    lat_chunks = tuple((o, tk) for o in range(lc, ltot, tk))
    colspec = pl.BlockSpec((1, width, nqs), lambda b, h, i: (b, h, i))
    kern = functools.partial(_attn_kernel, nhead=nhead, nsub=nsub, nq=nq, n_ctx_q=lc // nqs, ctx_chunks=ctx_chunks,
                             lat_chunks=lat_chunks, lam_init=lam_init)
    return pl.pallas_call(
        kern,
        grid=(bsz, heads // nhead, ltot // nqs),
        in_specs=[colspec,
                  pl.BlockSpec((1, ltot, width), lambda b, h, i: (b, 0, h)),
                  pl.BlockSpec((1, width, ltot), lambda b, h, i: (b, h, 0)),
                  colspec,
                  pl.BlockSpec(lam_p.shape, lambda b, h, i: (0, 0)),
                  pl.BlockSpec((DIFF_V, 1), lambda b, h, i: (0, 0))],
        out_specs=colspec,
        out_shape=jax.ShapeDtypeStruct((bsz, d_diff, ltot), BF16),
        scratch_shapes=[pltpu.VMEM((nun, DIFF_V + ONES_ROWS, 2 * nq), F32),
                        pltpu.VMEM((SCORE_SLOTS, nun, tk, 2 * nq), F32),
                        pltpu.VMEM((SCORE_SLOTS, nun, 1, 2 * nq), F32)],
        name="diff_attention",
        compiler_params=_cparams(("parallel", "parallel", "arbitrary")),
    )(qt, k, vt, sgbt, lam_p, subln.reshape(DIFF_V, 1))


def _seq_position(i, tm, ncb, lc, ll):
    nblk = ncb + ll // tm
    is_ctx = i < ncb
    t0 = jnp.where(is_ctx, i, i - ncb) * tm
    lseq = jnp.where(is_ctx, lc, ll)
    has_prev = jnp.logical_and(i != 0, i != ncb)
    has_next = jnp.logical_and(i != ncb - 1, i != nblk - 1)
    return t0, lseq, has_prev, has_next


def _outproj_even_kernel(x_ref, u_ref, up_ref, un_ref, sga_ref, ybt_ref, wpool_ref, pscale_ref, wout_ref,
                         gpost_ref, mod_ref, o_ref, *, tm, ncb, lc, ll, d_pool):
    i = pl.program_id(1)
    t0, lseq, has_prev, has_next = _seq_position(i, tm, ncb, lc, ll)
    y_attn = lax.dot_general(ybt_ref[0], wout_ref[d_pool:, :], (((0,), (0,)), ((), ())),
                             preferred_element_type=F32)
    u = u_ref[0]
    z = jnp.concatenate([jnp.where(has_prev, up_ref[0], 0.0), u, jnp.where(has_next, un_ref[0], 0.0)], axis=0)
    group_width = d_pool // len(POOL_WINDOWS)
    assert group_width & (group_width - 1) == 0
    grp = lax.broadcasted_iota(jnp.int32, u.shape, 1) >> int(math.log2(group_width))
    p, w, s, half = z, 1, None, None
    for g, win in enumerate(POOL_WINDOWS):
        while w < win:
            p = p[:-w] + p[w:]
            w *= 2
        assert w == win, "pooling windows must be powers of two in increasing order"
        start = POOL_HALO - win // 2
        s = p[start:start + tm] if s is None else jnp.where(grp == g, p[start:start + tm], s)
        half = jnp.full(u.shape, win // 2, jnp.int32) if half is None else jnp.where(grp == g, win // 2, half)
    t = lax.broadcasted_iota(jnp.int32, u.shape, 0) + t0
    cnt = jnp.minimum(t + half - 1, lseq - 1) + 1 - jnp.maximum(t - half, 0)
    dpool = s / cnt.astype(F32) - u
    ya = jnp.dot(dpool.astype(BF16), wpool_ref[...], preferred_element_type=F32) * pscale_ref[...] * sga_ref[0]
    y = jnp.dot(ya.astype(BF16), wout_ref[0:d_pool, :], preferred_element_type=F32) + y_attn
    _postnorm_residual(x_ref, y, gpost_ref, mod_ref, o_ref)


def _halo_specs(tm, halo, width, ltot, first=0):
    r = tm // halo
    prev = pl.BlockSpec((1, halo, width), lambda b, i: (b, jnp.maximum((i + first) * r - 1, 0), 0))
    nxt = pl.BlockSpec((1, halo, width),
                       lambda b, i: (b, jnp.minimum((i + first + 1) * r, ltot // halo - 1), 0))
    return prev, nxt


def _outproj_even(xc, u, sga, ybt, wpool, pscale, wout, gpost, mod, ncb, lc):
    bsz, ltot, d = xc.shape
    tm = ROW_TILE
    d_pool = u.shape[-1]
    rows, full2, modspec = _row_specs(tm, ncb)
    prev, nxt = _halo_specs(tm, POOL_HALO, d_pool, ltot)
    kern = functools.partial(_outproj_even_kernel, tm=tm, ncb=ncb, lc=lc, ll=ltot - lc, d_pool=d_pool)
    return pl.pallas_call(
        kern,
        grid=(bsz, ltot // tm),
        in_specs=[rows(d), rows(d_pool), prev, nxt, rows(d_pool),
                  pl.BlockSpec((1, ybt.shape[1], tm), lambda b, i: (b, 0, i)),
                  full2(*wpool.shape), full2(1, d_pool), full2(*wout.shape), full2(1, d), modspec(d)],
        out_specs=rows(d),
        out_shape=jax.ShapeDtypeStruct(xc.shape, F32),
        name="outproj_even",
        compiler_params=_cparams(("parallel", "parallel")),
    )(xc, u, u, u, sga, ybt, wpool, pscale.reshape(1, d_pool), wout, gpost.reshape(1, d), mod)


def _hgrn_lower_bounds(raw, layer):
    rows = [raw[:, j, :] for j in range(raw.shape[1])]
    mx = functools.reduce(jnp.maximum, rows)
    ex = [jnp.exp(r - mx) for r in rows]
    den = functools.reduce(lambda a, bb: a + bb, ex)
    if layer == 0:
        return jnp.zeros_like(mx)
    return functools.reduce(lambda a, bb: a + bb, [e / den for e in ex[1:layer + 1]])


def _chunk_tri(n, c, rev):
    ri = lax.broadcasted_iota(jnp.int32, (n, n), 0)
    ci = lax.broadcasted_iota(jnp.int32, (n, n), 1)
    shift = int(math.log2(c))
    keep = jnp.logical_and((ri >> shift) == (ci >> shift), (ci >= ri) if rev else (ci <= ri))
    return jnp.where(keep, 1.0, 0.0).astype(BF16)


def _cumsum_rows(x, tri):
    hi = x.astype(BF16)
    r1 = x - hi.astype(F32)
    mid = r1.astype(BF16)
    lo = (r1 - mid.astype(F32)).astype(BF16)
    n = x.shape[1]
    r = jnp.dot(tri, jnp.concatenate([hi, mid, lo], axis=1), preferred_element_type=F32)
    return r[:, :n] + r[:, n:2 * n] + r[:, 2 * n:]


def _inproj_odd_kernel(x_ref, g_ref, mod_ref, w_ref, lbraw_ref, glu_ref, scg_ref, qh_ref, bf_ref, kf_ref, bb_ref,
                       kb_ref, iv_ref, sog_ref, *, d_conv, d_hgrn, layer):
    bsz, tm, _ = x_ref.shape
    hb = _prenorm_all(x_ref, g_ref, mod_ref)
    brows = lambda b: slice(b * tm, (b + 1) * tm)
    lbs = _hgrn_lower_bounds(lbraw_ref[...], layer)

    def glu(r):
        for b in range(bsz):
            glu_ref[b] = r[brows(b), :d_conv] * jax.nn.sigmoid(r[brows(b), d_conv:])

    def silu_to(ref):
        def put(r):
            for b in range(bsz):
                ref[b] = _silu(r[brows(b)]).astype(BF16)
        return put

    def gate(dirn, b_ref, k_ref):
        def put(r):
            lb = lbs[dirn:dirn + 1]
            tri = _chunk_tri(tm, HGRN_CHUNK, bool(dirn))
            for b in range(bsz):
                f = lb + (1.0 - lb) * jax.nn.sigmoid(r[brows(b)])
                k_ref[b] = (1.0 - f).astype(BF16)
                b_ref[b] = _cumsum_rows(jnp.log2(f), tri)
        return put

    def value(r):
        for b in range(bsz):
            iv_ref[b] = r[brows(b)].astype(BF16)

    def proj(c0, width):
        return lambda: jnp.dot(hb, w_ref[:, c0:c0 + width], preferred_element_type=F32)

    c0 = 3 * d_conv
    _pipelined([(proj(0, 2 * d_conv), glu), (proj(2 * d_conv, d_conv), silu_to(scg_ref)),
                (proj(c0, d_hgrn), silu_to(qh_ref)), (proj(c0 + d_hgrn, d_hgrn), gate(0, bf_ref, kf_ref)),
                (proj(c0 + 2 * d_hgrn, d_hgrn), gate(1, bb_ref, kb_ref)),
                (proj(c0 + 3 * d_hgrn, d_hgrn), value), (proj(c0 + 4 * d_hgrn, d_hgrn), silu_to(sog_ref))])


def _inproj_odd(xc, g, mod, w, hgrn_lb, layer, ncb, d_conv, d_hgrn):
    bsz, ltot, d = xc.shape
    tm = ROW_TILE
    assert tm % HGRN_CHUNK == 0
    rows, _, full, modspec = _batch_row_specs(bsz, tm, ncb)
    shp = lambda width, dt: jax.ShapeDtypeStruct((bsz, ltot, width), dt)
    return pl.pallas_call(
        functools.partial(_inproj_odd_kernel, d_conv=d_conv, d_hgrn=d_hgrn, layer=layer),
        grid=(ltot // tm,),
        in_specs=[rows(d), full(1, d), modspec(d), full(*w.shape), full(*hgrn_lb.shape)],
        out_specs=[rows(d_conv), rows(d_conv), rows(d_hgrn), rows(d_hgrn), rows(d_hgrn), rows(d_hgrn),
                   rows(d_hgrn), rows(d_hgrn), rows(d_hgrn)],
        out_shape=[shp(d_conv, F32), shp(d_conv, BF16), shp(d_hgrn, BF16), shp(d_hgrn, F32), shp(d_hgrn, BF16),
                   shp(d_hgrn, F32), shp(d_hgrn, BF16), shp(d_hgrn, BF16), shp(d_hgrn, BF16)],
        name="inproj_odd",
        compiler_params=_cparams(("parallel",)),
    )(xc, g.reshape(1, d), mod, w, hgrn_lb)


def _hgrn_masks(c, rev):
    ri = lax.broadcasted_iota(jnp.int32, (c, c), 0)
    ci = lax.broadcasted_iota(jnp.int32, (c, c), 1)
    masks = {}
    w = c // 2
    while w >= HGRN_DIAG:
        shift = int(math.log2(2 * w))
        same_pair = (ri >> shift) == (ci >> shift)
        r_odd, c_odd = (ri & w) != 0, (ci & w) != 0
        owns = jnp.logical_and(c_odd, jnp.logical_not(r_odd)) if rev else jnp.logical_and(r_odd, jnp.logical_not(c_odd))
        masks[w] = jnp.logical_and(same_pair, owns)
        w //= 2
    return masks


class _Chain:
    def __init__(self, q_ref, b_ref, k_ref, v_ref, o_ref, idx, st_ref, rev, masks):
        self.refs = (q_ref, b_ref, k_ref, v_ref, o_ref, idx, st_ref)
        self.rev, self.masks = rev, masks

    def load(self):
        q_ref, b_ref, k_ref, v_ref, _, idx, _ = self.refs
        self.qf = q_ref[idx].astype(F32)
        self.vb = v_ref[idx]
        self.kk = k_ref[idx].astype(F32)
        self.b = b_ref[idx]

    def products(self):
        qf, kk, b, rev = self.qf, self.kk, self.b, self.rev
        c, n = qf.shape
        self.st = self.refs[6][...]
        self.inter = lax.dot_general((qf * jnp.exp2(b)).astype(BF16), self.st.astype(BF16),
                                     (((1,), (1,)), ((), ())), preferred_element_type=F32)
        self.levels = []
        b3 = b.reshape(c // SUBLANES, SUBLANES, n)
        sub = lax.broadcasted_iota(jnp.int32, (1, SUBLANES, 1), 1)
        w = c // 2
        while w >= HGRN_DIAG:
            if 2 * w >= SUBLANES:
                refs = []
                for p in range(c // (2 * w)):
                    r0 = p * 2 * w + (w if rev else w - 1)
                    refs.append(jnp.broadcast_to(b[r0:r0 + 1, :], (2 * w, n)))
                ref = refs[0] if len(refs) == 1 else jnp.concatenate(refs, axis=0)
            else:
                ref3 = None
                for p in range(SUBLANES // (2 * w)):
                    r0 = p * 2 * w + (w if rev else w - 1)
                    piece = jnp.broadcast_to(b3[:, r0:r0 + 1, :], b3.shape)
                    ref3 = piece if ref3 is None else jnp.where(sub >= p * 2 * w, piece, ref3)
                ref = ref3.reshape(c, n)
            qs = jnp.exp2(jnp.minimum(b - ref, 0.0)) * qf
            ks = jnp.exp2(jnp.minimum(ref - b, 0.0)) * kk
            self.levels.append((w, lax.dot_general(qs.astype(BF16), ks.astype(BF16), (((1,), (1,)), ((), ())),
                                                   preferred_element_type=F32)))
            w //= 2
        self.bl = b[0:1, :] if rev else b[c - 1:c, :]
        kd = (kk * jnp.exp2(self.bl - b)).astype(BF16)
        self.st_add = lax.dot_general(self.vb, kd, (((0,), (0,)), ((), ())), preferred_element_type=F32)

    def diagonal(self):
        c, n = self.qf.shape
        blocks = lambda x: x.reshape(c // SUBLANES, SUBLANES, n)
        b3, k3, v3, q3 = blocks(self.b), blocks(self.kk), blocks(self.vb.astype(F32)), blocks(self.qf)
        sub = lax.broadcasted_iota(jnp.int32, (1, SUBLANES, 1), 1) & (HGRN_DIAG - 1)
        diag = jnp.sum(q3 * k3, axis=2, keepdims=True) * v3
        for d in range(1, HGRN_DIAG):
            sh = SUBLANES - d if self.rev else d
            valid = (sub + d <= HGRN_DIAG - 1) if self.rev else (sub >= d)
            e = jnp.exp2(jnp.where(valid, b3 - pltpu.roll(b3, sh, 1), 0.0))
            a = jnp.sum(q3 * e * pltpu.roll(k3, sh, 1), axis=2, keepdims=True)
            diag = diag + jnp.where(valid, a, 0.0) * pltpu.roll(v3, sh, 1)
        self.diag = diag.reshape(c, n)

    def intra(self):
        att = functools.reduce(lambda x, y: x + y, [jnp.where(self.masks[w], a, 0.0) for w, a in self.levels])
        self.intra_out = jnp.dot(att.astype(BF16), self.vb, preferred_element_type=F32)
        self.refs[6][...] = self.st * jnp.exp2(self.bl) + self.st_add

    def finish(self):
        self.refs[4][self.refs[5]] = self.inter + self.intra_out + self.diag


def _hgrn_kernel(qf_ref, bf_ref, kf_ref, vf_ref, qb_ref, bb_ref, kb_ref, vb_ref, of_ref, ob_ref, st_sc,
                 *, bsz, heads):
    s = pl.program_id(0)

    @pl.when(s == 0)
    def _():
        st_sc[...] = jnp.zeros(st_sc.shape, F32)

    c = qf_ref.shape[1]
    fwd, bwd = _hgrn_masks(c, False), _hgrn_masks(c, True)
    chains = []
    for bi in range(bsz):
        for h in range(heads):
            sl = slice(h * HGRN_HEAD, (h + 1) * HGRN_HEAD)
            idx = (bi, slice(None), sl)
            chains.append(_Chain(qf_ref, bf_ref, kf_ref, vf_ref, of_ref, idx, st_sc.at[bi, 0, h], False, fwd))
            chains.append(_Chain(qb_ref, bb_ref, kb_ref, vb_ref, ob_ref, idx, st_sc.at[bi, 1, h], True, bwd))
    for g0 in range(0, len(chains), HGRN_INTERLEAVE):
        group = chains[g0:g0 + HGRN_INTERLEAVE]
        for stage in (_Chain.load, _Chain.products, _Chain.intra, _Chain.diagonal, _Chain.finish):
            for ch in group:
                stage(ch)


def _hgrn(qh, b_f, k_f, b_b, k_b, iv, lc):
    bsz, ltot, d_hgrn = qh.shape
    heads = d_hgrn // HGRN_HEAD
    c = HGRN_CHUNK
    assert lc % c == 0 and ltot % c == 0
    ncc, ntot = lc // c, ltot // c

    def bwd(s):
        return jnp.where(s < ncc, ncc - 1 - s, ntot - 1 + ncc - s)

    fspec = pl.BlockSpec((bsz, c, d_hgrn), lambda s: (0, s, 0))
    bspec = pl.BlockSpec((bsz, c, d_hgrn), lambda s: (0, bwd(s), 0))
    kern = functools.partial(_hgrn_kernel, bsz=bsz, heads=heads)
    return pl.pallas_call(
        kern,
        grid=(ntot,),
        in_specs=[fspec, fspec, fspec, fspec, bspec, bspec, bspec, bspec],
        out_specs=[fspec, bspec],
        out_shape=[jax.ShapeDtypeStruct((bsz, ltot, d_hgrn), F32)] * 2,
        scratch_shapes=[pltpu.VMEM((bsz, 2, heads, HGRN_HEAD, HGRN_HEAD), F32)],
        name="hgrn_scan",
        compiler_params=_cparams(("arbitrary",)),
    )(qh, b_f, k_f, iv, qh, b_b, k_b, iv)


def _outproj_odd_kernel(x_ref, glu_ref, gp_ref, gn_ref, scg_ref, cw_ref, cb_ref, lng_ref, lnb_ref,
                        of_ref, ob_ref, hn_ref, sog_ref, wout_ref, gpost_ref, mod_ref, o_ref, z_sc, zs_sc,
                        *, tm, ncb, lc, ll, d_conv, first):
    i = pl.program_id(1) + first
    _, _, has_prev, has_next = _seq_position(i, tm, ncb, lc, ll)
    h = CONV_HALO
    z_sc[0:h, :] = jnp.where(has_prev, gp_ref[0], 0.0)
    z_sc[h:h + tm, :] = glu_ref[0]
    z_sc[h + tm:, :] = jnp.where(has_next, gn_ref[0], 0.0)
    cw = cw_ref[...]
    base = h - CONV_WIDTH // 2
    span = zs_sc.shape[1]
    for r in range(SUBLANES):
        zs_sc[r] = z_sc[r:r + span, :]
    acc = jnp.zeros((tm, d_conv), F32)
    for j in range(CONV_WIDTH):
        r = (base + j) % SUBLANES
        a0 = base + j - r
        acc = acc + cw[j:j + 1, :] * zs_sc[r, a0:a0 + tm, :]
    zc = acc + cb_ref[...]
    mu = jnp.mean(zc, axis=-1, keepdims=True)
    zc = zc - mu
    var = jnp.mean(zc * zc, axis=-1, keepdims=True)
    zn = zc * lax.rsqrt(var + EPS) * lng_ref[...] + lnb_ref[...]
    yc = _silu(zn) * scg_ref[0].astype(F32)
    o = of_ref[0] + ob_ref[0]
    hn = hn_ref[...]
    parts = []
    for hd in range(o.shape[1] // HGRN_HEAD):
        sl = slice(hd * HGRN_HEAD, (hd + 1) * HGRN_HEAD)
        oh = o[:, sl]
        ms = jnp.mean(oh * oh, axis=-1, keepdims=True)
        parts.append(oh * lax.rsqrt(ms + EPS) * hn[:, sl])
    yd = jnp.concatenate(parts, axis=1) * sog_ref[0].astype(F32)
    y = (jnp.dot(yc.astype(BF16), wout_ref[0:d_conv, :], preferred_element_type=F32)
         + jnp.dot(yd.astype(BF16), wout_ref[d_conv:, :], preferred_element_type=F32))
    _postnorm_residual(x_ref, y, gpost_ref, mod_ref, o_ref)


def _outproj_odd(xc, glu, scg, cw, cb, lng, lnb, o_f, o_b, hnorm, sog, wout, gpost, mod, ncb, lc, latent_only):
    bsz, ltot, d = xc.shape
    tm = ROW_TILE
    d_conv = glu.shape[-1]
    d_hgrn = o_f.shape[-1]
    first = ncb if latent_only else 0
    rows, full2, modspec = _row_specs(tm, ncb, first)
    prev, nxt = _halo_specs(tm, CONV_HALO, d_conv, ltot, first)
    conv_span = (CONV_HALO + CONV_WIDTH // 2) // SUBLANES * SUBLANES
    assert SUBLANES - 1 + tm + conv_span <= tm + 2 * CONV_HALO
    kern = functools.partial(_outproj_odd_kernel, tm=tm, ncb=ncb, lc=lc, ll=ltot - lc, d_conv=d_conv, first=first)
    return pl.pallas_call(
        kern,
        grid=(bsz, ltot // tm - first),
        in_specs=[rows(d), rows(d_conv), prev, nxt, rows(d_conv), full2(*cw.shape), full2(1, d_conv),
                  full2(1, d_conv), full2(1, d_conv), rows(d_hgrn), rows(d_hgrn), full2(1, d_hgrn),
                  rows(d_hgrn), full2(*wout.shape), full2(1, d), modspec(d)],
        out_specs=pl.BlockSpec((1, tm, d), lambda b, i: (b, i, 0)),
        out_shape=jax.ShapeDtypeStruct((bsz, ltot - first * tm, d), F32),
        scratch_shapes=[pltpu.VMEM((tm + 2 * CONV_HALO, d_conv), F32),
                        pltpu.VMEM((SUBLANES, tm + conv_span, d_conv), F32)],
        name="outproj_odd",
        compiler_params=_cparams(("parallel", "parallel")),
    )(xc, glu, glu, glu, scg, cw, cb.reshape(1, d_conv), lng.reshape(1, d_conv), lnb.reshape(1, d_conv),
      o_f, o_b, hnorm.reshape(1, d_hgrn), sog, wout, gpost.reshape(1, d), mod)


def _rope_tables(lc, ll):
    t = jnp.arange(ll)
    inv = ROPE_BASE ** (-jnp.arange(ROPE_FREQS, dtype=F32) / ROPE_FREQS)
    ang = jnp.stack([t // GRID_W, t % GRID_W], axis=-1).astype(F32)[:, :, None] * inv
    cos = jnp.cos(ang)
    sin = jnp.sin(ang)
    cos64 = jnp.stack([cos, cos], axis=2).reshape(ll, DIFF_HEAD)
    sin64 = jnp.stack([-sin, sin], axis=2).reshape(ll, DIFF_HEAD)
    cos_t = jnp.concatenate([jnp.ones((lc, DIFF_HEAD), F32), cos64], axis=0)
    sin_t = jnp.concatenate([jnp.zeros((lc, DIFF_HEAD), F32), sin64], axis=0)
    return jnp.tile(cos_t, (1, LANES // DIFF_HEAD)), jnp.tile(sin_t, (1, LANES // DIFF_HEAD))


def _block_diag(w):
    g, a, b = w.shape
    out = jnp.zeros((g * a, g * b), w.dtype)
    for j in range(g):
        out = out.at[j * a:(j + 1) * a, j * b:(j + 1) * b].set(w[j])
    return out


def kernel(x, c, ctx, c_ctx, ada_w, ada_b, norm_pre, norm_post, w_in_even, w_out_even, pool_w, pool_scale,
           diff_lambda, diff_subln, w_in_odd, w_out_odd, conv_w, conv_b, conv_ln_g, conv_ln_b, hgrn_norm,
           hgrn_lb):
    bsz, ll, d = x.shape
    lc = ctx.shape[1]
    depth = ada_w.shape[0]
    d_pool = pool_scale.shape[-1]
    d_diff = w_out_even.shape[1] - d_pool
    d_conv = conv_b.shape[-1]
    d_hgrn = hgrn_norm.shape[-1]
    assert lc % ROW_TILE == 0 and ll % ROW_TILE == 0 and bsz + 1 <= SUBLANES
    ncb = lc // ROW_TILE

    cvec = jnp.zeros((SUBLANES, d), F32).at[:bsz].set(c).at[bsz].set(c_ctx)
    ada = _ada_call(cvec, ada_w, ada_b).reshape(depth, SUBLANES, 3, d)
    xc = jnp.concatenate([ctx, x], axis=1)
    cos, sin = _rope_tables(lc, ll)

    for l in range(depth):
        mod = jnp.stack([jnp.broadcast_to(ada[l, bsz], (bsz, 3, d)), ada[l, :bsz]], axis=1)
        j = l // 2
        if l % 2 == 0:
            lam_init = 0.8 - 0.6 * math.exp(-0.3 * l)
            wb = w_in_even[j].astype(BF16)
            q0, k0, v0, g0 = 2 * d_pool, 2 * d_pool + d_diff, 2 * d_pool + 2 * d_diff, 2 * d_pool + 3 * d_diff
            w_rows = jnp.concatenate([wb[:, :q0], wb[:, k0:v0]], axis=1)
            w_cols_t = jnp.concatenate([wb[:, q0:k0], wb[:, v0:g0], wb[:, g0:]], axis=1).T
            u, sga, k, qt, vt, sgbt = _inproj_even(xc, norm_pre[l], mod, w_rows, w_cols_t, cos, sin, ncb,
                                                   d_pool, d_diff)
            ybt = _attention(qt, k, vt, sgbt, diff_lambda[j], diff_subln[j], lc, lam_init)
            xc = _outproj_even(xc, u, sga, ybt, _block_diag(pool_w[j]).astype(BF16), pool_scale[j],
                               w_out_even[j].astype(BF16), norm_post[l], mod, ncb, lc)
        else:
            glu, scg, qh, b_f, k_f, b_b, k_b, iv, sog = _inproj_odd(xc, norm_pre[l], mod, w_in_odd[j].astype(BF16),
                                                                    hgrn_lb, l, ncb, d_conv, d_hgrn)
            o_f, o_b = _hgrn(qh, b_f, k_f, b_b, k_b, iv, lc)
            xc = _outproj_odd(xc, glu, scg, conv_w[j], conv_b[j], conv_ln_g[j], conv_ln_b[j], o_f, o_b,
                              hgrn_norm[j], sog, w_out_odd[j].astype(BF16), norm_post[l], mod, ncb, lc,
                              latent_only=l == depth - 1)
    return xc if depth % 2 == 0 else xc[:, lc:]
```

```python
import functools
import math

import jax
import jax.numpy as jnp
from jax import lax
from jax.experimental import pallas as pl
from jax.experimental.pallas import tpu as pltpu

F32 = jnp.float32
BF16 = jnp.bfloat16
EPS = 1e-6

GRID_W = 64
POOL_WINDOWS = (2, 4, 8, 16)
POOL_HALO = max(POOL_WINDOWS) // 2
DIFF_HEAD = 64
DIFF_V = 2 * DIFF_HEAD
DIFF_SCALE = DIFF_HEAD ** -0.5
ROPE_BASE = 10000.0
ROPE_FREQS = DIFF_HEAD // 4
CONV_WIDTH = 31
HGRN_HEAD = 128
LOG2E = 1.4426950408889634

LANES = 128
SUBLANES = 8
VMEM_LIMIT = 48 * 1024 * 1024

ROW_TILE = 256
CONV_HALO = 16
Q_TILE = 128
Q_SUBTILES = 2
ATTN_HEADS_PER_STEP = 2
KV_TILE = 512
SCORE_SLOTS = 3
ONES_ROWS = 16
HGRN_CHUNK = 64
HGRN_DIAG = 2
HGRN_INTERLEAVE = 16


def _cparams(sem):
    return pltpu.CompilerParams(dimension_semantics=sem, vmem_limit_bytes=VMEM_LIMIT)


def _silu(x):
    return x * jax.nn.sigmoid(x)


def _ada_kernel(c_ref, w_ref, b_ref, o_ref):
    c = c_ref[...]
    o_ref[0] = jnp.dot(_silu(c), w_ref[0], preferred_element_type=F32,
                       precision=lax.Precision.HIGHEST) + b_ref[0]


def _ada_call(cvec, ada_w, ada_b):
    depth, d, d3 = ada_w.shape
    nj = d3 // d
    return pl.pallas_call(
        _ada_kernel,
        grid=(depth, nj),
        in_specs=[pl.BlockSpec((SUBLANES, d), lambda l, j: (0, 0)),
                  pl.BlockSpec((1, d, d), lambda l, j: (l, 0, j)),
                  pl.BlockSpec((1, 1, d), lambda l, j: (l, 0, j))],
        out_specs=pl.BlockSpec((1, SUBLANES, d), lambda l, j: (l, 0, j)),
        out_shape=jax.ShapeDtypeStruct((depth, SUBLANES, d3), F32),
        name="ada_params",
        compiler_params=_cparams(("parallel", "parallel")),
    )(cvec, ada_w, ada_b.reshape(depth, 1, d3))


def _batch_tiles(x_ref, ctx_ref=None, is_ctx=None):
    if ctx_ref is None:
        return [x_ref[b] for b in range(x_ref.shape[0])]
    return [jnp.where(is_ctx, ctx_ref[b], x_ref[b]) for b in range(x_ref.shape[0])]


def _prenorm_all(xs, g_ref, mod_ref):
    g = g_ref[...]
    parts = []
    for b, x in enumerate(xs):
        mod = mod_ref[b, 0]
        ms = jnp.mean(x * x, axis=-1, keepdims=True)
        parts.append((x * lax.rsqrt(ms + EPS) * g * (1.0 + mod[1:2]) + mod[0:1]).astype(BF16))
    return jnp.concatenate(parts, axis=0)


def _batch_row_specs(bsz, tm, ncb):
    def rows(width):
        return pl.BlockSpec((bsz, tm, width), lambda i: (0, i, 0))

    def cols(height):
        return pl.BlockSpec((bsz, height, tm), lambda i: (0, 0, i))

    def full(*shape):
        return pl.BlockSpec(shape, lambda i: (0,) * len(shape))

    def mod(d):
        return pl.BlockSpec((bsz, 1, 3, d), lambda i: (0, jnp.where(i < ncb, 0, 1), 0, 0))

    return rows, cols, full, mod


def _pipelined(stages):
    nxt = stages[0][0]()
    for g, (_, consume) in enumerate(stages):
        cur = nxt
        if g + 1 < len(stages):
            nxt = stages[g + 1][0]()
        consume(cur)


def _postnorm_residual(x, y, gpost_ref, mod_ref, o_ref):
    mod = mod_ref[0, 0]
    ms = jnp.mean(y * y, axis=-1, keepdims=True)
    yn = y * lax.rsqrt(ms + EPS) * gpost_ref[...]
    o_ref[0] = x + mod[2:3] * yn


def _row_specs(tm, ncb, first=0):
    def rows(width):
        return pl.BlockSpec((1, tm, width), lambda b, i: (b, i + first, 0))

    def full2(a, bdim):
        return pl.BlockSpec((a, bdim), lambda b, i: (0, 0))

    def mod(d):
        return pl.BlockSpec((1, 1, 3, d), lambda b, i: (b, jnp.where(i + first < ncb, 0, 1), 0, 0))

    return rows, full2, mod


def _inproj_even_kernel(x_ref, *refs, **dims):
    _inproj_even_body(_batch_tiles(x_ref), *refs, **dims)


def _inproj_even_split_kernel(ctx_ref, x_ref, *refs, ncb, **dims):
    _inproj_even_body(_batch_tiles(x_ref, ctx_ref, pl.program_id(0) < ncb), *refs, **dims)


def _inproj_even_body(xs, g_ref, mod_ref, w_ref, wt_ref, cos_ref, sin_ref, cost_ref, sint_ref,
                      u_ref, sga_ref, k_ref, qt_ref, vt_ref, sgbt_ref, *, d_pool, d_diff):
    bsz, tm = len(xs), xs[0].shape[0]
    hb = _prenorm_all(xs, g_ref, mod_ref)
    wide = 2 * LANES
    brows = lambda b: slice(b * tm, (b + 1) * tm)

    def proj(c0, width):
        return lambda: jnp.dot(hb, w_ref[:, c0:c0 + width], preferred_element_type=F32)

    def proj_t(r0, height):
        return lambda: lax.dot_general(wt_ref[r0:r0 + height, :], hb, (((1,), (1,)), ((), ())),
                                       preferred_element_type=F32)

    def pool(r):
        for b in range(bsz):
            u_ref[b] = r[brows(b), :d_pool]
            sga_ref[b] = _silu(r[brows(b), d_pool:])

    lane = lax.broadcasted_iota(jnp.int32, cos_ref.shape, 1)
    first_half = (lane & ROPE_FREQS) == 0

    def keys(c0):
        def put(r):
            for b in range(bsz):
                for s in range(wide // LANES):
                    t = r[brows(b), s * LANES:(s + 1) * LANES]
                    partner = jnp.where(first_half, pltpu.roll(t, LANES - ROPE_FREQS, 1),
                                        pltpu.roll(t, ROPE_FREQS, 1))
                    k_ref[b, :, c0 + s * LANES:c0 + (s + 1) * LANES] = (
                        t * cos_ref[...] + partner * sin_ref[...]).astype(BF16)
        return put

    def gate(r0):
        def put(r):
            for b in range(bsz):
                sgbt_ref[b, r0:r0 + wide, :] = _silu(r[:, brows(b)]).astype(BF16)
        return put

    def queries(r0):
        f = ROPE_FREQS

        def put(r):
            for b in range(bsz):
                for s in range(wide // LANES):
                    t = r[s * LANES:(s + 1) * LANES, brows(b)]
                    partner = jnp.concatenate([t[(blk ^ 1) * f:((blk ^ 1) + 1) * f] for blk in range(LANES // f)],
                                              axis=0)
                    qt_ref[b, r0 + s * LANES:r0 + (s + 1) * LANES, :] = (
                        (t * cost_ref[...] + partner * sint_ref[...]) * (DIFF_SCALE * LOG2E)).astype(BF16)
        return put

    def values(r0):
        def put(r):
            for b in range(bsz):
                vt_ref[b, r0:r0 + wide, :] = r[:, brows(b)].astype(BF16)
        return put

    k0 = 2 * d_pool
    stages = [(proj(0, 2 * d_pool), pool)]
    stages += [(proj(k0 + c, wide), keys(c)) for c in range(0, d_diff, wide)]
    stages += [(proj_t(r, wide), queries(r)) for r in range(0, d_diff, wide)]
    stages += [(proj_t(d_diff + r, wide), values(r)) for r in range(0, d_diff, wide)]
    stages += [(proj_t(2 * d_diff + r, wide), gate(r)) for r in range(0, d_diff, wide)]
    _pipelined(stages)


def _split_row_specs(bsz, tm, ncb, d, batch_in_grid):
    ctx_blk = lambda i: jnp.minimum(i, ncb - 1)
    lat_blk = lambda i: jnp.maximum(i - ncb, 0)
    if batch_in_grid:
        return (pl.BlockSpec((1, tm, d), lambda b, i: (b, ctx_blk(i), 0)),
                pl.BlockSpec((1, tm, d), lambda b, i: (b, lat_blk(i), 0)))
    return (pl.BlockSpec((bsz, tm, d), lambda i: (0, ctx_blk(i), 0)),
            pl.BlockSpec((bsz, tm, d), lambda i: (0, lat_blk(i), 0)))


def _inproj_even(xc, g, mod, w, wt, cos, sin, ncb, d_pool, d_diff):
    split = isinstance(xc, tuple)
    bsz, ll, d = xc[1].shape if split else xc.shape
    ltot = ll + xc[0].shape[1] if split else ll
    tm = ROW_TILE
    rows, cols, full, modspec = _batch_row_specs(bsz, tm, ncb)
    tab = pl.BlockSpec((tm, LANES), lambda i: (i, 0))
    tab_t = pl.BlockSpec((LANES, tm), lambda i: (0, i))
    shp = lambda width, dt: jax.ShapeDtypeStruct((bsz, ltot, width), dt)
    shp_t = jax.ShapeDtypeStruct((bsz, d_diff, ltot), BF16)
    if split:
        kern = functools.partial(_inproj_even_split_kernel, ncb=ncb, d_pool=d_pool, d_diff=d_diff)
        x_specs, x_args = list(_split_row_specs(bsz, tm, ncb, d, False)), list(xc)
    else:
        kern = functools.partial(_inproj_even_kernel, d_pool=d_pool, d_diff=d_diff)
        x_specs, x_args = [rows(d)], [xc]
    return pl.pallas_call(
        kern,
        grid=(ltot // tm,),
        in_specs=x_specs + [full(1, d), modspec(d), full(*w.shape), full(*wt.shape), tab, tab, tab_t, tab_t],
        out_specs=[rows(d_pool), rows(d_pool), rows(d_diff), cols(d_diff), cols(d_diff), cols(d_diff)],
        out_shape=[shp(d_pool, F32), shp(d_pool, F32), shp(d_diff, BF16), shp_t, shp_t, shp_t],
        name="inproj_even",
        compiler_params=_cparams(("parallel",)),
    )(*x_args, g.reshape(1, d), mod, w, wt, cos, sin, cos.T, sin.T)


def _attn_kernel(qt_ref, k_ref, vt_ref, sgbt_ref, lam_ref, sub_ref, o_ref, acc_sc, st_sc, mx_sc,
                 *, nhead, nsub, nq, n_ctx_q, ctx_chunks, lat_chunks, lam_init):
    i = pl.program_id(2)
    units = [(e, s) for e in range(nhead) for s in range(nsub)]
    hd = lambda e: slice(e * DIFF_V, (e + 1) * DIFF_V)
    qs = lambda s: slice(s * nq, (s + 1) * nq)
    row = lax.broadcasted_iota(jnp.int32, (DIFF_V, nq), 0)
    ws = []
    for e, s in units:
        qt = qt_ref[0, hd(e), qs(s)]
        zero = jnp.zeros_like(qt)
        ws.append(jnp.concatenate([jnp.where(row < DIFF_HEAD, qt, zero), jnp.where(row >= DIFF_HEAD, qt, zero)],
                                  axis=1))

    ones = jnp.ones((ONES_ROWS, st_sc.shape[2]), BF16)

    def scores(chunk, slot):
        off, size = chunk
        kcs = [k_ref[0, off:off + size, hd(e)] for e in range(nhead)]
        for c, (e, _) in enumerate(units):
            st = jnp.dot(kcs[e], ws[c], preferred_element_type=F32)
            st_sc[slot, c, 0:size] = st
            mx_sc[slot, c] = jnp.max(st, axis=0, keepdims=True)

    def finish(pending):
        for c, (alpha, pv) in enumerate(pending):
            acc_sc[c] = pv if alpha is None else alpha * acc_sc[c] + pv

    def absorb(chunk, slot, ms, pending=None, ahead=None):
        off, size = chunk
        nun = len(units)
        new = [mx_sc[slot, c] if ms is None else jnp.maximum(ms[c], mx_sc[slot, c]) for c in range(nun)]
        alphas = [None if ms is None else jnp.exp2(ms[c] - new[c]) for c in range(nun)]
        if ahead is not None:
            ahead()
        if pending is not None:
            finish(pending)
        vtes = [jnp.concatenate([vt_ref[0, hd(e), off:off + size], ones[:, :size]], axis=0)
                for e in range(nhead)]
        pvs = [jnp.dot(vtes[e], jnp.exp2(st_sc[slot, c, 0:size] - new[c]).astype(BF16),
                       preferred_element_type=F32) for c, (e, _) in enumerate(units)]
        return tuple(new), tuple(zip(alphas, pvs))

    def sweep(chunks):
        slots = SCORE_SLOTS
        for n in range(min(2, len(chunks))):
            scores(chunks[n], n)
        ms, pending = None, None
        for n, chunk in enumerate(chunks):
            ahead = (lambda c=chunks[n + 2], s=(n + 2) % slots: scores(c, s)) if n + 2 < len(chunks) else None
            ms, pending = absorb(chunk, n % slots, ms, pending, ahead)
        finish(pending)

    @pl.when(i < n_ctx_q)
    def _():
        sweep(ctx_chunks)

    @pl.when(i >= n_ctx_q)
    def _():
        sweep(ctx_chunks + lat_chunks)

    lp = lam_ref[...]
    lam = (jnp.exp(jnp.sum(lp[0:1] * lp[1:2], axis=1, keepdims=True))
           - jnp.exp(jnp.sum(lp[2:3] * lp[3:4], axis=1, keepdims=True)) + lam_init)
    accs = [acc_sc[c] for c in range(len(units))]
    o_alls = [acc[:DIFF_V] / acc[DIFF_V:DIFF_V + 1] for acc in accs]
    os = [o_all[:, :nq] - lam * o_all[:, nq:] for o_all in o_alls]
    mss = [jnp.mean(o * o, axis=0, keepdims=True) for o in os]
    gain = sub_ref[...] * (1.0 - lam_init)
    for c, (e, s) in enumerate(units):
        on = os[c] * lax.rsqrt(mss[c] + EPS) * gain
        o_ref[0, hd(e), qs(s)] = (on * sgbt_ref[0, hd(e), qs(s)].astype(F32)).astype(BF16)


def _attention(qt, k, vt, sgbt, lam_p, subln, lc, lam_init):
    bsz, ltot, d_diff = k.shape
    heads = d_diff // DIFF_V
    nq, nsub, tk, nhead = Q_TILE, Q_SUBTILES, KV_TILE, ATTN_HEADS_PER_STEP
    nqs = nq * nsub
    nun = nhead * nsub
    width = nhead * DIFF_V
    assert lc % nqs == 0 and ltot % nqs == 0 and lc % LANES == 0 and (ltot - lc) % tk == 0 and heads % nhead == 0
    ctx_chunks = tuple((o, min(tk, lc - o)) for o in range(0, lc, tk))
    lat_chunks = tuple((o, tk) for o in range(lc, ltot, tk))
    colspec = pl.BlockSpec((1, width, nqs), lambda b, h, i: (b, h, i))
    kern = functools.partial(_attn_kernel, nhead=nhead, nsub=nsub, nq=nq, n_ctx_q=lc // nqs, ctx_chunks=ctx_chunks,
                             lat_chunks=lat_chunks, lam_init=lam_init)
    return pl.pallas_call(
        kern,
        grid=(bsz, heads // nhead, ltot // nqs),
        in_specs=[colspec,
                  pl.BlockSpec((1, ltot, width), lambda b, h, i: (b, 0, h)),
                  pl.BlockSpec((1, width, ltot), lambda b, h, i: (b, h, 0)),
                  colspec,
                  pl.BlockSpec(lam_p.shape, lambda b, h, i: (0, 0)),
                  pl.BlockSpec((DIFF_V, 1), lambda b, h, i: (0, 0))],
        out_specs=colspec,
        out_shape=jax.ShapeDtypeStruct((bsz, d_diff, ltot), BF16),
        scratch_shapes=[pltpu.VMEM((nun, DIFF_V + ONES_ROWS, 2 * nq), F32),
                        pltpu.VMEM((SCORE_SLOTS, nun, tk, 2 * nq), F32),
                        pltpu.VMEM((SCORE_SLOTS, nun, 1, 2 * nq), F32)],
        name="diff_attention",
        compiler_params=_cparams(("parallel", "parallel", "arbitrary")),
    )(qt, k, vt, sgbt, lam_p, subln.reshape(DIFF_V, 1))


def _seq_position(i, tm, ncb, lc, ll):
    nblk = ncb + ll // tm
    is_ctx = i < ncb
    t0 = jnp.where(is_ctx, i, i - ncb) * tm
    lseq = jnp.where(is_ctx, lc, ll)
    has_prev = jnp.logical_and(i != 0, i != ncb)
    has_next = jnp.logical_and(i != ncb - 1, i != nblk - 1)
    return t0, lseq, has_prev, has_next


def _outproj_even_kernel(x_ref, *refs, **dims):
    _outproj_even_body(x_ref[0], *refs, **dims)


def _outproj_even_split_kernel(ctx_ref, x_ref, *refs, **dims):
    _outproj_even_body(jnp.where(pl.program_id(1) < dims["ncb"], ctx_ref[0], x_ref[0]), *refs, **dims)


def _outproj_even_body(x, u_ref, up_ref, un_ref, sga_ref, ybt_ref, wpool_ref, pscale_ref, wout_ref,
                       gpost_ref, mod_ref, o_ref, *, tm, ncb, lc, ll, d_pool):
    i = pl.program_id(1)
    t0, lseq, has_prev, has_next = _seq_position(i, tm, ncb, lc, ll)
    y_attn = lax.dot_general(ybt_ref[0], wout_ref[d_pool:, :], (((0,), (0,)), ((), ())),
                             preferred_element_type=F32)
    u = u_ref[0]
    z = jnp.concatenate([jnp.where(has_prev, up_ref[0], 0.0), u, jnp.where(has_next, un_ref[0], 0.0)], axis=0)
    group_width = d_pool // len(POOL_WINDOWS)
    assert group_width & (group_width - 1) == 0
    grp = lax.broadcasted_iota(jnp.int32, u.shape, 1) >> int(math.log2(group_width))
    p, w, s, half = z, 1, None, None
    for g, win in enumerate(POOL_WINDOWS):
        while w < win:
            p = p[:-w] + p[w:]
            w *= 2
        assert w == win, "pooling windows must be powers of two in increasing order"
        start = POOL_HALO - win // 2
        s = p[start:start + tm] if s is None else jnp.where(grp == g, p[start:start + tm], s)
        half = jnp.full(u.shape, win // 2, jnp.int32) if half is None else jnp.where(grp == g, win // 2, half)
    t = lax.broadcasted_iota(jnp.int32, u.shape, 0) + t0
    cnt = jnp.minimum(t + half - 1, lseq - 1) + 1 - jnp.maximum(t - half, 0)
    dpool = s / cnt.astype(F32) - u
    ya = jnp.dot(dpool.astype(BF16), wpool_ref[...], preferred_element_type=F32) * pscale_ref[...] * sga_ref[0]
    y = jnp.dot(ya.astype(BF16), wout_ref[0:d_pool, :], preferred_element_type=F32) + y_attn
    _postnorm_residual(x, y, gpost_ref, mod_ref, o_ref)


def _halo_specs(tm, halo, width, ltot, first=0):
    r = tm // halo
    prev = pl.BlockSpec((1, halo, width), lambda b, i: (b, jnp.maximum((i + first) * r - 1, 0), 0))
    nxt = pl.BlockSpec((1, halo, width),
                       lambda b, i: (b, jnp.minimum((i + first + 1) * r, ltot // halo - 1), 0))
    return prev, nxt


def _outproj_even(xc, u, sga, ybt, wpool, pscale, wout, gpost, mod, ncb, lc):
    split = isinstance(xc, tuple)
    bsz, ltot, d_pool = u.shape
    d = wout.shape[1]
    tm = ROW_TILE
    rows, full2, modspec = _row_specs(tm, ncb)
    prev, nxt = _halo_specs(tm, POOL_HALO, d_pool, ltot)
    dims = dict(tm=tm, ncb=ncb, lc=lc, ll=ltot - lc, d_pool=d_pool)
    if split:
        kern = functools.partial(_outproj_even_split_kernel, **dims)
        x_specs, x_args = list(_split_row_specs(bsz, tm, ncb, d, True)), list(xc)
    else:
        kern = functools.partial(_outproj_even_kernel, **dims)
        x_specs, x_args = [rows(d)], [xc]
    return pl.pallas_call(
        kern,
        grid=(bsz, ltot // tm),
        in_specs=x_specs + [rows(d_pool), prev, nxt, rows(d_pool),
                            pl.BlockSpec((1, ybt.shape[1], tm), lambda b, i: (b, 0, i)),
                            full2(*wpool.shape), full2(1, d_pool), full2(*wout.shape), full2(1, d), modspec(d)],
        out_specs=rows(d),
        out_shape=jax.ShapeDtypeStruct((bsz, ltot, d), F32),
        name="outproj_even",
        compiler_params=_cparams(("parallel", "parallel")),
    )(*x_args, u, u, u, sga, ybt, wpool, pscale.reshape(1, d_pool), wout, gpost.reshape(1, d), mod)


def _hgrn_lower_bounds(raw, layer):
    rows = [raw[:, j, :] for j in range(raw.shape[1])]
    mx = functools.reduce(jnp.maximum, rows)
    ex = [jnp.exp(r - mx) for r in rows]
    den = functools.reduce(lambda a, bb: a + bb, ex)
    if layer == 0:
        return jnp.zeros_like(mx)
    return functools.reduce(lambda a, bb: a + bb, [e / den for e in ex[1:layer + 1]])


def _chunk_tri(n, c, rev):
    ri = lax.broadcasted_iota(jnp.int32, (n, n), 0)
    ci = lax.broadcasted_iota(jnp.int32, (n, n), 1)
    shift = int(math.log2(c))
    keep = jnp.logical_and((ri >> shift) == (ci >> shift), (ci >= ri) if rev else (ci <= ri))
    return jnp.where(keep, 1.0, 0.0).astype(BF16)


def _cumsum_rows(x, tri):
    hi = x.astype(BF16)
    r1 = x - hi.astype(F32)
    mid = r1.astype(BF16)
    lo = (r1 - mid.astype(F32)).astype(BF16)
    n = x.shape[1]
    r = jnp.dot(tri, jnp.concatenate([hi, mid, lo], axis=1), preferred_element_type=F32)
    return r[:, :n] + r[:, n:2 * n] + r[:, 2 * n:]


def _inproj_odd_kernel(x_ref, g_ref, mod_ref, w_ref, lbraw_ref, glu_ref, scg_ref, qh_ref, bf_ref, kf_ref, bb_ref,
                       kb_ref, iv_ref, sog_ref, *, d_conv, d_hgrn, layer):
    bsz, tm, _ = x_ref.shape
    hb = _prenorm_all(_batch_tiles(x_ref), g_ref, mod_ref)
    brows = lambda b: slice(b * tm, (b + 1) * tm)
    lbs = _hgrn_lower_bounds(lbraw_ref[...], layer)

    def glu(r):
        for b in range(bsz):
            glu_ref[b] = r[brows(b), :d_conv] * jax.nn.sigmoid(r[brows(b), d_conv:])

    def silu_to(ref):
        def put(r):
            for b in range(bsz):
                ref[b] = _silu(r[brows(b)]).astype(BF16)
        return put

    def gate(dirn, b_ref, k_ref):
        def put(r):
            lb = lbs[dirn:dirn + 1]
            tri = _chunk_tri(tm, HGRN_CHUNK, bool(dirn))
            for b in range(bsz):
                f = lb + (1.0 - lb) * jax.nn.sigmoid(r[brows(b)])
                k_ref[b] = (1.0 - f).astype(BF16)
                b_ref[b] = _cumsum_rows(jnp.log2(f), tri)
        return put

    def value(r):
        for b in range(bsz):
            iv_ref[b] = r[brows(b)].astype(BF16)

    def proj(c0, width):
        return lambda: jnp.dot(hb, w_ref[:, c0:c0 + width], preferred_element_type=F32)

    c0 = 3 * d_conv
    _pipelined([(proj(0, 2 * d_conv), glu), (proj(2 * d_conv, d_conv), silu_to(scg_ref)),
                (proj(c0, d_hgrn), silu_to(qh_ref)), (proj(c0 + d_hgrn, d_hgrn), gate(0, bf_ref, kf_ref)),
                (proj(c0 + 2 * d_hgrn, d_hgrn), gate(1, bb_ref, kb_ref)),
                (proj(c0 + 3 * d_hgrn, d_hgrn), value), (proj(c0 + 4 * d_hgrn, d_hgrn), silu_to(sog_ref))])


def _inproj_odd(xc, g, mod, w, hgrn_lb, layer, ncb, d_conv, d_hgrn):
    bsz, ltot, d = xc.shape
    tm = ROW_TILE
    assert tm % HGRN_CHUNK == 0
    rows, _, full, modspec = _batch_row_specs(bsz, tm, ncb)
    shp = lambda width, dt: jax.ShapeDtypeStruct((bsz, ltot, width), dt)
    return pl.pallas_call(
        functools.partial(_inproj_odd_kernel, d_conv=d_conv, d_hgrn=d_hgrn, layer=layer),
        grid=(ltot // tm,),
        in_specs=[rows(d), full(1, d), modspec(d), full(*w.shape), full(*hgrn_lb.shape)],
        out_specs=[rows(d_conv), rows(d_conv), rows(d_hgrn), rows(d_hgrn), rows(d_hgrn), rows(d_hgrn),
                   rows(d_hgrn), rows(d_hgrn), rows(d_hgrn)],
        out_shape=[shp(d_conv, F32), shp(d_conv, BF16), shp(d_hgrn, BF16), shp(d_hgrn, F32), shp(d_hgrn, BF16),
                   shp(d_hgrn, F32), shp(d_hgrn, BF16), shp(d_hgrn, BF16), shp(d_hgrn, BF16)],
        name="inproj_odd",
        compiler_params=_cparams(("parallel",)),
    )(xc, g.reshape(1, d), mod, w, hgrn_lb)


def _hgrn_masks(c, rev):
    ri = lax.broadcasted_iota(jnp.int32, (c, c), 0)
    ci = lax.broadcasted_iota(jnp.int32, (c, c), 1)
    masks = {}
    w = c // 2
    while w >= HGRN_DIAG:
        shift = int(math.log2(2 * w))
        same_pair = (ri >> shift) == (ci >> shift)
        r_odd, c_odd = (ri & w) != 0, (ci & w) != 0
        owns = jnp.logical_and(c_odd, jnp.logical_not(r_odd)) if rev else jnp.logical_and(r_odd, jnp.logical_not(c_odd))
        masks[w] = jnp.logical_and(same_pair, owns)
        w //= 2
    return masks


class _Chain:
    def __init__(self, q_ref, b_ref, k_ref, v_ref, o_ref, idx, st_ref, rev, masks):
        self.refs = (q_ref, b_ref, k_ref, v_ref, o_ref, idx, st_ref)
        self.rev, self.masks = rev, masks

    def load(self):
        q_ref, b_ref, k_ref, v_ref, _, idx, _ = self.refs
        self.qf = q_ref[idx].astype(F32)
        self.vb = v_ref[idx]
        self.kk = k_ref[idx].astype(F32)
        self.b = b_ref[idx]

    def products(self):
        qf, kk, b, rev = self.qf, self.kk, self.b, self.rev
        c, n = qf.shape
        self.st = self.refs[6][...]
        self.inter = lax.dot_general((qf * jnp.exp2(b)).astype(BF16), self.st.astype(BF16),
                                     (((1,), (1,)), ((), ())), preferred_element_type=F32)
        self.levels = []
        b3 = b.reshape(c // SUBLANES, SUBLANES, n)
        sub = lax.broadcasted_iota(jnp.int32, (1, SUBLANES, 1), 1)
        w = c // 2
        while w >= HGRN_DIAG:
            if 2 * w >= SUBLANES:
                refs = []
                for p in range(c // (2 * w)):
                    r0 = p * 2 * w + (w if rev else w - 1)
                    refs.append(jnp.broadcast_to(b[r0:r0 + 1, :], (2 * w, n)))
                ref = refs[0] if len(refs) == 1 else jnp.concatenate(refs, axis=0)
            else:
                ref3 = None
                for p in range(SUBLANES // (2 * w)):
                    r0 = p * 2 * w + (w if rev else w - 1)
                    piece = jnp.broadcast_to(b3[:, r0:r0 + 1, :], b3.shape)
                    ref3 = piece if ref3 is None else jnp.where(sub >= p * 2 * w, piece, ref3)
                ref = ref3.reshape(c, n)
            qs = jnp.exp2(jnp.minimum(b - ref, 0.0)) * qf
            ks = jnp.exp2(jnp.minimum(ref - b, 0.0)) * kk
            self.levels.append((w, lax.dot_general(qs.astype(BF16), ks.astype(BF16), (((1,), (1,)), ((), ())),
                                                   preferred_element_type=F32)))
            w //= 2
        self.bl = b[0:1, :] if rev else b[c - 1:c, :]
        kd = (kk * jnp.exp2(self.bl - b)).astype(BF16)
        self.st_add = lax.dot_general(self.vb, kd, (((0,), (0,)), ((), ())), preferred_element_type=F32)

    def diagonal(self):
        c, n = self.qf.shape
        blocks = lambda x: x.reshape(c // SUBLANES, SUBLANES, n)
        b3, k3, v3, q3 = blocks(self.b), blocks(self.kk), blocks(self.vb.astype(F32)), blocks(self.qf)
        sub = lax.broadcasted_iota(jnp.int32, (1, SUBLANES, 1), 1) & (HGRN_DIAG - 1)
        diag = jnp.sum(q3 * k3, axis=2, keepdims=True) * v3
        for d in range(1, HGRN_DIAG):
            sh = SUBLANES - d if self.rev else d
            valid = (sub + d <= HGRN_DIAG - 1) if self.rev else (sub >= d)
            e = jnp.exp2(jnp.where(valid, b3 - pltpu.roll(b3, sh, 1), 0.0))
            a = jnp.sum(q3 * e * pltpu.roll(k3, sh, 1), axis=2, keepdims=True)
            diag = diag + jnp.where(valid, a, 0.0) * pltpu.roll(v3, sh, 1)
        self.diag = diag.reshape(c, n)

    def intra(self):
        att = functools.reduce(lambda x, y: x + y, [jnp.where(self.masks[w], a, 0.0) for w, a in self.levels])
        self.intra_out = jnp.dot(att.astype(BF16), self.vb, preferred_element_type=F32)
        self.refs[6][...] = self.st * jnp.exp2(self.bl) + self.st_add

    def finish(self):
        self.refs[4][self.refs[5]] = self.inter + self.intra_out + self.diag


def _hgrn_kernel(qf_ref, bf_ref, kf_ref, vf_ref, qb_ref, bb_ref, kb_ref, vb_ref, of_ref, ob_ref, st_sc,
                 *, bsz, heads):
    s = pl.program_id(0)

    @pl.when(s == 0)
    def _():
        st_sc[...] = jnp.zeros(st_sc.shape, F32)

    c = qf_ref.shape[1]
    fwd, bwd = _hgrn_masks(c, False), _hgrn_masks(c, True)
    chains = []
    for bi in range(bsz):
        for h in range(heads):
            sl = slice(h * HGRN_HEAD, (h + 1) * HGRN_HEAD)
            idx = (bi, slice(None), sl)
            chains.append(_Chain(qf_ref, bf_ref, kf_ref, vf_ref, of_ref, idx, st_sc.at[bi, 0, h], False, fwd))
            chains.append(_Chain(qb_ref, bb_ref, kb_ref, vb_ref, ob_ref, idx, st_sc.at[bi, 1, h], True, bwd))
    for g0 in range(0, len(chains), HGRN_INTERLEAVE):
        group = chains[g0:g0 + HGRN_INTERLEAVE]
        for stage in (_Chain.load, _Chain.products, _Chain.intra, _Chain.diagonal, _Chain.finish):
            for ch in group:
                stage(ch)


def _hgrn(qh, b_f, k_f, b_b, k_b, iv, lc):
    bsz, ltot, d_hgrn = qh.shape
    heads = d_hgrn // HGRN_HEAD
    c = HGRN_CHUNK
    assert lc % c == 0 and ltot % c == 0
    ncc, ntot = lc // c, ltot // c

    def bwd(s):
        return jnp.where(s < ncc, ncc - 1 - s, ntot - 1 + ncc - s)

    fspec = pl.BlockSpec((bsz, c, d_hgrn), lambda s: (0, s, 0))
    bspec = pl.BlockSpec((bsz, c, d_hgrn), lambda s: (0, bwd(s), 0))
    kern = functools.partial(_hgrn_kernel, bsz=bsz, heads=heads)
    return pl.pallas_call(
        kern,
        grid=(ntot,),
        in_specs=[fspec, fspec, fspec, fspec, bspec, bspec, bspec, bspec],
        out_specs=[fspec, bspec],
        out_shape=[jax.ShapeDtypeStruct((bsz, ltot, d_hgrn), F32)] * 2,
        scratch_shapes=[pltpu.VMEM((bsz, 2, heads, HGRN_HEAD, HGRN_HEAD), F32)],
        name="hgrn_scan",
        compiler_params=_cparams(("arbitrary",)),
    )(qh, b_f, k_f, iv, qh, b_b, k_b, iv)


def _outproj_odd_kernel(x_ref, glu_ref, gp_ref, gn_ref, scg_ref, cw_ref, cb_ref, lng_ref, lnb_ref,
                        of_ref, ob_ref, hn_ref, sog_ref, wout_ref, gpost_ref, mod_ref, o_ref, z_sc, zs_sc,
                        *, tm, ncb, lc, ll, d_conv, first):
    i = pl.program_id(1) + first
    _, _, has_prev, has_next = _seq_position(i, tm, ncb, lc, ll)
    h = CONV_HALO
    z_sc[0:h, :] = jnp.where(has_prev, gp_ref[0], 0.0)
    z_sc[h:h + tm, :] = glu_ref[0]
    z_sc[h + tm:, :] = jnp.where(has_next, gn_ref[0], 0.0)
    cw = cw_ref[...]
    base = h - CONV_WIDTH // 2
    span = zs_sc.shape[1]
    for r in range(SUBLANES):
        zs_sc[r] = z_sc[r:r + span, :]
    acc = jnp.zeros((tm, d_conv), F32)
    for j in range(CONV_WIDTH):
        r = (base + j) % SUBLANES
        a0 = base + j - r
        acc = acc + cw[j:j + 1, :] * zs_sc[r, a0:a0 + tm, :]
    zc = acc + cb_ref[...]
    mu = jnp.mean(zc, axis=-1, keepdims=True)
    zc = zc - mu
    var = jnp.mean(zc * zc, axis=-1, keepdims=True)
    zn = zc * lax.rsqrt(var + EPS) * lng_ref[...] + lnb_ref[...]
    yc = _silu(zn) * scg_ref[0].astype(F32)
    o = of_ref[0] + ob_ref[0]
    hn = hn_ref[...]
    parts = []
    for hd in range(o.shape[1] // HGRN_HEAD):
        sl = slice(hd * HGRN_HEAD, (hd + 1) * HGRN_HEAD)
        oh = o[:, sl]
        ms = jnp.mean(oh * oh, axis=-1, keepdims=True)
        parts.append(oh * lax.rsqrt(ms + EPS) * hn[:, sl])
    yd = jnp.concatenate(parts, axis=1) * sog_ref[0].astype(F32)
    y = (jnp.dot(yc.astype(BF16), wout_ref[0:d_conv, :], preferred_element_type=F32)
         + jnp.dot(yd.astype(BF16), wout_ref[d_conv:, :], preferred_element_type=F32))
    _postnorm_residual(x_ref[0], y, gpost_ref, mod_ref, o_ref)


def _outproj_odd(xc, glu, scg, cw, cb, lng, lnb, o_f, o_b, hnorm, sog, wout, gpost, mod, ncb, lc, latent_only):
    bsz, ltot, d = xc.shape
    tm = ROW_TILE
    d_conv = glu.shape[-1]
    d_hgrn = o_f.shape[-1]
    first = ncb if latent_only else 0
    rows, full2, modspec = _row_specs(tm, ncb, first)
    prev, nxt = _halo_specs(tm, CONV_HALO, d_conv, ltot, first)
    conv_span = (CONV_HALO + CONV_WIDTH // 2) // SUBLANES * SUBLANES
    assert SUBLANES - 1 + tm + conv_span <= tm + 2 * CONV_HALO
    kern = functools.partial(_outproj_odd_kernel, tm=tm, ncb=ncb, lc=lc, ll=ltot - lc, d_conv=d_conv, first=first)
    return pl.pallas_call(
        kern,
        grid=(bsz, ltot // tm - first),
        in_specs=[rows(d), rows(d_conv), prev, nxt, rows(d_conv), full2(*cw.shape), full2(1, d_conv),
                  full2(1, d_conv), full2(1, d_conv), rows(d_hgrn), rows(d_hgrn), full2(1, d_hgrn),
                  rows(d_hgrn), full2(*wout.shape), full2(1, d), modspec(d)],
        out_specs=pl.BlockSpec((1, tm, d), lambda b, i: (b, i, 0)),
        out_shape=jax.ShapeDtypeStruct((bsz, ltot - first * tm, d), F32),
        scratch_shapes=[pltpu.VMEM((tm + 2 * CONV_HALO, d_conv), F32),
                        pltpu.VMEM((SUBLANES, tm + conv_span, d_conv), F32)],
        name="outproj_odd",
        compiler_params=_cparams(("parallel", "parallel")),
    )(xc, glu, glu, glu, scg, cw, cb.reshape(1, d_conv), lng.reshape(1, d_conv), lnb.reshape(1, d_conv),
      o_f, o_b, hnorm.reshape(1, d_hgrn), sog, wout, gpost.reshape(1, d), mod)


def _rope_tables(lc, ll):
    t = jnp.arange(ll)
    inv = ROPE_BASE ** (-jnp.arange(ROPE_FREQS, dtype=F32) / ROPE_FREQS)
    ang = jnp.stack([t // GRID_W, t % GRID_W], axis=-1).astype(F32)[:, :, None] * inv
    cos = jnp.cos(ang)
    sin = jnp.sin(ang)
    cos64 = jnp.stack([cos, cos], axis=2).reshape(ll, DIFF_HEAD)
    sin64 = jnp.stack([-sin, sin], axis=2).reshape(ll, DIFF_HEAD)
    cos_t = jnp.concatenate([jnp.ones((lc, DIFF_HEAD), F32), cos64], axis=0)
    sin_t = jnp.concatenate([jnp.zeros((lc, DIFF_HEAD), F32), sin64], axis=0)
    return jnp.tile(cos_t, (1, LANES // DIFF_HEAD)), jnp.tile(sin_t, (1, LANES // DIFF_HEAD))


def _block_diag(w):
    g, a, b = w.shape
    out = jnp.zeros((g * a, g * b), w.dtype)
    for j in range(g):
        out = out.at[j * a:(j + 1) * a, j * b:(j + 1) * b].set(w[j])
    return out


def kernel(x, c, ctx, c_ctx, ada_w, ada_b, norm_pre, norm_post, w_in_even, w_out_even, pool_w, pool_scale,
           diff_lambda, diff_subln, w_in_odd, w_out_odd, conv_w, conv_b, conv_ln_g, conv_ln_b, hgrn_norm,
           hgrn_lb):
    bsz, ll, d = x.shape
    lc = ctx.shape[1]
    depth = ada_w.shape[0]
    d_pool = pool_scale.shape[-1]
    d_diff = w_out_even.shape[1] - d_pool
    d_conv = conv_b.shape[-1]
    d_hgrn = hgrn_norm.shape[-1]
    assert lc % ROW_TILE == 0 and ll % ROW_TILE == 0 and bsz + 1 <= SUBLANES
    ncb = lc // ROW_TILE

    cvec = jnp.zeros((SUBLANES, d), F32).at[:bsz].set(c).at[bsz].set(c_ctx)
    ada = _ada_call(cvec, ada_w, ada_b).reshape(depth, SUBLANES, 3, d)
    xc = (ctx, x)
    cos, sin = _rope_tables(lc, ll)

    for l in range(depth):
        mod = jnp.stack([jnp.broadcast_to(ada[l, bsz], (bsz, 3, d)), ada[l, :bsz]], axis=1)
        j = l // 2
        if l % 2 == 0:
            lam_init = 0.8 - 0.6 * math.exp(-0.3 * l)
            wb = w_in_even[j].astype(BF16)
            q0, k0, v0, g0 = 2 * d_pool, 2 * d_pool + d_diff, 2 * d_pool + 2 * d_diff, 2 * d_pool + 3 * d_diff
            w_rows = jnp.concatenate([wb[:, :q0], wb[:, k0:v0]], axis=1)
            w_cols_t = jnp.concatenate([wb[:, q0:k0], wb[:, v0:g0], wb[:, g0:]], axis=1).T
            u, sga, k, qt, vt, sgbt = _inproj_even(xc, norm_pre[l], mod, w_rows, w_cols_t, cos, sin, ncb,
                                                   d_pool, d_diff)
            ybt = _attention(qt, k, vt, sgbt, diff_lambda[j], diff_subln[j], lc, lam_init)
            xc = _outproj_even(xc, u, sga, ybt, _block_diag(pool_w[j]).astype(BF16), pool_scale[j],
                               w_out_even[j].astype(BF16), norm_post[l], mod, ncb, lc)
        else:
            glu, scg, qh, b_f, k_f, b_b, k_b, iv, sog = _inproj_odd(xc, norm_pre[l], mod, w_in_odd[j].astype(BF16),
                                                                    hgrn_lb, l, ncb, d_conv, d_hgrn)
            o_f, o_b = _hgrn(qh, b_f, k_f, b_b, k_b, iv, lc)
            xc = _outproj_odd(xc, glu, scg, conv_w[j], conv_b[j], conv_ln_g[j], conv_ln_b[j], o_f, o_b,
                              hgrn_norm[j], sog, w_out_odd[j].astype(BF16), norm_post[l], mod, ncb, lc,
                              latent_only=l == depth - 1)
    return xc if depth % 2 == 0 else xc[:, lc:]
```
